```python
import jax, jax.numpy as jnp
from jax import lax
import numpy as np

D_MODEL = 1024
BATCH = 8
SEQ = 2048
DEPTH = 1
DEC_BATCH = 128
DEC_SEQ = 4
PAST_LEN = 16384
PAGE_SIZE = 128

N_META = 16
D_RWKV = D_MODEL
HEAD_SIZE = 64
N_HEADS = D_RWKV // HEAD_SIZE
D_DECAY_LORA = 64
D_AAA_LORA = 64
D_GATE_LORA = 160
D_CONV = D_MODEL
CONV_W = 3
D_FF = 2816
RWKV_PROJ = 3 * D_RWKV + D_DECAY_LORA + D_AAA_LORA + D_GATE_LORA
SC_PROJ = 3 * D_CONV
GATE_PROJ = 2 * D_MODEL
P_TOTAL = RWKV_PROJ + SC_PROJ + GATE_PROJ
RMS_EPS = 1e-6
GN_EPS = 64e-5

kernel_name = "rwkv7_shortconv_gated_hybrid_step"

RWKV_SPLITS = [D_RWKV, 2 * D_RWKV, 3 * D_RWKV, 3 * D_RWKV + D_DECAY_LORA,
               3 * D_RWKV + D_DECAY_LORA + D_AAA_LORA]


def rms_norm(x, g):
    xf = x.astype(jnp.float32)
    y = xf * lax.rsqrt(jnp.mean(xf * xf, axis=-1, keepdims=True) + RMS_EPS)
    return (y * g.astype(jnp.float32)).astype(x.dtype)


def causal_dwconv(u, buf, w):
    T = u.shape[1]
    full = jnp.concatenate([buf.astype(u.dtype), u], axis=1)
    out = full[:, 0:T] * w[0]
    for i in range(1, CONV_W):
        out = out + full[:, i:i + T] * w[i]
    return out, full[:, -(CONV_W - 1):]


def wkv7_scan(S0, r, w, k, v, a, b):
    def step(S, inp):
        r_t, w_t, k_t, v_t, a_t, b_t = inp
        sa = jnp.einsum('bhij,bhj->bhi', S, a_t)
        S = (S * w_t[:, :, None, :] + sa[..., None] * b_t[:, :, None, :]
             + v_t[..., None] * k_t[:, :, None, :])
        y = jnp.einsum('bhij,bhj->bhi', S, r_t)
        return S, y
    seq = tuple(jnp.moveaxis(t, 1, 0) for t in (r, w, k, v, a, b))
    S, ys = lax.scan(step, S0, seq)
    return jnp.moveaxis(ys, 0, 1), S


def hybrid_layer(x, s_wkv, s_shift, s_sc, s_ffn, norm1_g, w_in, b_gate, mu_shift, w0,
                 w_decay_up, a0, w_aaa_up, w_gate_up, k_k, k_a, r_k, lnx_g, lnx_b,
                 w_branch_rwkv, w_branch_sc, conv_sc, w_out, norm2_g, w_up, conv_ffn, w_down):
    Bsz, T, _ = x.shape
    xn = rms_norm(x, norm1_g)
    p = xn @ w_in
    p_rwkv = p[..., :RWKV_PROJ]
    p_sc = p[..., RWKV_PROJ:RWKV_PROJ + SC_PROJ]
    p_gate = p[..., RWKV_PROJ + SC_PROJ:] + b_gate

    prev = jnp.concatenate([s_shift[:, None].astype(p.dtype), p_rwkv[:, :-1]], axis=1)
    xs = p_rwkv + (prev - p_rwkv) * mu_shift
    new_shift = p_rwkv[:, -1]
    r, k, v, xw, xa, xg = jnp.split(xs, RWKV_SPLITS, axis=-1)
    w = -jax.nn.softplus(-(w0 + jnp.tanh(xw) @ w_decay_up)) - 0.5
    a = jax.nn.sigmoid(a0 + xa @ w_aaa_up)
    g = jax.nn.sigmoid(xg) @ w_gate_up
    kk = (k * k_k).reshape(Bsz, T, N_HEADS, HEAD_SIZE).astype(jnp.float32)
    kk = kk / jnp.maximum(jnp.linalg.norm(kk, axis=-1, keepdims=True), 1e-12)
    k = k * (1 + (a - 1) * k_a)
    heads = lambda t: t.reshape(Bsz, T, N_HEADS, HEAD_SIZE).astype(jnp.float32)
    r_h, k_h, v_h, a_h = heads(r), heads(k), heads(v), heads(a)
    decay = jnp.exp(-jnp.exp(heads(w)))
    y, S = wkv7_scan(s_wkv.astype(jnp.float32), r_h, decay, k_h, v_h, -kk, kk * a_h)
    mu = jnp.mean(y, axis=-1, keepdims=True)
    var = jnp.mean(jnp.square(y - mu), axis=-1, keepdims=True)
    yn = ((y - mu) * lax.rsqrt(var + GN_EPS)).reshape(Bsz, T, D_RWKV)
    yn = yn * lnx_g.astype(jnp.float32) + lnx_b.astype(jnp.float32)
    bonus = jnp.sum(r_h * k_h * r_k.astype(jnp.float32), axis=-1, keepdims=True) * v_h
    o_a = ((yn + bonus.reshape(Bsz, T, D_RWKV)).astype(x.dtype) * g) @ w_branch_rwkv

    h, Bg, Cg = jnp.split(p_sc, 3, axis=-1)
    conv_out, new_sc = causal_dwconv(Cg * h, s_sc, conv_sc)
    o_b = (Bg * conv_out) @ w_branch_sc

    ga, gb = jnp.split(jax.nn.sigmoid(p_gate), 2, axis=-1)
    x = x + (ga * o_a + gb * o_b) @ w_out

    xn2 = rms_norm(x, norm2_g)
    up = xn2 @ w_up
    upc, new_ffn = causal_dwconv(up, s_ffn, conv_ffn)
    gate, val = jnp.split(upc, 2, axis=-1)
    x = x + (jax.nn.silu(gate) * val) @ w_down
    return x, S.astype(s_wkv.dtype), new_shift, new_sc, new_ffn


def trunk(x, s_wkv, s_shift, s_sc, s_ffn, layer_params, final_norm_g):
    new_wkv, new_shift, new_sc, new_ffn = [], [], [], []
    for l in range(DEPTH):
        x, a, b, c, d = hybrid_layer(x, s_wkv[l], s_shift[l], s_sc[l], s_ffn[l],
                                     *[prm[l] for prm in layer_params])
        new_wkv.append(a); new_shift.append(b); new_sc.append(c); new_ffn.append(d)
    y = rms_norm(x, final_norm_g)
    return y, jnp.stack(new_wkv), jnp.stack(new_shift), jnp.stack(new_sc), jnp.stack(new_ffn)


def setup_inputs(seed: int = 0) -> dict:
    key = jax.random.key(seed)
    ks = iter(jax.random.split(key, 40))
    nrm = lambda shape, s: jax.random.normal(next(ks), shape, jnp.float32) * s
    L = DEPTH
    return {
        "x_prompt": nrm((BATCH, SEQ, D_MODEL), 1.0),
        "x_sample": nrm((DEC_BATCH, DEC_SEQ, D_MODEL), 1.0),
        "state_wkv": nrm((L, DEC_BATCH, N_HEADS, HEAD_SIZE, HEAD_SIZE), 0.1),
        "state_shift": nrm((L, DEC_BATCH, RWKV_PROJ), 1.0),
        "state_sc_conv": nrm((L, DEC_BATCH, CONV_W - 1, D_CONV), 1.0),
        "state_ffn_conv": nrm((L, DEC_BATCH, CONV_W - 1, 2 * D_FF), 1.0),
        "meta_tokens": nrm((N_META, D_MODEL), 1.0),
        "norm1_g": 1.0 + nrm((L, D_MODEL), 0.02),
        "w_in": nrm((L, D_MODEL, P_TOTAL), D_MODEL ** -0.5),
        "b_gate": nrm((L, GATE_PROJ), 0.02),
        "mu_shift": jax.random.uniform(next(ks), (L, RWKV_PROJ), jnp.float32),
        "w0": nrm((L, D_RWKV), 0.5) - 0.5,
        "w_decay_up": nrm((L, D_DECAY_LORA, D_RWKV), 0.1 * D_DECAY_LORA ** -0.5),
        "a0": nrm((L, D_RWKV), 0.1),
        "w_aaa_up": nrm((L, D_AAA_LORA, D_RWKV), 0.1 * D_AAA_LORA ** -0.5),
        "w_gate_up": nrm((L, D_GATE_LORA, D_RWKV), D_GATE_LORA ** -0.5),
        "k_k": 0.85 + nrm((L, D_RWKV), 0.02),
        "k_a": 1.0 + nrm((L, D_RWKV), 0.02),
        "r_k": nrm((L, N_HEADS, HEAD_SIZE), 0.1),
        "lnx_g": 1.0 + nrm((L, D_RWKV), 0.02),
        "lnx_b": nrm((L, D_RWKV), 0.02),
        "w_branch_rwkv": nrm((L, D_RWKV, D_MODEL), D_RWKV ** -0.5),
        "w_branch_sc": nrm((L, D_CONV, D_MODEL), D_CONV ** -0.5),
        "conv_sc": nrm((L, CONV_W, D_CONV), CONV_W ** -0.5),
        "w_out": nrm((L, D_MODEL, D_MODEL), 0.5 * D_MODEL ** -0.5),
        "norm2_g": 1.0 + nrm((L, D_MODEL), 0.02),
        "w_up": nrm((L, D_MODEL, 2 * D_FF), D_MODEL ** -0.5),
        "conv_ffn": nrm((L, CONV_W, 2 * D_FF), CONV_W ** -0.5),
        "w_down": nrm((L, D_FF, D_MODEL), 0.5 * D_FF ** -0.5),
        "final_norm_g": 1.0 + nrm((D_MODEL,), 0.02),
    }


def reference(x_prompt, x_sample, state_wkv, state_shift, state_sc_conv, state_ffn_conv,
              meta_tokens, norm1_g, w_in, b_gate, mu_shift, w0, w_decay_up, a0, w_aaa_up,
              w_gate_up, k_k, k_a, r_k, lnx_g, lnx_b, w_branch_rwkv, w_branch_sc, conv_sc,
              w_out, norm2_g, w_up, conv_ffn, w_down, final_norm_g):
    layer_params = (norm1_g, w_in, b_gate, mu_shift, w0, w_decay_up, a0, w_aaa_up, w_gate_up,
                    k_k, k_a, r_k, lnx_g, lnx_b, w_branch_rwkv, w_branch_sc, conv_sc, w_out,
                    norm2_g, w_up, conv_ffn, w_down)
    dt = x_prompt.dtype
    Bp = x_prompt.shape[0]
    meta = jnp.broadcast_to(meta_tokens.astype(dt)[None], (Bp, N_META, D_MODEL))
    xp = jnp.concatenate([meta, x_prompt], axis=1)
    z_wkv = jnp.zeros((DEPTH, Bp, N_HEADS, HEAD_SIZE, HEAD_SIZE), dt)
    z_shift = jnp.zeros((DEPTH, Bp, RWKV_PROJ), dt)
    z_sc = jnp.zeros((DEPTH, Bp, CONV_W - 1, D_CONV), dt)
    z_ffn = jnp.zeros((DEPTH, Bp, CONV_W - 1, 2 * D_FF), dt)
    yp, wkv_p, shift_p, sc_p, ffn_p = trunk(xp, z_wkv, z_shift, z_sc, z_ffn, layer_params, final_norm_g)
    y_prompt = yp[:, N_META:]
    y_sample, wkv_s, shift_s, sc_s, ffn_s = trunk(x_sample, state_wkv, state_shift, state_sc_conv,
                                                  state_ffn_conv, layer_params, final_norm_g)
    return (y_prompt, y_sample, wkv_p, wkv_s, shift_p, shift_s, sc_p, sc_s, ffn_p, ffn_s)
```

```python
import functools

import jax
import jax.numpy as jnp
from jax import lax
from jax.experimental import pallas as pl
from jax.experimental.pallas import tpu as pltpu

D_MODEL = 1024
N_META = 16
HEAD_SIZE = 64
N_HEADS = D_MODEL // HEAD_SIZE
LANES = 128
N_PAIRS = D_MODEL // LANES
D_DECAY_LORA = 64
D_AAA_LORA = 64
D_GATE_LORA = 160
D_LORA_PAD = 512
D_FF = 2816
FF_CHUNK = 256
CONV_W = 3
RMS_EPS = 1e-6
GN_EPS = 64e-5
VMEM_LIMIT = 60 * 1024 * 1024

F32 = jnp.float32
BF16 = jnp.bfloat16


def _dot(a, b):
    return jnp.dot(a.astype(BF16), b.astype(BF16), preferred_element_type=F32)


def _dot_nt(a, b):
    return lax.dot_general(a.astype(BF16), b.astype(BF16), (((1,), (1,)), ((), ())),
                           preferred_element_type=F32)


def _dot_tn(a, b):
    return lax.dot_general(a.astype(BF16), b.astype(BF16), (((0,), (0,)), ((), ())),
                           preferred_element_type=F32)


def _split2(x):
    hi = x.astype(BF16)
    lo = (x - hi.astype(F32)).astype(BF16)
    return hi, lo


def _head_ones(n):
    r = lax.broadcasted_iota(jnp.int32, (n, n), 0) // HEAD_SIZE
    c = lax.broadcasted_iota(jnp.int32, (n, n), 1) // HEAD_SIZE
    return jnp.where(r == c, 1.0, 0.0).astype(BF16)


def _head_sum(x):
    ones = _head_ones(256)
    hi, lo = _split2(x)
    outs = []
    for g in range(D_MODEL // 256):
        sl = slice(256 * g, 256 * (g + 1))
        outs.append(jnp.dot(hi[:, sl], ones, preferred_element_type=F32)
                    + jnp.dot(lo[:, sl], ones, preferred_element_type=F32))
    return jnp.concatenate(outs, axis=1)


def _rms_norm(x, g):
    return x * lax.rsqrt(jnp.mean(x * x, axis=-1, keepdims=True) + RMS_EPS) * g


def _sigmoid(x):
    return 1.0 / (1.0 + jnp.exp(-x))


def _shift_rows(cur, carry, s):
    rows = cur.shape[0]
    if s % 8 == 0:
        return jnp.concatenate([carry, cur[:rows - s]], axis=0)
    assert s == 1
    rolled = pltpu.roll(cur, 1, axis=0)
    row = lax.broadcasted_iota(jnp.int32, cur.shape, 0)
    return jnp.where(row == 0, carry, rolled)


def _rwkv_prep_kernel(x_ref, sh_rkv_ref, sh_lora_ref, n1g_ref, w_rkv_ref, w_lora_ref, mu_rkv_ref,
                      mu_lora_ref, w0_ref, wd_ref, a0_ref, wa_ref, wg_ref, kk_ref, ka_ref, rk_ref,
                      r_out, lw_out, k_out, v_out, kk_out, b_out, g_out, bonus_out, shr_out, shl_out,
                      c_rkv, c_lora, *, s):
    @pl.when(pl.program_id(1) == 0)
    def _():
        c_rkv[...] = sh_rkv_ref[...]
        c_lora[...] = sh_lora_ref[...]

    rows = x_ref.shape[0]
    xb = _rms_norm(x_ref[...], n1g_ref[...]).astype(BF16)
    p_rkv = jnp.dot(xb, w_rkv_ref[...], preferred_element_type=F32)
    p_lora = jnp.dot(xb, w_lora_ref[...], preferred_element_type=F32)
    prev_rkv = _shift_rows(p_rkv, c_rkv[...], s)
    prev_lora = _shift_rows(p_lora, c_lora[...], s)
    xs = p_rkv + (prev_rkv - p_rkv) * mu_rkv_ref[...]
    xl = p_lora + (prev_lora - p_lora) * mu_lora_ref[...]
    new_rkv = p_rkv[rows - s:]
    new_lora = p_lora[rows - s:]
    c_rkv[...] = new_rkv
    c_lora[...] = new_lora
    shr_out[...] = new_rkv
    shl_out[...] = new_lora

    r = xs[:, :D_MODEL]
    k = xs[:, D_MODEL:2 * D_MODEL]
    v = xs[:, 2 * D_MODEL:]
    xw = xl[:, :128]
    xa = xl[:, 128:256]
    xg = xl[:, 256:]

    zw = w0_ref[...] + _dot(jnp.tanh(xw), wd_ref[...])
    wlog = -(jnp.maximum(-zw, 0.0) + jnp.log(1.0 + jnp.exp(-jnp.abs(zw)))) - 0.5
    lw_out[...] = -jnp.exp(wlog)
    a = _sigmoid(a0_ref[...] + _dot(xa, wa_ref[...]))
    g_out[...] = _dot(_sigmoid(xg), wg_ref[...])

    kkr = k * kk_ref[...]
    kk = kkr * lax.rsqrt(jnp.maximum(_head_sum(kkr * kkr), 1e-24))
    k2 = k * (1.0 + (a - 1.0) * ka_ref[...])
    r_out[...] = r
    k_out[...] = k2
    v_out[...] = v
    kk_out[...] = kk
    b_out[...] = kk * a
    bonus_out[...] = _head_sum(r * k2 * rk_ref[...]) * v


def _sc_gate_kernel(x_ref, sc_state_ref, n1g_ref, w_sc_ref, w_gate_ref, b_gate_ref, conv_ref,
                    w_bsc_ref, ga_out, gbo_out, sc_out, c_sc, *, s):
    @pl.when(pl.program_id(1) == 0)
    def _():
        c_sc[...] = sc_state_ref[...]

    rows = x_ref.shape[0]
    xb = _rms_norm(x_ref[...], n1g_ref[...]).astype(BF16)
    p_sc = jnp.dot(xb, w_sc_ref[...], preferred_element_type=F32)
    u = p_sc[:, 2 * D_MODEL:] * p_sc[:, :D_MODEL]
    carry = c_sc[...]
    prev1 = _shift_rows(u, carry[s:], s)
    prev2 = _shift_rows(prev1, carry[:s], s)
    cw = conv_ref[...]
    conv = prev2 * cw[0:1] + prev1 * cw[1:2] + u * cw[2:3]
    o_b = _dot(p_sc[:, D_MODEL:2 * D_MODEL] * conv, w_bsc_ref[...])
    new = u[rows - 2 * s:]
    c_sc[...] = new
    sc_out[...] = new

    pg = jnp.dot(xb, w_gate_ref[...], preferred_element_type=F32) + b_gate_ref[...]
    ga_out[...] = _sigmoid(pg[:, :D_MODEL])
    gbo_out[...] = _sigmoid(pg[:, D_MODEL:]) * o_b


def _wkv_kernel(r_ref, lw_ref, k_ref, v_ref, kk_ref, b_ref, s0_ref, y_out, s_out, s_scr, *, C):
    c_idx = pl.program_id(1)

    @pl.when(c_idx == 0)
    def _():
        s_scr[...] = s0_ref[...]

    lw = lw_ref[...]
    ti = lax.broadcasted_iota(jnp.int32, (C, C), 0)
    tj = lax.broadcasted_iota(jnp.int32, (C, C), 1)
    tri = jnp.where(tj <= ti, 1.0, 0.0).astype(BF16)
    h1 = lw.astype(BF16)
    r1 = lw - h1.astype(F32)
    h2 = r1.astype(BF16)
    h3 = (r1 - h2.astype(F32)).astype(BF16)
    cum = (jnp.dot(tri, h1, preferred_element_type=F32) + jnp.dot(tri, h2, preferred_element_type=F32)
           + jnp.dot(tri, h3, preferred_element_type=F32))
    w_t = jnp.exp(cum)
    w_inv = jnp.exp(-cum)
    a_t = -kk_ref[...] * jnp.exp(cum - lw)
    b_t = b_ref[...] * w_inv
    k_t = k_ref[...] * w_inv
    r_t = r_ref[...] * w_t
    v_all = v_ref[...]
    w_last = w_t[C - 1:C]

    m0 = lax.broadcasted_iota(jnp.int32, (C, LANES), 1) < HEAD_SIZE
    gi = lax.broadcasted_iota(jnp.int32, (C, 2 * C), 0)
    gj = lax.broadcasted_iota(jnp.int32, (C, 2 * C), 1)
    gjm = jnp.where(gj >= C, gj - C, gj)
    strict = gjm < gi
    incl = gjm <= gi
    strict_l = gj < gi
    strict_r = jnp.logical_and(gj >= C, gj - C < gi)
    ei = lax.broadcasted_iota(jnp.int32, (2 * C, 2 * C), 0)
    ej = lax.broadcasted_iota(jnp.int32, (2 * C, 2 * C), 1)
    eye = jnp.where(ei == ej, 1.0, 0.0)
    bi = lax.broadcasted_iota(jnp.int32, (LANES, LANES), 0) // HEAD_SIZE
    bj = lax.broadcasted_iota(jnp.int32, (LANES, LANES), 1) // HEAD_SIZE
    bd = bi == bj
    zc = jnp.zeros((C, LANES), F32)

    for p in range(N_PAIRS):
        sl = slice(LANES * p, LANES * (p + 1))
        a_p, r_p, b_p, k_p, v_p = a_t[:, sl], r_t[:, sl], b_t[:, sl], k_t[:, sl], v_all[:, sl]
        bk = jnp.concatenate([b_p, k_p], axis=0)
        kb = jnp.concatenate([k_p, b_p], axis=0)
        lhs0 = jnp.concatenate([jnp.where(m0, a_p, 0.0), jnp.where(m0, r_p, 0.0)], axis=0)
        lhs1 = jnp.concatenate([jnp.where(m0, 0.0, a_p), jnp.where(m0, 0.0, r_p)], axis=0)
        g0 = _dot_nt(lhs0, bk)
        g1 = _dot_nt(lhs1, kb)
        g0a, g0r, g1a, g1r = g0[:C], g0[C:], g1[:C], g1[C:]
        l_bd = jnp.concatenate([jnp.where(strict_l, g0a, 0.0), jnp.where(strict_r, g1a, 0.0)], axis=0)
        xpow = l_bd
        tinv = eye + l_bd
        n = 1
        while 2 * n < C:
            xpow = _dot(xpow, xpow)
            tinv = tinv + _dot(tinv, xpow)
            n *= 2

        s_bd = s_scr[p]
        asrs = _dot_nt(jnp.concatenate([a_p, r_p], axis=0), s_bd)
        vm0 = jnp.where(m0, v_p, 0.0)
        vm1 = jnp.where(m0, 0.0, v_p)
        rhs = (asrs[:C] + _dot(jnp.where(strict, g0a, 0.0), jnp.concatenate([zc, vm0], axis=0))
               + _dot(jnp.where(strict, g1a, 0.0), jnp.concatenate([vm1, zc], axis=0)))
        uu = _dot(tinv, jnp.concatenate([jnp.where(m0, rhs, 0.0), jnp.where(m0, 0.0, rhs)], axis=0))
        u = uu[:C] + uu[C:]
        um0 = jnp.where(m0, u, 0.0)
        um1 = jnp.where(m0, 0.0, u)
        y = (asrs[C:] + _dot(jnp.where(incl, g0r, 0.0), jnp.concatenate([um0, vm0], axis=0))
             + _dot(jnp.where(incl, g1r, 0.0), jnp.concatenate([vm1, um1], axis=0)))
        y_out[:, sl] = y
        upd = _dot_tn(jnp.concatenate([u, v_p], axis=0), bk)
        s_scr[p] = (s_bd + jnp.where(bd, upd, 0.0)) * w_last[:, sl]

    @pl.when(c_idx == pl.num_programs(1) - 1)
    def _():
        s_out[...] = s_scr[...]


def _post_kernel(x_ref, y_ref, bonus_ref, g_ref, ga_ref, gbo_ref, ffn_state_ref, lnx_g_ref, lnx_b_ref,
                 w_br_ref, w_out_ref, n2g_ref, w_up_ref, conv_ref, w_down_ref, fng_ref,
                 out_ref, ffn_out, c_ffn, *, s):
    @pl.when(pl.program_id(1) == 0)
    def _():
        c_ffn[...] = ffn_state_ref[...]

    rows = x_ref.shape[0]
    y = y_ref[...]
    dev = y - _head_sum(y) * (1.0 / HEAD_SIZE)
    var = _head_sum(dev * dev) * (1.0 / HEAD_SIZE)
    yn = dev * lax.rsqrt(var + GN_EPS) * lnx_g_ref[...] + lnx_b_ref[...]
    o_a = _dot((yn + bonus_ref[...]) * g_ref[...], w_br_ref[...])
    x1 = x_ref[...] + _dot(ga_ref[...] * o_a + gbo_ref[...], w_out_ref[...])

    xb = _rms_norm(x1, n2g_ref[...]).astype(BF16)
    acc = jnp.zeros((rows, D_MODEL), F32)
    for j in range(D_FF // FF_CHUNK):
        halves = []
        for base in (0, D_FF):
            cols = slice(base + FF_CHUNK * j, base + FF_CHUNK * (j + 1))
            up = jnp.dot(xb, w_up_ref[:, cols], preferred_element_type=F32)
            carry = c_ffn[:, cols]
            prev1 = _shift_rows(up, carry[s:], s)
            prev2 = _shift_rows(prev1, carry[:s], s)
            cw = conv_ref[:, cols]
            halves.append(prev2 * cw[0:1] + prev1 * cw[1:2] + up * cw[2:3])
            new = up[rows - 2 * s:]
            c_ffn[:, cols] = new
            ffn_out[:, cols] = new
        gate, val = halves
        acc = acc + _dot(gate * _sigmoid(gate) * val, w_down_ref[FF_CHUNK * j:FF_CHUNK * (j + 1), :])
    out_ref[...] = _rms_norm(x1 + acc, fng_ref[...])


def _const_spec(arr):
    nd = arr.ndim
    return pl.BlockSpec(arr.shape, lambda b, t: (0,) * nd, pipeline_mode=pl.Buffered(1))


def _row_spec(rows, cols):
    return pl.BlockSpec((None, rows, cols), lambda b, t: (b, t, 0))


def _state_spec(rows, cols):
    return pl.BlockSpec((None, rows, cols), lambda b, t: (b, 0, 0))


def _params():
    return pltpu.CompilerParams(dimension_semantics=("arbitrary", "arbitrary"),
                                vmem_limit_bytes=VMEM_LIMIT)


def _rwkv_prep(x, sh_rkv, sh_lora, w, *, rows, s):
    nb, total, _ = x.shape
    grid = (nb, total // rows)
    consts = [w["norm1_g"], w["w_rkv"], w["w_lora"], w["mu_rkv"], w["mu_lora"], w["w0"], w["wd"],
              w["a0"], w["wa"], w["wg"], w["k_k"], w["k_a"], w["r_k"]]
    tok = jax.ShapeDtypeStruct((nb, total, D_MODEL), F32)
    return pl.pallas_call(
        functools.partial(_rwkv_prep_kernel, s=s),
        grid=grid,
        in_specs=[_row_spec(rows, D_MODEL), _state_spec(s, 3 * D_MODEL), _state_spec(s, D_LORA_PAD)]
                 + [_const_spec(c) for c in consts],
        out_specs=[_row_spec(rows, D_MODEL)] * 8
                  + [_state_spec(s, 3 * D_MODEL), _state_spec(s, D_LORA_PAD)],
        out_shape=[tok] * 8 + [jax.ShapeDtypeStruct((nb, s, 3 * D_MODEL), F32),
                               jax.ShapeDtypeStruct((nb, s, D_LORA_PAD), F32)],
        scratch_shapes=[pltpu.VMEM((s, 3 * D_MODEL), F32), pltpu.VMEM((s, D_LORA_PAD), F32)],
        compiler_params=_params(),
        name="rwkv_prep",
    )(x, sh_rkv, sh_lora, *consts)


def _sc_gate(x, sc_state, w, *, rows, s):
    nb, total, _ = x.shape
    grid = (nb, total // rows)
    consts = [w["norm1_g"], w["w_sc"], w["w_gate"], w["b_gate"], w["conv_sc"], w["w_branch_sc"]]
    tok = jax.ShapeDtypeStruct((nb, total, D_MODEL), F32)
    return pl.pallas_call(
        functools.partial(_sc_gate_kernel, s=s),
        grid=grid,
        in_specs=[_row_spec(rows, D_MODEL), _state_spec(2 * s, D_MODEL)] + [_const_spec(c) for c in consts],
        out_specs=[_row_spec(rows, D_MODEL)] * 2 + [_state_spec(2 * s, D_MODEL)],
        out_shape=[tok] * 2 + [jax.ShapeDtypeStruct((nb, 2 * s, D_MODEL), F32)],
        scratch_shapes=[pltpu.VMEM((2 * s, D_MODEL), F32)],
        compiler_params=_params(),
        name="sc_gate",
    )(x, sc_state, *consts)


def _wkv(r, lw, k, v, kk, b, s0_bd, *, chunk):
    nb, total, _ = r.shape
    grid = (nb, total // chunk)
    tok_spec = pl.BlockSpec((None, chunk, D_MODEL), lambda i, c: (i, c, 0))
    st_spec = pl.BlockSpec((None, N_PAIRS, LANES, LANES), lambda i, c: (i, 0, 0, 0))
    return pl.pallas_call(
        functools.partial(_wkv_kernel, C=chunk),
        grid=grid,
        in_specs=[tok_spec] * 6 + [st_spec],
        out_specs=[tok_spec, st_spec],
        out_shape=[jax.ShapeDtypeStruct((nb, total, D_MODEL), F32),
                   jax.ShapeDtypeStruct((nb, N_PAIRS, LANES, LANES), F32)],
        scratch_shapes=[pltpu.VMEM((N_PAIRS, LANES, LANES), F32)],
        compiler_params=_params(),
        name="wkv",
    )(r, lw, k, v, kk, b, s0_bd)


def _post(x, y, bonus, g, ga, gbo, ffn_state, w, *, rows, s):
    nb, total, _ = x.shape
    grid = (nb, total // rows)
    consts = [w["lnx_g"], w["lnx_b"], w["w_branch_rwkv"], w["w_out"], w["norm2_g"], w["w_up"],
              w["conv_ffn"], w["w_down"], w["final_norm_g"]]
    return pl.pallas_call(
        functools.partial(_post_kernel, s=s),
        grid=grid,
        in_specs=[_row_spec(rows, D_MODEL)] * 6 + [_state_spec(2 * s, 2 * D_FF)]
                 + [_const_spec(c) for c in consts],
        out_specs=[_row_spec(rows, D_MODEL), _state_spec(2 * s, 2 * D_FF)],
        out_shape=[jax.ShapeDtypeStruct((nb, total, D_MODEL), F32),
                   jax.ShapeDtypeStruct((nb, 2 * s, 2 * D_FF), F32)],
        scratch_shapes=[pltpu.VMEM((2 * s, 2 * D_FF), F32)],
        compiler_params=_params(),
        name="post",
    )(x, y, bonus, g, ga, gbo, ffn_state, *consts)


def _pack_bd(s):
    nb = s.shape[0]
    s = s.reshape(nb, N_PAIRS, 2, HEAD_SIZE, HEAD_SIZE)
    z = jnp.zeros_like(s[:, :, 0])
    top = jnp.concatenate([s[:, :, 0], z], axis=-1)
    bot = jnp.concatenate([z, s[:, :, 1]], axis=-1)
    return jnp.concatenate([top, bot], axis=-2)


def _unpack_bd(s_bd):
    nb = s_bd.shape[0]
    s0 = s_bd[:, :, :HEAD_SIZE, :HEAD_SIZE]
    s1 = s_bd[:, :, HEAD_SIZE:, HEAD_SIZE:]
    return jnp.stack([s0, s1], axis=2).reshape(nb, N_HEADS, HEAD_SIZE, HEAD_SIZE)


def _pad_lora_cols(a):
    pad = lambda t, n: jnp.pad(t, [(0, 0)] * (t.ndim - 1) + [(0, n - t.shape[-1])])
    return jnp.concatenate([pad(a[..., :64], 128), pad(a[..., 64:128], 128), pad(a[..., 128:], 256)], axis=-1)


def _unpad_lora_cols(a):
    return jnp.concatenate([a[..., :64], a[..., 128:192], a[..., 256:256 + D_GATE_LORA]], axis=-1)


def _prep_weights(norm1_g, w_in, b_gate, mu_shift, w0, w_decay_up, a0, w_aaa_up, w_gate_up, k_k, k_a,
                  r_k, lnx_g, lnx_b, w_branch_rwkv, w_branch_sc, conv_sc, w_out, norm2_g, w_up, conv_ffn,
                  w_down, final_norm_g):
    row = lambda t: t.reshape(1, -1).astype(F32)
    d3 = 3 * D_MODEL
    n_lora = D_DECAY_LORA + D_AAA_LORA + D_GATE_LORA
    pad_rows = lambda t, n: jnp.pad(t, [(0, n - t.shape[0]), (0, 0)])
    return {
        "norm1_g": row(norm1_g),
        "w_rkv": w_in[:, :d3].astype(BF16),
        "w_lora": _pad_lora_cols(w_in[:, d3:d3 + n_lora]).astype(BF16),
        "w_sc": w_in[:, d3 + n_lora:2 * d3 + n_lora].astype(BF16),
        "w_gate": w_in[:, 2 * d3 + n_lora:].astype(BF16),
        "b_gate": row(b_gate),
        "mu_rkv": row(mu_shift[:d3]),
        "mu_lora": _pad_lora_cols(row(mu_shift[d3:])),
        "w0": row(w0),
        "wd": pad_rows(w_decay_up, 128).astype(BF16),
        "a0": row(a0),
        "wa": pad_rows(w_aaa_up, 128).astype(BF16),
        "wg": pad_rows(w_gate_up, 256).astype(BF16),
        "k_k": row(k_k), "k_a": row(k_a), "r_k": row(r_k),
        "lnx_g": row(lnx_g), "lnx_b": row(lnx_b),
        "w_branch_rwkv": w_branch_rwkv.astype(BF16),
        "w_branch_sc": w_branch_sc.astype(BF16),
        "conv_sc": conv_sc.astype(F32),
        "w_out": w_out.astype(BF16),
        "norm2_g": row(norm2_g),
        "w_up": w_up.astype(BF16),
        "conv_ffn": conv_ffn.astype(F32),
        "w_down": w_down.astype(BF16),
        "final_norm_g": row(final_norm_g),
    }


def _layer(x, s_bd, sh_rkv, sh_lora, sc_state, ffn_state, w, *, rows, s, to_seq, from_seq, chunk):
    r, lw, k, v, kk, b, g, bonus, shr, shl = _rwkv_prep(x, sh_rkv, sh_lora, w, rows=rows, s=s)
    ga, gbo, sc_new = _sc_gate(x, sc_state, w, rows=rows, s=s)
    y_seq, s_new = _wkv(*[to_seq(t) for t in (r, lw, k, v, kk, b)], s_bd, chunk=chunk)
    out, ffn_new = _post(x, from_seq(y_seq), bonus, g, ga, gbo, ffn_state, w, rows=rows, s=s)
    return out, s_new, shr, shl, sc_new, ffn_new


def kernel(x_prompt, x_sample, state_wkv, state_shift, state_sc_conv, state_ffn_conv, meta_tokens,
           norm1_g, w_in, b_gate, mu_shift, w0, w_decay_up, a0, w_aaa_up, w_gate_up, k_k, k_a, r_k,
           lnx_g, lnx_b, w_branch_rwkv, w_branch_sc, conv_sc, w_out, norm2_g, w_up, conv_ffn, w_down,
           final_norm_g):
    w = _prep_weights(norm1_g[0], w_in[0], b_gate[0], mu_shift[0], w0[0], w_decay_up[0], a0[0],
                      w_aaa_up[0], w_gate_up[0], k_k[0], k_a[0], r_k[0], lnx_g[0], lnx_b[0],
                      w_branch_rwkv[0], w_branch_sc[0], conv_sc[0], w_out[0], norm2_g[0], w_up[0],
                      conv_ffn[0], w_down[0], final_norm_g)
    d3 = 3 * D_MODEL
    ident = lambda t: t

    bp, seq, _ = x_prompt.shape
    zeros = lambda *shape: jnp.zeros(shape, F32)
    _, m_wkv, m_shr, m_shl, m_sc, m_ffn = _layer(
        meta_tokens.astype(F32)[None], zeros(1, N_PAIRS, LANES, LANES), zeros(1, 1, d3),
        zeros(1, 1, D_LORA_PAD), zeros(1, 2, D_MODEL), zeros(1, 2, 2 * D_FF), w,
        rows=N_META, s=1, to_seq=ident, from_seq=ident, chunk=N_META)

    rep = lambda t: jnp.broadcast_to(t, (bp,) + t.shape[1:])
    y_prompt, p_wkv, p_shr, p_shl, p_sc, p_ffn = _layer(
        x_prompt, rep(m_wkv), rep(m_shr), rep(m_shl), rep(m_sc), rep(m_ffn), w,
        rows=256, s=1, to_seq=ident, from_seq=ident, chunk=64)

    bs, ts, _ = x_sample.shape
    n_grp = 2
    gsz = bs // n_grp
    t_pad = 8

    def to_rows(t):
        n, c = t.shape[1:]
        return t.reshape(n_grp, gsz, n, c).transpose(0, 2, 1, 3).reshape(n_grp, n * gsz, c)

    def to_seq(t):
        c = t.shape[-1]
        t = t.reshape(n_grp, ts, gsz, c).transpose(0, 2, 1, 3).reshape(bs, ts, c)
        return jnp.pad(t, ((0, 0), (0, t_pad - ts), (0, 0)))

    def from_seq(t):
        return to_rows(t[:, :ts])

    conv_rows = lambda st: to_rows(st)
    sh = state_shift[0].reshape(n_grp, gsz, -1)
    y_s, s_wkv, s_shr, s_shl, s_sc, s_ffn = _layer(
        to_rows(x_sample), _pack_bd(state_wkv[0]), sh[..., :d3], _pad_lora_cols(sh[..., d3:]),
        conv_rows(state_sc_conv[0]), conv_rows(state_ffn_conv[0]), w,
        rows=ts * gsz, s=gsz, to_seq=to_seq, from_seq=from_seq, chunk=t_pad)

    def rows_to_batch(t, n):
        c = t.shape[-1]
        return t.reshape(n_grp, n, gsz, c).transpose(0, 2, 1, 3).reshape(bs, n, c)

    y_sample = rows_to_batch(y_s, ts)
    shift_p = jnp.concatenate([p_shr[:, 0], _unpad_lora_cols(p_shl[:, 0])], axis=-1)
    shift_s = jnp.concatenate([rows_to_batch(s_shr, 1)[:, 0], _unpad_lora_cols(rows_to_batch(s_shl, 1)[:, 0])],
                              axis=-1)
    return (y_prompt, y_sample,
            _unpack_bd(p_wkv)[None], _unpack_bd(s_wkv)[None],
            shift_p[None], shift_s[None],
            p_sc[None], rows_to_batch(s_sc, 2)[None],
            p_ffn[None], rows_to_batch(s_ffn, 2)[None])
```

```python
import functools

import jax
import jax.numpy as jnp
from jax import lax
from jax.experimental import pallas as pl
from jax.experimental.pallas import tpu as pltpu

D_MODEL = 1024
N_META = 16
HEAD_SIZE = 64
N_HEADS = D_MODEL // HEAD_SIZE
LANES = 128
N_PAIRS = D_MODEL // LANES
D_DECAY_LORA = 64
D_AAA_LORA = 64
D_GATE_LORA = 160
D_LORA_PAD = 512
D_FF = 2816
FF_CHUNK = 256
CONV_W = 3
RMS_EPS = 1e-6
GN_EPS = 64e-5
VMEM_LIMIT = 60 * 1024 * 1024

F32 = jnp.float32
BF16 = jnp.bfloat16


def _dot(a, b):
    return jnp.dot(a.astype(BF16), b.astype(BF16), preferred_element_type=F32)


def _dot_nt(a, b):
    return lax.dot_general(a.astype(BF16), b.astype(BF16), (((1,), (1,)), ((), ())),
                           preferred_element_type=F32)


def _dot_tn(a, b):
    return lax.dot_general(a.astype(BF16), b.astype(BF16), (((0,), (0,)), ((), ())),
                           preferred_element_type=F32)


def _split2(x):
    hi = x.astype(BF16)
    lo = (x - hi.astype(F32)).astype(BF16)
    return hi, lo


def _head_ones(n):
    r = lax.broadcasted_iota(jnp.int32, (n, n), 0) // HEAD_SIZE
    c = lax.broadcasted_iota(jnp.int32, (n, n), 1) // HEAD_SIZE
    return jnp.where(r == c, 1.0, 0.0).astype(BF16)


def _head_sum(x):
    ones = _head_ones(256)
    hi, lo = _split2(x)
    outs = []
    for g in range(D_MODEL // 256):
        sl = slice(256 * g, 256 * (g + 1))
        outs.append(jnp.dot(hi[:, sl], ones, preferred_element_type=F32)
                    + jnp.dot(lo[:, sl], ones, preferred_element_type=F32))
    return jnp.concatenate(outs, axis=1)


def _rms_norm(x, g):
    return x * lax.rsqrt(jnp.mean(x * x, axis=-1, keepdims=True) + RMS_EPS) * g


def _sigmoid(x):
    return 1.0 / (1.0 + jnp.exp(-x))


def _shift_rows(cur, carry, s):
    rows = cur.shape[0]
    if s % 8 == 0:
        return jnp.concatenate([carry, cur[:rows - s]], axis=0)
    assert s == 1
    rolled = pltpu.roll(cur, 1, axis=0)
    row = lax.broadcasted_iota(jnp.int32, cur.shape, 0)
    return jnp.where(row == 0, carry, rolled)


def _rwkv_prep_kernel(x_ref, sh_rkv_ref, sh_lora_ref, n1g_ref, w_rkv_ref, w_lora_ref, mu_rkv_ref,
                      mu_lora_ref, w0_ref, wd_ref, a0_ref, wa_ref, wg_ref, kk_ref, ka_ref, rk_ref,
                      r_out, lw_out, k_out, v_out, kk_out, b_out, g_out, bonus_out, shr_out, shl_out,
                      c_rkv, c_lora, *, s):
    @pl.when(pl.program_id(1) == 0)
    def _():
        c_rkv[...] = sh_rkv_ref[...]
        c_lora[...] = sh_lora_ref[...]

    rows = x_ref.shape[0]
    xb = _rms_norm(x_ref[...], n1g_ref[...]).astype(BF16)
    p_rkv = jnp.dot(xb, w_rkv_ref[...], preferred_element_type=F32)
    p_lora = jnp.dot(xb, w_lora_ref[...], preferred_element_type=F32)
    prev_rkv = _shift_rows(p_rkv, c_rkv[...], s)
    prev_lora = _shift_rows(p_lora, c_lora[...], s)
    xs = p_rkv + (prev_rkv - p_rkv) * mu_rkv_ref[...]
    xl = p_lora + (prev_lora - p_lora) * mu_lora_ref[...]
    new_rkv = p_rkv[rows - s:]
    new_lora = p_lora[rows - s:]
    c_rkv[...] = new_rkv
    c_lora[...] = new_lora
    shr_out[...] = new_rkv
    shl_out[...] = new_lora

    r = xs[:, :D_MODEL]
    k = xs[:, D_MODEL:2 * D_MODEL]
    v = xs[:, 2 * D_MODEL:]
    xw = xl[:, :128]
    xa = xl[:, 128:256]
    xg = xl[:, 256:]

    zw = w0_ref[...] + _dot(jnp.tanh(xw), wd_ref[...])
    wlog = -(jnp.maximum(-zw, 0.0) + jnp.log(1.0 + jnp.exp(-jnp.abs(zw)))) - 0.5
    lw_out[...] = -jnp.exp(wlog)
    a = _sigmoid(a0_ref[...] + _dot(xa, wa_ref[...]))
    g_out[...] = _dot(_sigmoid(xg), wg_ref[...])

    kkr = k * kk_ref[...]
    kk = kkr * lax.rsqrt(jnp.maximum(_head_sum(kkr * kkr), 1e-24))
    k2 = k * (1.0 + (a - 1.0) * ka_ref[...])
    r_out[...] = r
    k_out[...] = k2
    v_out[...] = v
    kk_out[...] = kk
    b_out[...] = kk * a
    bonus_out[...] = _head_sum(r * k2 * rk_ref[...]) * v


def _sc_gate_kernel(x_ref, sc_state_ref, n1g_ref, w_sc_ref, w_gate_ref, b_gate_ref, conv_ref,
                    w_bsc_ref, ga_out, gbo_out, sc_out, c_sc, *, s):
    @pl.when(pl.program_id(1) == 0)
    def _():
        c_sc[...] = sc_state_ref[...]

    rows = x_ref.shape[0]
    xb = _rms_norm(x_ref[...], n1g_ref[...]).astype(BF16)
    p_sc = jnp.dot(xb, w_sc_ref[...], preferred_element_type=F32)
    u = p_sc[:, 2 * D_MODEL:] * p_sc[:, :D_MODEL]
    carry = c_sc[...]
    prev1 = _shift_rows(u, carry[s:], s)
    prev2 = _shift_rows(prev1, carry[:s], s)
    cw = conv_ref[...]
    conv = prev2 * cw[0:1] + prev1 * cw[1:2] + u * cw[2:3]
    o_b = _dot(p_sc[:, D_MODEL:2 * D_MODEL] * conv, w_bsc_ref[...])
    new = u[rows - 2 * s:]
    c_sc[...] = new
    sc_out[...] = new

    pg = jnp.dot(xb, w_gate_ref[...], preferred_element_type=F32) + b_gate_ref[...]
    ga_out[...] = _sigmoid(pg[:, :D_MODEL])
    gbo_out[...] = _sigmoid(pg[:, D_MODEL:]) * o_b


def _wkv_kernel(r_ref, lw_ref, k_ref, v_ref, kk_ref, b_ref, s0_ref, y_out, s_out, s_scr, *, C):
    c_idx = pl.program_id(1)

    @pl.when(c_idx == 0)
    def _():
        zh = jnp.zeros((HEAD_SIZE, HEAD_SIZE), F32)
        for p in range(N_PAIRS):
            s_scr[p] = jnp.concatenate([jnp.concatenate([s0_ref[2 * p], zh], axis=1),
                                        jnp.concatenate([zh, s0_ref[2 * p + 1]], axis=1)], axis=0)

    lw = lw_ref[...]
    ti = lax.broadcasted_iota(jnp.int32, (C, C), 0)
    tj = lax.broadcasted_iota(jnp.int32, (C, C), 1)
    tri = jnp.where(tj <= ti, 1.0, 0.0).astype(BF16)
    h1 = lw.astype(BF16)
    r1 = lw - h1.astype(F32)
    h2 = r1.astype(BF16)
    h3 = (r1 - h2.astype(F32)).astype(BF16)
    cum = (jnp.dot(tri, h1, preferred_element_type=F32) + jnp.dot(tri, h2, preferred_element_type=F32)
           + jnp.dot(tri, h3, preferred_element_type=F32))
    w_t = jnp.exp(cum)
    w_inv = jnp.exp(-cum)
    a_t = -kk_ref[...] * jnp.exp(cum - lw)
    b_t = b_ref[...] * w_inv
    k_t = k_ref[...] * w_inv
    r_t = r_ref[...] * w_t
    v_all = v_ref[...]
    w_last = w_t[C - 1:C]

    m0 = lax.broadcasted_iota(jnp.int32, (C, LANES), 1) < HEAD_SIZE
    gi = lax.broadcasted_iota(jnp.int32, (C, 2 * C), 0)
    gj = lax.broadcasted_iota(jnp.int32, (C, 2 * C), 1)
    gjm = jnp.where(gj >= C, gj - C, gj)
    strict = gjm < gi
    incl = gjm <= gi
    strict_l = gj < gi
    strict_r = jnp.logical_and(gj >= C, gj - C < gi)
    ei = lax.broadcasted_iota(jnp.int32, (2 * C, 2 * C), 0)
    ej = lax.broadcasted_iota(jnp.int32, (2 * C, 2 * C), 1)
    eye = jnp.where(ei == ej, 1.0, 0.0)
    bi = lax.broadcasted_iota(jnp.int32, (LANES, LANES), 0) // HEAD_SIZE
    bj = lax.broadcasted_iota(jnp.int32, (LANES, LANES), 1) // HEAD_SIZE
    bd = bi == bj
    zc = jnp.zeros((C, LANES), F32)

    pairs = range(N_PAIRS)
    sls = [slice(LANES * p, LANES * (p + 1)) for p in pairs]
    a_p = [a_t[:, sl] for sl in sls]
    r_p = [r_t[:, sl] for sl in sls]
    v_p = [v_all[:, sl] for sl in sls]
    bk = [jnp.concatenate([b_t[:, sl], k_t[:, sl]], axis=0) for sl in sls]
    kb = [jnp.concatenate([k_t[:, sl], b_t[:, sl]], axis=0) for sl in sls]
    g0 = [_dot_nt(jnp.concatenate([jnp.where(m0, a_p[p], 0.0), jnp.where(m0, r_p[p], 0.0)], axis=0), bk[p])
          for p in pairs]
    g1 = [_dot_nt(jnp.concatenate([jnp.where(m0, 0.0, a_p[p]), jnp.where(m0, 0.0, r_p[p])], axis=0), kb[p])
          for p in pairs]
    s_bd = [s_scr[p] for p in pairs]
    asrs = [_dot_nt(jnp.concatenate([a_p[p], r_p[p]], axis=0), s_bd[p]) for p in pairs]
    xpow = [jnp.concatenate([jnp.where(strict_l, g0[p][:C], 0.0), jnp.where(strict_r, g1[p][:C], 0.0)], axis=0)
            for p in pairs]
    tinv = [eye + xpow[p] for p in pairs]
    n = 1
    while 2 * n < C:
        xpow = [_dot(xpow[p], xpow[p]) for p in pairs]
        tinv = [tinv[p] + _dot(tinv[p], xpow[p]) for p in pairs]
        n *= 2
    vm0 = [jnp.where(m0, v_p[p], 0.0) for p in pairs]
    vm1 = [jnp.where(m0, 0.0, v_p[p]) for p in pairs]
    rhs = [asrs[p][:C] + _dot(jnp.where(strict, g0[p][:C], 0.0), jnp.concatenate([zc, vm0[p]], axis=0))
           + _dot(jnp.where(strict, g1[p][:C], 0.0), jnp.concatenate([vm1[p], zc], axis=0)) for p in pairs]
    uu = [_dot(tinv[p], jnp.concatenate([jnp.where(m0, rhs[p], 0.0), jnp.where(m0, 0.0, rhs[p])], axis=0))
          for p in pairs]
    u = [uu[p][:C] + uu[p][C:] for p in pairs]
    for p in pairs:
        um0 = jnp.where(m0, u[p], 0.0)
        um1 = jnp.where(m0, 0.0, u[p])
        y_out[:, sls[p]] = (
            asrs[p][C:] + _dot(jnp.where(incl, g0[p][C:], 0.0), jnp.concatenate([um0, vm0[p]], axis=0))
            + _dot(jnp.where(incl, g1[p][C:], 0.0), jnp.concatenate([vm1[p], um1], axis=0)))
    for p in pairs:
        upd = _dot_tn(jnp.concatenate([u[p], v_p[p]], axis=0), bk[p])
        s_scr[p] = (s_bd[p] + jnp.where(bd, upd, 0.0)) * w_last[:, sls[p]]

    @pl.when(c_idx == pl.num_programs(1) - 1)
    def _():
        for p in range(N_PAIRS):
            s_fin = s_scr[p]
            s_out[2 * p] = s_fin[:HEAD_SIZE, :HEAD_SIZE]
            s_out[2 * p + 1] = s_fin[HEAD_SIZE:, HEAD_SIZE:]


def _post_kernel(x_ref, y_ref, bonus_ref, g_ref, ga_ref, gbo_ref, ffn_state_ref, lnx_g_ref, lnx_b_ref,
                 w_br_ref, w_out_ref, n2g_ref, w_up_ref, conv_ref, w_down_ref, fng_ref,
                 out_ref, ffn_out, c_ffn, *, s):
    @pl.when(pl.program_id(1) == 0)
    def _():
        c_ffn[...] = ffn_state_ref[...]

    rows = x_ref.shape[0]
    y = y_ref[...]
    dev = y - _head_sum(y) * (1.0 / HEAD_SIZE)
    var = _head_sum(dev * dev) * (1.0 / HEAD_SIZE)
    yn = dev * lax.rsqrt(var + GN_EPS) * lnx_g_ref[...] + lnx_b_ref[...]
    o_a = _dot((yn + bonus_ref[...]) * g_ref[...], w_br_ref[...])
    x1 = x_ref[...] + _dot(ga_ref[...] * o_a + gbo_ref[...], w_out_ref[...])

    xb = _rms_norm(x1, n2g_ref[...]).astype(BF16)
    n_chunk = D_FF // FF_CHUNK

    def up_proj(j):
        return [jnp.dot(xb, w_up_ref[:, base + FF_CHUNK * j:base + FF_CHUNK * (j + 1)],
                        preferred_element_type=F32) for base in (0, D_FF)]

    acc = jnp.zeros((rows, D_MODEL), F32)
    ups = up_proj(0)
    for j in range(n_chunk):
        cur = ups
        if j + 1 < n_chunk:
            ups = up_proj(j + 1)
        halves = []
        for base, up in zip((0, D_FF), cur):
            cols = slice(base + FF_CHUNK * j, base + FF_CHUNK * (j + 1))
            carry = c_ffn[:, cols]
            prev1 = _shift_rows(up, carry[s:], s)
            prev2 = _shift_rows(prev1, carry[:s], s)
            cw = conv_ref[:, cols]
            halves.append(prev2 * cw[0:1] + prev1 * cw[1:2] + up * cw[2:3])
            new = up[rows - 2 * s:]
            c_ffn[:, cols] = new
            ffn_out[:, cols] = new
        gate, val = halves
        acc = acc + _dot(gate * _sigmoid(gate) * val, w_down_ref[FF_CHUNK * j:FF_CHUNK * (j + 1), :])
    out_ref[...] = _rms_norm(x1 + acc, fng_ref[...])


def _const_spec(arr):
    nd = arr.ndim
    return pl.BlockSpec(arr.shape, lambda b, t: (0,) * nd, pipeline_mode=pl.Buffered(1))


def _row_spec(rows, cols):
    return pl.BlockSpec((None, rows, cols), lambda b, t: (b, t, 0))


def _state_spec(rows, cols):
    return pl.BlockSpec((None, rows, cols), lambda b, t: (b, 0, 0))


def _params():
    return pltpu.CompilerParams(dimension_semantics=("arbitrary", "arbitrary"),
                                vmem_limit_bytes=VMEM_LIMIT)


def _rwkv_prep(x, sh_rkv, sh_lora, w, *, rows, s):
    nb, total, _ = x.shape
    grid = (nb, total // rows)
    consts = [w["norm1_g"], w["w_rkv"], w["w_lora"], w["mu_rkv"], w["mu_lora"], w["w0"], w["wd"],
              w["a0"], w["wa"], w["wg"], w["k_k"], w["k_a"], w["r_k"]]
    tok = jax.ShapeDtypeStruct((nb, total, D_MODEL), F32)
    return pl.pallas_call(
        functools.partial(_rwkv_prep_kernel, s=s),
        grid=grid,
        in_specs=[_row_spec(rows, D_MODEL), _state_spec(s, 3 * D_MODEL), _state_spec(s, D_LORA_PAD)]
                 + [_const_spec(c) for c in consts],
        out_specs=[_row_spec(rows, D_MODEL)] * 8
                  + [_state_spec(s, 3 * D_MODEL), _state_spec(s, D_LORA_PAD)],
        out_shape=[tok] * 8 + [jax.ShapeDtypeStruct((nb, s, 3 * D_MODEL), F32),
                               jax.ShapeDtypeStruct((nb, s, D_LORA_PAD), F32)],
        scratch_shapes=[pltpu.VMEM((s, 3 * D_MODEL), F32), pltpu.VMEM((s, D_LORA_PAD), F32)],
        compiler_params=_params(),
        name="rwkv_prep",
    )(x, sh_rkv, sh_lora, *consts)


def _sc_gate(x, sc_state, w, *, rows, s):
    nb, total, _ = x.shape
    grid = (nb, total // rows)
    consts = [w["norm1_g"], w["w_sc"], w["w_gate"], w["b_gate"], w["conv_sc"], w["w_branch_sc"]]
    tok = jax.ShapeDtypeStruct((nb, total, D_MODEL), F32)
    return pl.pallas_call(
        functools.partial(_sc_gate_kernel, s=s),
        grid=grid,
        in_specs=[_row_spec(rows, D_MODEL), _state_spec(2 * s, D_MODEL)] + [_const_spec(c) for c in consts],
        out_specs=[_row_spec(rows, D_MODEL)] * 2 + [_state_spec(2 * s, D_MODEL)],
        out_shape=[tok] * 2 + [jax.ShapeDtypeStruct((nb, 2 * s, D_MODEL), F32)],
        scratch_shapes=[pltpu.VMEM((2 * s, D_MODEL), F32)],
        compiler_params=_params(),
        name="sc_gate",
    )(x, sc_state, *consts)


def _wkv(r, lw, k, v, kk, b, s0_bd, *, chunk):
    nb, total, _ = r.shape
    grid = (nb, total // chunk)
    tok_spec = pl.BlockSpec((None, chunk, D_MODEL), lambda i, c: (i, c, 0))
    st_spec = pl.BlockSpec((None, N_HEADS, HEAD_SIZE, HEAD_SIZE), lambda i, c: (i, 0, 0, 0))
    return pl.pallas_call(
        functools.partial(_wkv_kernel, C=chunk),
        grid=grid,
        in_specs=[tok_spec] * 6 + [st_spec],
        out_specs=[tok_spec, st_spec],
        out_shape=[jax.ShapeDtypeStruct((nb, total, D_MODEL), F32),
                   jax.ShapeDtypeStruct((nb, N_HEADS, HEAD_SIZE, HEAD_SIZE), F32)],
        scratch_shapes=[pltpu.VMEM((N_PAIRS, LANES, LANES), F32)],
        compiler_params=_params(),
        name="wkv",
    )(r, lw, k, v, kk, b, s0_bd)


def _post(x, y, bonus, g, ga, gbo, ffn_state, w, *, rows, s):
    nb, total, _ = x.shape
    grid = (nb, total // rows)
    consts = [w["lnx_g"], w["lnx_b"], w["w_branch_rwkv"], w["w_out"], w["norm2_g"], w["w_up"],
              w["conv_ffn"], w["w_down"], w["final_norm_g"]]
    return pl.pallas_call(
        functools.partial(_post_kernel, s=s),
        grid=grid,
        in_specs=[_row_spec(rows, D_MODEL)] * 6 + [_state_spec(2 * s, 2 * D_FF)]
                 + [_const_spec(c) for c in consts],
        out_specs=[_row_spec(rows, D_MODEL), _state_spec(2 * s, 2 * D_FF)],
        out_shape=[jax.ShapeDtypeStruct((nb, total, D_MODEL), F32),
                   jax.ShapeDtypeStruct((nb, 2 * s, 2 * D_FF), F32)],
        scratch_shapes=[pltpu.VMEM((2 * s, 2 * D_FF), F32)],
        compiler_params=_params(),
        name="post",
    )(x, y, bonus, g, ga, gbo, ffn_state, *consts)


def _pad_lora_cols(a):
    pad = lambda t, n: jnp.pad(t, [(0, 0)] * (t.ndim - 1) + [(0, n - t.shape[-1])])
    return jnp.concatenate([pad(a[..., :64], 128), pad(a[..., 64:128], 128), pad(a[..., 128:], 256)], axis=-1)


def _unpad_lora_cols(a):
    return jnp.concatenate([a[..., :64], a[..., 128:192], a[..., 256:256 + D_GATE_LORA]], axis=-1)


def _prep_weights(norm1_g, w_in, b_gate, mu_shift, w0, w_decay_up, a0, w_aaa_up, w_gate_up, k_k, k_a,
                  r_k, lnx_g, lnx_b, w_branch_rwkv, w_branch_sc, conv_sc, w_out, norm2_g, w_up, conv_ffn,
                  w_down, final_norm_g):
    row = lambda t: t.reshape(1, -1).astype(F32)
    d3 = 3 * D_MODEL
    n_lora = D_DECAY_LORA + D_AAA_LORA + D_GATE_LORA
    pad_rows = lambda t, n: jnp.pad(t, [(0, n - t.shape[0]), (0, 0)])
    return {
        "norm1_g": row(norm1_g),
        "w_rkv": w_in[:, :d3].astype(BF16),
        "w_lora": _pad_lora_cols(w_in[:, d3:d3 + n_lora]).astype(BF16),
        "w_sc": w_in[:, d3 + n_lora:2 * d3 + n_lora].astype(BF16),
        "w_gate": w_in[:, 2 * d3 + n_lora:].astype(BF16),
        "b_gate": row(b_gate),
        "mu_rkv": row(mu_shift[:d3]),
        "mu_lora": _pad_lora_cols(row(mu_shift[d3:])),
        "w0": row(w0),
        "wd": pad_rows(w_decay_up, 128).astype(BF16),
        "a0": row(a0),
        "wa": pad_rows(w_aaa_up, 128).astype(BF16),
        "wg": pad_rows(w_gate_up, 256).astype(BF16),
        "k_k": row(k_k), "k_a": row(k_a), "r_k": row(r_k),
        "lnx_g": row(lnx_g), "lnx_b": row(lnx_b),
        "w_branch_rwkv": w_branch_rwkv.astype(BF16),
        "w_branch_sc": w_branch_sc.astype(BF16),
        "conv_sc": conv_sc.astype(F32),
        "w_out": w_out.astype(BF16),
        "norm2_g": row(norm2_g),
        "w_up": w_up.astype(BF16),
        "conv_ffn": conv_ffn.astype(F32),
        "w_down": w_down.astype(BF16),
        "final_norm_g": row(final_norm_g),
    }


def _layer(x, s_wkv, sh_rkv, sh_lora, sc_state, ffn_state, w, *, rows, s, to_seq, from_seq, chunk):
    r, lw, k, v, kk, b, g, bonus, shr, shl = _rwkv_prep(x, sh_rkv, sh_lora, w, rows=rows, s=s)
    ga, gbo, sc_new = _sc_gate(x, sc_state, w, rows=rows, s=s)
    y_seq, s_new = _wkv(*[to_seq(t) for t in (r, lw, k, v, kk, b)], s_wkv, chunk=chunk)
    out, ffn_new = _post(x, from_seq(y_seq), bonus, g, ga, gbo, ffn_state, w, rows=rows, s=s)
    return out, s_new, shr, shl, sc_new, ffn_new


def kernel(x_prompt, x_sample, state_wkv, state_shift, state_sc_conv, state_ffn_conv, meta_tokens,
           norm1_g, w_in, b_gate, mu_shift, w0, w_decay_up, a0, w_aaa_up, w_gate_up, k_k, k_a, r_k,
           lnx_g, lnx_b, w_branch_rwkv, w_branch_sc, conv_sc, w_out, norm2_g, w_up, conv_ffn, w_down,
           final_norm_g):
    w = _prep_weights(norm1_g[0], w_in[0], b_gate[0], mu_shift[0], w0[0], w_decay_up[0], a0[0],
                      w_aaa_up[0], w_gate_up[0], k_k[0], k_a[0], r_k[0], lnx_g[0], lnx_b[0],
                      w_branch_rwkv[0], w_branch_sc[0], conv_sc[0], w_out[0], norm2_g[0], w_up[0],
                      conv_ffn[0], w_down[0], final_norm_g)
    d3 = 3 * D_MODEL
    ident = lambda t: t

    bp, seq, _ = x_prompt.shape
    zeros = lambda *shape: jnp.zeros(shape, F32)
    _, m_wkv, m_shr, m_shl, m_sc, m_ffn = _layer(
        meta_tokens.astype(F32)[None], zeros(1, N_HEADS, HEAD_SIZE, HEAD_SIZE), zeros(1, 1, d3),
        zeros(1, 1, D_LORA_PAD), zeros(1, 2, D_MODEL), zeros(1, 2, 2 * D_FF), w,
        rows=N_META, s=1, to_seq=ident, from_seq=ident, chunk=N_META)

    rep = lambda t: jnp.broadcast_to(t, (bp,) + t.shape[1:])
    y_prompt, p_wkv, p_shr, p_shl, p_sc, p_ffn = _layer(
        x_prompt, rep(m_wkv), rep(m_shr), rep(m_shl), rep(m_sc), rep(m_ffn), w,
        rows=256, s=1, to_seq=ident, from_seq=ident, chunk=64)

    bs, ts, _ = x_sample.shape
    n_grp = 2
    gsz = bs // n_grp
    t_pad = 8

    def to_rows(t):
        n, c = t.shape[1:]
        return t.reshape(n_grp, gsz, n, c).transpose(0, 2, 1, 3).reshape(n_grp, n * gsz, c)

    def to_seq(t):
        c = t.shape[-1]
        t = t.reshape(n_grp, ts, gsz, c).transpose(0, 2, 1, 3).reshape(bs, ts, c)
        return jnp.pad(t, ((0, 0), (0, t_pad - ts), (0, 0)))

    def from_seq(t):
        return to_rows(t[:, :ts])

    conv_rows = lambda st: to_rows(st)
    sh = state_shift[0].reshape(n_grp, gsz, -1)
    y_s, s_wkv, s_shr, s_shl, s_sc, s_ffn = _layer(
        to_rows(x_sample), state_wkv[0], sh[..., :d3], _pad_lora_cols(sh[..., d3:]),
        conv_rows(state_sc_conv[0]), conv_rows(state_ffn_conv[0]), w,
        rows=ts * gsz, s=gsz, to_seq=to_seq, from_seq=from_seq, chunk=t_pad)

    def rows_to_batch(t, n):
        c = t.shape[-1]
        return t.reshape(n_grp, n, gsz, c).transpose(0, 2, 1, 3).reshape(bs, n, c)

    y_sample = rows_to_batch(y_s, ts)
    shift_p = jnp.concatenate([p_shr[:, 0], _unpad_lora_cols(p_shl[:, 0])], axis=-1)
    shift_s = jnp.concatenate([rows_to_batch(s_shr, 1)[:, 0], _unpad_lora_cols(rows_to_batch(s_shl, 1)[:, 0])],
                              axis=-1)
    return (y_prompt, y_sample,
            p_wkv[None], s_wkv[None],
            shift_p[None], shift_s[None],
            p_sc[None], rows_to_batch(s_sc, 2)[None],
            p_ffn[None], rows_to_batch(s_ffn, 2)[None])
```

```python
import functools

import jax
import jax.numpy as jnp
from jax import lax
from jax.experimental import pallas as pl
from jax.experimental.pallas import tpu as pltpu

D_MODEL = 1024
N_META = 16
HEAD_SIZE = 64
N_HEADS = D_MODEL // HEAD_SIZE
LANES = 128
N_PAIRS = D_MODEL // LANES
D_DECAY_LORA = 64
D_AAA_LORA = 64
D_GATE_LORA = 160
D_LORA_PAD = 512
D_FF = 2816
FF_CHUNK = 256
CONV_W = 3
RMS_EPS = 1e-6
GN_EPS = 64e-5
VMEM_LIMIT = 60 * 1024 * 1024

F32 = jnp.float32
BF16 = jnp.bfloat16


def _dot(a, b):
    return jnp.dot(a.astype(BF16), b.astype(BF16), preferred_element_type=F32)


def _dot_nt(a, b):
    return lax.dot_general(a.astype(BF16), b.astype(BF16), (((1,), (1,)), ((), ())),
                           preferred_element_type=F32)


def _dot_tn(a, b):
    return lax.dot_general(a.astype(BF16), b.astype(BF16), (((0,), (0,)), ((), ())),
                           preferred_element_type=F32)


def _head_ones(n):
    r = lax.broadcasted_iota(jnp.int32, (n, n), 0) // HEAD_SIZE
    c = lax.broadcasted_iota(jnp.int32, (n, n), 1) // HEAD_SIZE
    return jnp.where(r == c, 1.0, 0.0).astype(BF16)


def _head_sum(x):
    ones = _head_ones(256)
    xb = x.astype(BF16)
    return jnp.concatenate([jnp.dot(xb[:, 256 * g:256 * (g + 1)], ones, preferred_element_type=F32)
                            for g in range(D_MODEL // 256)], axis=1)


def _rms_norm(x, g):
    return x * lax.rsqrt(jnp.mean(x * x, axis=-1, keepdims=True) + RMS_EPS) * g


def _sigmoid(x):
    return 1.0 / (1.0 + jnp.exp(-x))


def _shift_rows(cur, carry, s):
    rows = cur.shape[0]
    if s % 8 == 0:
        return jnp.concatenate([carry, cur[:rows - s]], axis=0)
    assert s == 1
    rolled = pltpu.roll(cur, 1, axis=0)
    row = lax.broadcasted_iota(jnp.int32, cur.shape, 0)
    return jnp.where(row == 0, carry, rolled)


def _rwkv_prep_kernel(x_ref, sh_rkv_ref, sh_lora_ref, n1g_ref, w_rkv_ref, w_lora_ref, mu_rkv_ref,
                      mu_lora_ref, w0_ref, wd_ref, a0_ref, wa_ref, wg_ref, kk_ref, ka_ref, rk_ref,
                      r_out, lw_out, k_out, v_out, kk_out, b_out, g_out, bonus_out, shr_out, shl_out,
                      c_rkv, c_lora, *, s):
    @pl.when(pl.program_id(1) == 0)
    def _():
        c_rkv[...] = sh_rkv_ref[...]
        c_lora[...] = sh_lora_ref[...]

    rows = x_ref.shape[0]
    xb = _rms_norm(x_ref[...], n1g_ref[...]).astype(BF16)
    p_rkv = jnp.dot(xb, w_rkv_ref[...], preferred_element_type=F32)
    p_lora = jnp.dot(xb, w_lora_ref[...], preferred_element_type=F32)
    prev_rkv = _shift_rows(p_rkv, c_rkv[...], s)
    prev_lora = _shift_rows(p_lora, c_lora[...], s)
    xs = p_rkv + (prev_rkv - p_rkv) * mu_rkv_ref[...]
    xl = p_lora + (prev_lora - p_lora) * mu_lora_ref[...]
    new_rkv = p_rkv[rows - s:]
    new_lora = p_lora[rows - s:]
    c_rkv[...] = new_rkv
    c_lora[...] = new_lora
    shr_out[...] = new_rkv
    shl_out[...] = new_lora

    r = xs[:, :D_MODEL]
    k = xs[:, D_MODEL:2 * D_MODEL]
    v = xs[:, 2 * D_MODEL:]
    xw = xl[:, :128]
    xa = xl[:, 128:256]
    xg = xl[:, 256:]

    zw = w0_ref[...] + _dot(jnp.tanh(xw), wd_ref[...])
    wlog = -(jnp.maximum(-zw, 0.0) + jnp.log(1.0 + jnp.exp(-jnp.abs(zw)))) - 0.5
    lw_out[...] = -jnp.exp(wlog)
    a = _sigmoid(a0_ref[...] + _dot(xa, wa_ref[...]))
    g_out[...] = _dot(_sigmoid(xg), wg_ref[...])

    kkr = k * kk_ref[...]
    kk = kkr * lax.rsqrt(jnp.maximum(_head_sum(kkr * kkr), 1e-24))
    k2 = k * (1.0 + (a - 1.0) * ka_ref[...])
    r_out[...] = r
    k_out[...] = k2
    v_out[...] = v
    kk_out[...] = kk
    b_out[...] = kk * a
    bonus_out[...] = _head_sum(r * k2 * rk_ref[...]) * v


def _sc_gate_kernel(x_ref, sc_state_ref, n1g_ref, w_sc_ref, w_gate_ref, b_gate_ref, conv_ref,
                    w_bsc_ref, ga_out, gbo_out, sc_out, c_sc, *, s):
    @pl.when(pl.program_id(1) == 0)
    def _():
        c_sc[...] = sc_state_ref[...]

    rows = x_ref.shape[0]
    xb = _rms_norm(x_ref[...], n1g_ref[...]).astype(BF16)
    p_sc = jnp.dot(xb, w_sc_ref[...], preferred_element_type=F32)
    u = p_sc[:, 2 * D_MODEL:] * p_sc[:, :D_MODEL]
    carry = c_sc[...]
    prev1 = _shift_rows(u, carry[s:], s)
    prev2 = _shift_rows(prev1, carry[:s], s)
    cw = conv_ref[...]
    conv = prev2 * cw[0:1] + prev1 * cw[1:2] + u * cw[2:3]
    o_b = _dot(p_sc[:, D_MODEL:2 * D_MODEL] * conv, w_bsc_ref[...])
    new = u[rows - 2 * s:]
    c_sc[...] = new
    sc_out[...] = new

    pg = jnp.dot(xb, w_gate_ref[...], preferred_element_type=F32) + b_gate_ref[...]
    ga_out[...] = _sigmoid(pg[:, :D_MODEL])
    gbo_out[...] = _sigmoid(pg[:, D_MODEL:]) * o_b


def _wkv_kernel(r_ref, lw_ref, k_ref, v_ref, kk_ref, b_ref, s0_ref, y_out, s_out, s_scr, *, C, nq):
    c_idx = pl.program_id(1)

    @pl.when(c_idx == 0)
    def _():
        zh = jnp.zeros((HEAD_SIZE, HEAD_SIZE), F32)
        for q in range(nq):
            for p in range(N_PAIRS):
                s_scr[q * N_PAIRS + p] = jnp.concatenate(
                    [jnp.concatenate([s0_ref[q, 2 * p], zh], axis=1),
                     jnp.concatenate([zh, s0_ref[q, 2 * p + 1]], axis=1)], axis=0)

    ti = lax.broadcasted_iota(jnp.int32, (C, C), 0)
    tj = lax.broadcasted_iota(jnp.int32, (C, C), 1)
    tri = jnp.where(tj <= ti, 1.0, 0.0).astype(BF16)
    m0 = lax.broadcasted_iota(jnp.int32, (C, LANES), 1) < HEAD_SIZE
    gi = lax.broadcasted_iota(jnp.int32, (C, 2 * C), 0)
    gj = lax.broadcasted_iota(jnp.int32, (C, 2 * C), 1)
    gjm = jnp.where(gj >= C, gj - C, gj)
    strict = gjm < gi
    incl = gjm <= gi
    left = gj < C
    strict_l = gj < gi
    strict_r = jnp.logical_and(gj >= C, gj - C < gi)
    ei = lax.broadcasted_iota(jnp.int32, (2 * C, 2 * C), 0)
    ej = lax.broadcasted_iota(jnp.int32, (2 * C, 2 * C), 1)
    eye = jnp.where(ei == ej, 1.0, 0.0)
    bi = lax.broadcasted_iota(jnp.int32, (LANES, LANES), 0) // HEAD_SIZE
    bj = lax.broadcasted_iota(jnp.int32, (LANES, LANES), 1) // HEAD_SIZE
    bd = bi == bj

    def swap_halves(x):
        if 2 * C == LANES:
            return pltpu.roll(x, C, axis=1)
        return jnp.concatenate([x[:, C:], x[:, :C]], axis=1)

    a_p, r_p, v_p, bk, w_last = [], [], [], [], []
    for q in range(nq):
        lw = lw_ref[q]
        h1 = lw.astype(BF16)
        h2 = (lw - h1.astype(F32)).astype(BF16)
        cum = jnp.dot(tri, h1, preferred_element_type=F32) + jnp.dot(tri, h2, preferred_element_type=F32)
        w_t = jnp.exp(cum)
        w_inv = jnp.exp(-cum)
        a_t = -kk_ref[q] * jnp.exp(cum - lw)
        b_t = b_ref[q] * w_inv
        k_t = k_ref[q] * w_inv
        r_t = r_ref[q] * w_t
        v_q = v_ref[q]
        for p in range(N_PAIRS):
            sl = slice(LANES * p, LANES * (p + 1))
            a_p.append(a_t[:, sl])
            r_p.append(r_t[:, sl])
            v_p.append(v_q[:, sl])
            bk.append(jnp.concatenate([b_t[:, sl], k_t[:, sl]], axis=0))
            w_last.append(w_t[C - 1:C, sl])
    ents = range(nq * N_PAIRS)

    g = [_dot_nt(jnp.concatenate([jnp.where(m0, a_p[e], 0.0), jnp.where(m0, r_p[e], 0.0),
                                  jnp.where(m0, 0.0, a_p[e]), jnp.where(m0, 0.0, r_p[e])], axis=0), bk[e])
         for e in ents]
    s_bd = [s_scr[e] for e in ents]
    asrs = [_dot_nt(jnp.concatenate([a_p[e], r_p[e]], axis=0), s_bd[e]) for e in ents]
    g1a = [swap_halves(g[e][2 * C:3 * C]) for e in ents]
    xpow = [jnp.concatenate([jnp.where(strict_l, g[e][:C], 0.0), jnp.where(strict_r, g1a[e], 0.0)], axis=0)
            for e in ents]
    tinv = [eye + xpow[e] for e in ents]
    n = 1
    if 2 * n < C:
        xpow = [_dot(xpow[e], xpow[e]) for e in ents]
        n *= 2
    while 2 * n < C:
        z = [_dot(jnp.concatenate([xpow[e], tinv[e]], axis=0), xpow[e]) for e in ents]
        xpow = [z[e][:2 * C] for e in ents]
        tinv = [tinv[e] + z[e][2 * C:] for e in ents]
        n *= 2
    if C > 2:
        tinv = [tinv[e] + _dot(tinv[e], xpow[e]) for e in ents]
    vm0 = [jnp.where(m0, v_p[e], 0.0) for e in ents]
    vm1 = [jnp.where(m0, 0.0, v_p[e]) for e in ents]
    mak = [jnp.where(strict, jnp.where(left, g1a[e], g[e][:C]), 0.0) for e in ents]
    rhs = [asrs[e][:C] + _dot(mak[e], jnp.concatenate([vm1[e], vm0[e]], axis=0)) for e in ents]
    uu = [_dot(tinv[e], jnp.concatenate([jnp.where(m0, rhs[e], 0.0), jnp.where(m0, 0.0, rhs[e])], axis=0))
          for e in ents]
    u = [uu[e][:C] + uu[e][C:] for e in ents]
    for e in ents:
        q, p = divmod(e, N_PAIRS)
        um0 = jnp.where(m0, u[e], 0.0)
        um1 = jnp.where(m0, 0.0, u[e])
        g0r = jnp.where(incl, g[e][C:2 * C], 0.0)
        g1r = jnp.where(incl, g[e][3 * C:], 0.0)
        if 2 * C == LANES:
            intra = _dot(jnp.concatenate([g0r, g1r], axis=1),
                         jnp.concatenate([um0, vm0[e], um1, vm1[e]], axis=0))
        else:
            intra = (_dot(g0r, jnp.concatenate([um0, vm0[e]], axis=0))
                     + _dot(g1r, jnp.concatenate([um1, vm1[e]], axis=0)))
        y_out[q, :, LANES * p:LANES * (p + 1)] = asrs[e][C:] + intra
    for e in ents:
        upd = _dot_tn(jnp.concatenate([u[e], v_p[e]], axis=0), bk[e])
        s_scr[e] = (s_bd[e] + jnp.where(bd, upd, 0.0)) * w_last[e]

    @pl.when(c_idx == pl.num_programs(1) - 1)
    def _():
        for q in range(nq):
            for p in range(N_PAIRS):
                s_fin = s_scr[q * N_PAIRS + p]
                s_out[q, 2 * p] = s_fin[:HEAD_SIZE, :HEAD_SIZE]
                s_out[q, 2 * p + 1] = s_fin[HEAD_SIZE:, HEAD_SIZE:]


def _post_kernel(x_ref, y_ref, bonus_ref, g_ref, ga_ref, gbo_ref, ffn_state_ref, lnx_g_ref, lnx_b_ref,
                 w_br_ref, w_out_ref, n2g_ref, w_up_ref, conv_ref, w_down_ref, fng_ref,
                 out_ref, ffn_out, c_ffn, *, s):
    @pl.when(pl.program_id(1) == 0)
    def _():
        c_ffn[...] = ffn_state_ref[...]

    rows = x_ref.shape[0]
    y = y_ref[...]
    dev = y - _head_sum(y) * (1.0 / HEAD_SIZE)
    var = _head_sum(dev * dev) * (1.0 / HEAD_SIZE)
    yn = dev * lax.rsqrt(var + GN_EPS) * lnx_g_ref[...] + lnx_b_ref[...]
    o_a = _dot((yn + bonus_ref[...]) * g_ref[...], w_br_ref[...])
    x1 = x_ref[...] + _dot(ga_ref[...] * o_a + gbo_ref[...], w_out_ref[...])

    xb = _rms_norm(x1, n2g_ref[...]).astype(BF16)
    n_chunk = D_FF // FF_CHUNK

    def up_proj(j):
        return [jnp.dot(xb, w_up_ref[:, base + FF_CHUNK * j:base + FF_CHUNK * (j + 1)],
                        preferred_element_type=F32) for base in (0, D_FF)]

    acc = jnp.zeros((rows, D_MODEL), F32)
    ups = up_proj(0)
    for j in range(n_chunk):
        cur = ups
        if j + 1 < n_chunk:
            ups = up_proj(j + 1)
        halves = []
        for base, up in zip((0, D_FF), cur):
            cols = slice(base + FF_CHUNK * j, base + FF_CHUNK * (j + 1))
            carry = c_ffn[:, cols]
            prev1 = _shift_rows(up, carry[s:], s)
            prev2 = _shift_rows(prev1, carry[:s], s)
            cw = conv_ref[:, cols]
            halves.append(prev2 * cw[0:1] + prev1 * cw[1:2] + up * cw[2:3])
            new = up[rows - 2 * s:]
            c_ffn[:, cols] = new
            ffn_out[:, cols] = new
        gate, val = halves
        acc = acc + _dot(gate * _sigmoid(gate) * val, w_down_ref[FF_CHUNK * j:FF_CHUNK * (j + 1), :])
    out_ref[...] = _rms_norm(x1 + acc, fng_ref[...])


def _const_spec(arr):
    nd = arr.ndim
    return pl.BlockSpec(arr.shape, lambda b, t: (0,) * nd, pipeline_mode=pl.Buffered(1))


def _row_spec(rows, cols):
    return pl.BlockSpec((None, rows, cols), lambda b, t: (b, t, 0))


def _state_spec(rows, cols):
    return pl.BlockSpec((None, rows, cols), lambda b, t: (b, 0, 0))


def _params():
    return pltpu.CompilerParams(dimension_semantics=("arbitrary", "arbitrary"),
                                vmem_limit_bytes=VMEM_LIMIT)


def _rwkv_prep(x, sh_rkv, sh_lora, w, *, rows, s):
    nb, total, _ = x.shape
    grid = (nb, total // rows)
    consts = [w["norm1_g"], w["w_rkv"], w["w_lora"], w["mu_rkv"], w["mu_lora"], w["w0"], w["wd"],
              w["a0"], w["wa"], w["wg"], w["k_k"], w["k_a"], w["r_k"]]
    tok = jax.ShapeDtypeStruct((nb, total, D_MODEL), F32)
    return pl.pallas_call(
        functools.partial(_rwkv_prep_kernel, s=s),
        grid=grid,
        in_specs=[_row_spec(rows, D_MODEL), _state_spec(s, 3 * D_MODEL), _state_spec(s, D_LORA_PAD)]
                 + [_const_spec(c) for c in consts],
        out_specs=[_row_spec(rows, D_MODEL)] * 8
                  + [_state_spec(s, 3 * D_MODEL), _state_spec(s, D_LORA_PAD)],
        out_shape=[tok] * 8 + [jax.ShapeDtypeStruct((nb, s, 3 * D_MODEL), F32),
                               jax.ShapeDtypeStruct((nb, s, D_LORA_PAD), F32)],
        scratch_shapes=[pltpu.VMEM((s, 3 * D_MODEL), F32), pltpu.VMEM((s, D_LORA_PAD), F32)],
        compiler_params=_params(),
        name="rwkv_prep",
    )(x, sh_rkv, sh_lora, *consts)


def _sc_gate(x, sc_state, w, *, rows, s):
    nb, total, _ = x.shape
    grid = (nb, total // rows)
    consts = [w["norm1_g"], w["w_sc"], w["w_gate"], w["b_gate"], w["conv_sc"], w["w_branch_sc"]]
    tok = jax.ShapeDtypeStruct((nb, total, D_MODEL), F32)
    return pl.pallas_call(
        functools.partial(_sc_gate_kernel, s=s),
        grid=grid,
        in_specs=[_row_spec(rows, D_MODEL), _state_spec(2 * s, D_MODEL)] + [_const_spec(c) for c in consts],
        out_specs=[_row_spec(rows, D_MODEL)] * 2 + [_state_spec(2 * s, D_MODEL)],
        out_shape=[tok] * 2 + [jax.ShapeDtypeStruct((nb, 2 * s, D_MODEL), F32)],
        scratch_shapes=[pltpu.VMEM((2 * s, D_MODEL), F32)],
        compiler_params=_params(),
        name="sc_gate",
    )(x, sc_state, *consts)


def _wkv(r, lw, k, v, kk, b, s0, *, chunk, nq):
    nb, total, _ = r.shape
    grid = (nb // nq, total // chunk)
    tok_spec = pl.BlockSpec((nq, chunk, D_MODEL), lambda i, c: (i, c, 0))
    st_spec = pl.BlockSpec((nq, N_HEADS, HEAD_SIZE, HEAD_SIZE), lambda i, c: (i, 0, 0, 0))
    return pl.pallas_call(
        functools.partial(_wkv_kernel, C=chunk, nq=nq),
        grid=grid,
        in_specs=[tok_spec] * 6 + [st_spec],
        out_specs=[tok_spec, st_spec],
        out_shape=[jax.ShapeDtypeStruct((nb, total, D_MODEL), F32),
                   jax.ShapeDtypeStruct((nb, N_HEADS, HEAD_SIZE, HEAD_SIZE), F32)],
        scratch_shapes=[pltpu.VMEM((nq * N_PAIRS, LANES, LANES), F32)],
        compiler_params=_params(),
        name="wkv",
    )(r, lw, k, v, kk, b, s0)


def _post(x, y, bonus, g, ga, gbo, ffn_state, w, *, rows, s):
    nb, total, _ = x.shape
    grid = (nb, total // rows)
    consts = [w["lnx_g"], w["lnx_b"], w["w_branch_rwkv"], w["w_out"], w["norm2_g"], w["w_up"],
              w["conv_ffn"], w["w_down"], w["final_norm_g"]]
    return pl.pallas_call(
        functools.partial(_post_kernel, s=s),
        grid=grid,
        in_specs=[_row_spec(rows, D_MODEL)] * 6 + [_state_spec(2 * s, 2 * D_FF)]
                 + [_const_spec(c) for c in consts],
        out_specs=[_row_spec(rows, D_MODEL), _state_spec(2 * s, 2 * D_FF)],
        out_shape=[jax.ShapeDtypeStruct((nb, total, D_MODEL), F32),
                   jax.ShapeDtypeStruct((nb, 2 * s, 2 * D_FF), F32)],
        scratch_shapes=[pltpu.VMEM((2 * s, 2 * D_FF), F32)],
        compiler_params=_params(),
        name="post",
    )(x, y, bonus, g, ga, gbo, ffn_state, *consts)


def _pad_lora_cols(a):
    pad = lambda t, n: jnp.pad(t, [(0, 0)] * (t.ndim - 1) + [(0, n - t.shape[-1])])
    return jnp.concatenate([pad(a[..., :64], 128), pad(a[..., 64:128], 128), pad(a[..., 128:], 256)], axis=-1)


def _unpad_lora_cols(a):
    return jnp.concatenate([a[..., :64], a[..., 128:192], a[..., 256:256 + D_GATE_LORA]], axis=-1)


def _prep_weights(norm1_g, w_in, b_gate, mu_shift, w0, w_decay_up, a0, w_aaa_up, w_gate_up, k_k, k_a,
                  r_k, lnx_g, lnx_b, w_branch_rwkv, w_branch_sc, conv_sc, w_out, norm2_g, w_up, conv_ffn,
                  w_down, final_norm_g):
    row = lambda t: t.reshape(1, -1).astype(F32)
    d3 = 3 * D_MODEL
    n_lora = D_DECAY_LORA + D_AAA_LORA + D_GATE_LORA
    pad_rows = lambda t, n: jnp.pad(t, [(0, n - t.shape[0]), (0, 0)])
    return {
        "norm1_g": row(norm1_g),
        "w_rkv": w_in[:, :d3].astype(BF16),
        "w_lora": _pad_lora_cols(w_in[:, d3:d3 + n_lora]).astype(BF16),
        "w_sc": w_in[:, d3 + n_lora:2 * d3 + n_lora].astype(BF16),
        "w_gate": w_in[:, 2 * d3 + n_lora:].astype(BF16),
        "b_gate": row(b_gate),
        "mu_rkv": row(mu_shift[:d3]),
        "mu_lora": _pad_lora_cols(row(mu_shift[d3:])),
        "w0": row(w0),
        "wd": pad_rows(w_decay_up, 128).astype(BF16),
        "a0": row(a0),
        "wa": pad_rows(w_aaa_up, 128).astype(BF16),
        "wg": pad_rows(w_gate_up, 256).astype(BF16),
        "k_k": row(k_k), "k_a": row(k_a), "r_k": row(r_k),
        "lnx_g": row(lnx_g), "lnx_b": row(lnx_b),
        "w_branch_rwkv": w_branch_rwkv.astype(BF16),
        "w_branch_sc": w_branch_sc.astype(BF16),
        "conv_sc": conv_sc.astype(F32),
        "w_out": w_out.astype(BF16),
        "norm2_g": row(norm2_g),
        "w_up": w_up.astype(BF16),
        "conv_ffn": conv_ffn.astype(F32),
        "w_down": w_down.astype(BF16),
        "final_norm_g": row(final_norm_g),
    }


def _layer(x, s_wkv, sh_rkv, sh_lora, sc_state, ffn_state, w, *, rows, s, to_seq, from_seq, chunk, nq):
    r, lw, k, v, kk, b, g, bonus, shr, shl = _rwkv_prep(x, sh_rkv, sh_lora, w, rows=rows, s=s)
    ga, gbo, sc_new = _sc_gate(x, sc_state, w, rows=rows, s=s)
    y_seq, s_new = _wkv(*[to_seq(t) for t in (r, lw, k, v, kk, b)], s_wkv, chunk=chunk, nq=nq)
    out, ffn_new = _post(x, from_seq(y_seq), bonus, g, ga, gbo, ffn_state, w, rows=rows, s=s)
    return out, s_new, shr, shl, sc_new, ffn_new


def kernel(x_prompt, x_sample, state_wkv, state_shift, state_sc_conv, state_ffn_conv, meta_tokens,
           norm1_g, w_in, b_gate, mu_shift, w0, w_decay_up, a0, w_aaa_up, w_gate_up, k_k, k_a, r_k,
           lnx_g, lnx_b, w_branch_rwkv, w_branch_sc, conv_sc, w_out, norm2_g, w_up, conv_ffn, w_down,
           final_norm_g):
    w = _prep_weights(norm1_g[0], w_in[0], b_gate[0], mu_shift[0], w0[0], w_decay_up[0], a0[0],
                      w_aaa_up[0], w_gate_up[0], k_k[0], k_a[0], r_k[0], lnx_g[0], lnx_b[0],
                      w_branch_rwkv[0], w_branch_sc[0], conv_sc[0], w_out[0], norm2_g[0], w_up[0],
                      conv_ffn[0], w_down[0], final_norm_g)
    d3 = 3 * D_MODEL
    ident = lambda t: t

    bp, seq, _ = x_prompt.shape
    zeros = lambda *shape: jnp.zeros(shape, F32)
    _, m_wkv, m_shr, m_shl, m_sc, m_ffn = _layer(
        meta_tokens.astype(F32)[None], zeros(1, N_HEADS, HEAD_SIZE, HEAD_SIZE), zeros(1, 1, d3),
        zeros(1, 1, D_LORA_PAD), zeros(1, 2, D_MODEL), zeros(1, 2, 2 * D_FF), w,
        rows=N_META, s=1, to_seq=ident, from_seq=ident, chunk=N_META, nq=1)

    rep = lambda t: jnp.broadcast_to(t, (bp,) + t.shape[1:])
    y_prompt, p_wkv, p_shr, p_shl, p_sc, p_ffn = _layer(
        x_prompt, rep(m_wkv), rep(m_shr), rep(m_shl), rep(m_sc), rep(m_ffn), w,
        rows=256, s=1, to_seq=ident, from_seq=ident, chunk=64, nq=2)

    bs, ts, _ = x_sample.shape
    n_grp = 2
    gsz = bs // n_grp
    t_pad = 8

    def to_rows(t):
        n, c = t.shape[1:]
        return t.reshape(n_grp, gsz, n, c).transpose(0, 2, 1, 3).reshape(n_grp, n * gsz, c)

    def to_seq(t):
        c = t.shape[-1]
        t = t.reshape(n_grp, ts, gsz, c).transpose(0, 2, 1, 3).reshape(bs, ts, c)
        return jnp.pad(t, ((0, 0), (0, t_pad - ts), (0, 0)))

    def from_seq(t):
        return to_rows(t[:, :ts])

    conv_rows = lambda st: to_rows(st)
    sh = state_shift[0].reshape(n_grp, gsz, -1)
    y_s, s_wkv, s_shr, s_shl, s_sc, s_ffn = _layer(
        to_rows(x_sample), state_wkv[0], sh[..., :d3], _pad_lora_cols(sh[..., d3:]),
        conv_rows(state_sc_conv[0]), conv_rows(state_ffn_conv[0]), w,
        rows=ts * gsz, s=gsz, to_seq=to_seq, from_seq=from_seq, chunk=t_pad, nq=4)

    def rows_to_batch(t, n):
        c = t.shape[-1]
        return t.reshape(n_grp, n, gsz, c).transpose(0, 2, 1, 3).reshape(bs, n, c)

    y_sample = rows_to_batch(y_s, ts)
    shift_p = jnp.concatenate([p_shr[:, 0], _unpad_lora_cols(p_shl[:, 0])], axis=-1)
    shift_s = jnp.concatenate([rows_to_batch(s_shr, 1)[:, 0], _unpad_lora_cols(rows_to_batch(s_shl, 1)[:, 0])],
                              axis=-1)
    return (y_prompt, y_sample,
            p_wkv[None], s_wkv[None],
            shift_p[None], shift_s[None],
            p_sc[None], rows_to_batch(s_sc, 2)[None],
            p_ffn[None], rows_to_batch(s_ffn, 2)[None])
```

```python
import functools

import jax
import jax.numpy as jnp
from jax import lax
from jax.experimental import pallas as pl
from jax.experimental.pallas import tpu as pltpu

D_MODEL = 1024
N_META = 16
HEAD_SIZE = 64
N_HEADS = D_MODEL // HEAD_SIZE
LANES = 128
N_PAIRS = D_MODEL // LANES
D_DECAY_LORA = 64
D_AAA_LORA = 64
D_GATE_LORA = 160
D_LORA_PAD = 512
D_FF = 2816
FF_CHUNK = 256
COL_CHUNK = 256
CONV_W = 3
NEG_LOG2_E = -1.4426950408889634
EXP_NEG_HALF = 0.6065306597126334
RMS_EPS = 1e-6
GN_EPS = 64e-5
VMEM_LIMIT = 60 * 1024 * 1024

F32 = jnp.float32
BF16 = jnp.bfloat16


def _dot(a, b):
    return jnp.dot(a.astype(BF16), b.astype(BF16), preferred_element_type=F32)


def _dot_nt(a, b):
    return lax.dot_general(a.astype(BF16), b.astype(BF16), (((1,), (1,)), ((), ())),
                           preferred_element_type=F32)


def _dot_tn(a, b):
    return lax.dot_general(a.astype(BF16), b.astype(BF16), (((0,), (0,)), ((), ())),
                           preferred_element_type=F32)


def _head_ones(n):
    r = lax.broadcasted_iota(jnp.int32, (n, n), 0) // HEAD_SIZE
    c = lax.broadcasted_iota(jnp.int32, (n, n), 1) // HEAD_SIZE
    return jnp.where(r == c, 1.0, 0.0).astype(BF16)


def _head_sum(x):
    return jnp.dot(x.astype(BF16), _head_ones(x.shape[1]), preferred_element_type=F32)


def _rms_norm(x, g):
    return x * lax.rsqrt(jnp.mean(x * x, axis=-1, keepdims=True) + RMS_EPS) * g


def _sigmoid(x):
    return 1.0 / (1.0 + jnp.exp2(x * NEG_LOG2_E))


def _shift_rows(cur, carry, s):
    rows = cur.shape[0]
    if s % 8 == 0:
        return jnp.concatenate([carry, cur[:rows - s]], axis=0)
    assert s == 1
    rolled = pltpu.roll(cur, 1, axis=0)
    row = lax.broadcasted_iota(jnp.int32, cur.shape, 0)
    return jnp.where(row == 0, carry, rolled)


def _rwkv_prep_kernel(xb_ref, sh_rkv_ref, sh_lora_ref, w_rkv_ref, w_lora_ref, mu_rkv_ref,
                      mu_lora_ref, w0_ref, wd_ref, a0_ref, wa_ref, wg_ref, kk_ref, ka_ref, rk_ref,
                      r_out, lw_out, k_out, v_out, kk_out, b_out, g_out, bonus_out, shr_out, shl_out,
                      c_rkv, c_lora, *, s):
    @pl.when(pl.program_id(1) == 0)
    def _():
        c_rkv[...] = sh_rkv_ref[...]
        c_lora[...] = sh_lora_ref[...]

    rows = xb_ref.shape[0]
    xb = xb_ref[...]
    p_lora = jnp.dot(xb, w_lora_ref[...], preferred_element_type=F32)
    prev_lora = _shift_rows(p_lora, c_lora[...], s)
    xl = p_lora + (prev_lora - p_lora) * mu_lora_ref[...]
    new_lora = p_lora[rows - s:]
    c_lora[...] = new_lora
    shl_out[...] = new_lora
    t_xw = jnp.tanh(xl[:, :128]).astype(BF16)
    xa = xl[:, 128:256].astype(BF16)
    s_xg = _sigmoid(xl[:, 256:]).astype(BF16)

    def proj(c):
        return [jnp.dot(xb, w_rkv_ref[:, base + COL_CHUNK * c:base + COL_CHUNK * (c + 1)],
                        preferred_element_type=F32) for base in (0, D_MODEL, 2 * D_MODEL)]

    nxt = proj(0)
    for c in range(D_MODEL // COL_CHUNK):
        cur = nxt
        if c + 1 < D_MODEL // COL_CHUNK:
            nxt = proj(c + 1)
        cols = slice(COL_CHUNK * c, COL_CHUNK * (c + 1))
        shifted = []
        for part, p in enumerate(cur):
            pcols = slice(part * D_MODEL + COL_CHUNK * c, part * D_MODEL + COL_CHUNK * (c + 1))
            prev = _shift_rows(p, c_rkv[:, pcols], s)
            shifted.append(p + (prev - p) * mu_rkv_ref[:, pcols])
            new = p[rows - s:]
            c_rkv[:, pcols] = new
            shr_out[:, pcols] = new
        r, k, v = shifted

        zw = w0_ref[:, cols] + jnp.dot(t_xw, wd_ref[:, cols], preferred_element_type=F32)
        lw_out[:, cols] = _sigmoid(zw) * (-EXP_NEG_HALF)
        a = _sigmoid(a0_ref[:, cols] + jnp.dot(xa, wa_ref[:, cols], preferred_element_type=F32))
        g_out[:, cols] = jnp.dot(s_xg, wg_ref[:, cols], preferred_element_type=F32)

        kkr = k * kk_ref[:, cols]
        kk = kkr * lax.rsqrt(jnp.maximum(_head_sum(kkr * kkr), 1e-24))
        k2 = k * (1.0 + (a - 1.0) * ka_ref[:, cols])
        r_out[:, cols] = r
        k_out[:, cols] = k2
        v_out[:, cols] = v
        kk_out[:, cols] = kk
        b_out[:, cols] = kk * a
        bonus_out[:, cols] = _head_sum(r * k2 * rk_ref[:, cols]) * v


def _sc_gate_kernel(x_ref, sc_state_ref, n1g_ref, w_sc_ref, w_gate_ref, b_gate_ref, conv_ref,
                    w_bsc_ref, ga_out, gbo_out, xb_out, sc_out, c_sc, *, s):
    @pl.when(pl.program_id(1) == 0)
    def _():
        c_sc[...] = sc_state_ref[...]

    rows = x_ref.shape[0]
    xb = _rms_norm(x_ref[...], n1g_ref[...]).astype(BF16)
    xb_out[...] = xb

    def proj(c):
        cs = lambda base: slice(base + COL_CHUNK * c, base + COL_CHUNK * (c + 1))
        sc = [jnp.dot(xb, w_sc_ref[:, cs(base)], preferred_element_type=F32)
              for base in (0, D_MODEL, 2 * D_MODEL)]
        gate = [jnp.dot(xb, w_gate_ref[:, cs(base)], preferred_element_type=F32) + b_gate_ref[:, cs(base)]
                for base in (0, D_MODEL)]
        return sc + gate

    mid = []
    gb = []
    nxt = proj(0)
    for c in range(D_MODEL // COL_CHUNK):
        cur = nxt
        if c + 1 < D_MODEL // COL_CHUNK:
            nxt = proj(c + 1)
        cols = slice(COL_CHUNK * c, COL_CHUNK * (c + 1))
        h, b_g, c_g, pg_a, pg_b = cur
        u = c_g * h
        carry = c_sc[:, cols]
        prev1 = _shift_rows(u, carry[s:], s)
        prev2 = _shift_rows(prev1, carry[:s], s)
        cw = conv_ref[:, cols]
        conv = prev2 * cw[0:1] + prev1 * cw[1:2] + u * cw[2:3]
        mid.append((b_g * conv).astype(BF16))
        new = u[rows - 2 * s:]
        c_sc[:, cols] = new
        sc_out[:, cols] = new
        ga_out[:, cols] = _sigmoid(pg_a)
        gb.append(_sigmoid(pg_b))
    o_b = jnp.dot(jnp.concatenate(mid, axis=1), w_bsc_ref[...], preferred_element_type=F32)
    for c in range(D_MODEL // COL_CHUNK):
        cols = slice(COL_CHUNK * c, COL_CHUNK * (c + 1))
        gbo_out[:, cols] = gb[c] * o_b[:, cols]


def _wkv_kernel(r_ref, lw_ref, k_ref, v_ref, kk_ref, b_ref, s0_ref, y_out, s_out, s_scr, *, C, nq):
    c_idx = pl.program_id(1)

    @pl.when(c_idx == 0)
    def _():
        zh = jnp.zeros((HEAD_SIZE, HEAD_SIZE), F32)
        for q in range(nq):
            for p in range(N_PAIRS):
                s_scr[q * N_PAIRS + p] = jnp.concatenate(
                    [jnp.concatenate([s0_ref[q, 2 * p], zh], axis=1),
                     jnp.concatenate([zh, s0_ref[q, 2 * p + 1]], axis=1)], axis=0)

    ti = lax.broadcasted_iota(jnp.int32, (C, C), 0)
    tj = lax.broadcasted_iota(jnp.int32, (C, C), 1)
    tri = jnp.where(tj <= ti, 1.0, 0.0).astype(BF16)
    m0 = lax.broadcasted_iota(jnp.int32, (C, LANES), 1) < HEAD_SIZE
    gi = lax.broadcasted_iota(jnp.int32, (C, 2 * C), 0)
    gj = lax.broadcasted_iota(jnp.int32, (C, 2 * C), 1)
    gjm = jnp.where(gj >= C, gj - C, gj)
    strict = gjm < gi
    incl = gjm <= gi
    left = gj < C
    strict_l = gj < gi
    strict_r = jnp.logical_and(gj >= C, gj - C < gi)
    ei = lax.broadcasted_iota(jnp.int32, (2 * C, 2 * C), 0)
    ej = lax.broadcasted_iota(jnp.int32, (2 * C, 2 * C), 1)
    eye = jnp.where(ei == ej, 1.0, 0.0)
    bi = lax.broadcasted_iota(jnp.int32, (LANES, LANES), 0) // HEAD_SIZE
    bj = lax.broadcasted_iota(jnp.int32, (LANES, LANES), 1) // HEAD_SIZE
    bd = bi == bj

    def swap_halves(x):
        if 2 * C == LANES:
            return pltpu.roll(x, C, axis=1)
        return jnp.concatenate([x[:, C:], x[:, :C]], axis=1)

    a_p, r_p, v_p, bk, w_last = [], [], [], [], []
    for q in range(nq):
        lw = lw_ref[q]
        h1 = lw.astype(BF16)
        h2 = (lw - h1.astype(F32)).astype(BF16)
        cum = jnp.dot(tri, h1, preferred_element_type=F32) + jnp.dot(tri, h2, preferred_element_type=F32)
        w_t = jnp.exp(cum)
        w_inv = jnp.exp(-cum)
        a_t = -kk_ref[q] * jnp.exp(cum - lw)
        b_t = b_ref[q] * w_inv
        k_t = k_ref[q] * w_inv
        r_t = r_ref[q] * w_t
        v_q = v_ref[q]
        for p in range(N_PAIRS):
            sl = slice(LANES * p, LANES * (p + 1))
            a_p.append(a_t[:, sl])
            r_p.append(r_t[:, sl])
            v_p.append(v_q[:, sl])
            bk.append(jnp.concatenate([b_t[:, sl], k_t[:, sl]], axis=0))
            w_last.append(w_t[C - 1:C, sl])
    ents = range(nq * N_PAIRS)

    g = [_dot_nt(jnp.concatenate([jnp.where(m0, a_p[e], 0.0), jnp.where(m0, r_p[e], 0.0),
                                  jnp.where(m0, 0.0, a_p[e]), jnp.where(m0, 0.0, r_p[e])], axis=0), bk[e])
         for e in ents]
    s_bd = [s_scr[e] for e in ents]
    asrs = [_dot_nt(jnp.concatenate([a_p[e], r_p[e]], axis=0), s_bd[e]) for e in ents]
    g1a = [swap_halves(g[e][2 * C:3 * C]) for e in ents]
    xpow = [jnp.concatenate([jnp.where(strict_l, g[e][:C], 0.0), jnp.where(strict_r, g1a[e], 0.0)], axis=0)
            for e in ents]
    tinv = [eye + xpow[e] for e in ents]
    n = 1
    if 2 * n < C:
        xpow = [_dot(xpow[e], xpow[e]) for e in ents]
        n *= 2
    while 2 * n < C:
        z = [_dot(jnp.concatenate([xpow[e], tinv[e]], axis=0), xpow[e]) for e in ents]
        xpow = [z[e][:2 * C] for e in ents]
        tinv = [tinv[e] + z[e][2 * C:] for e in ents]
        n *= 2
    if C > 2:
        tinv = [tinv[e] + _dot(tinv[e], xpow[e]) for e in ents]
    vm0 = [jnp.where(m0, v_p[e], 0.0) for e in ents]
    vm1 = [jnp.where(m0, 0.0, v_p[e]) for e in ents]
    mak = [jnp.where(strict, jnp.where(left, g1a[e], g[e][:C]), 0.0) for e in ents]
    rhs = [asrs[e][:C] + _dot(mak[e], jnp.concatenate([vm1[e], vm0[e]], axis=0)) for e in ents]
    uu = [_dot(tinv[e], jnp.concatenate([jnp.where(m0, rhs[e], 0.0), jnp.where(m0, 0.0, rhs[e])], axis=0))
          for e in ents]
    u = [uu[e][:C] + uu[e][C:] for e in ents]
    for e in ents:
        q, p = divmod(e, N_PAIRS)
        um0 = jnp.where(m0, u[e], 0.0)
        um1 = jnp.where(m0, 0.0, u[e])
        g0r = jnp.where(incl, g[e][C:2 * C], 0.0)
        g1r = jnp.where(incl, g[e][3 * C:], 0.0)
        if 2 * C == LANES:
            intra = _dot(jnp.concatenate([g0r, g1r], axis=1),
                         jnp.concatenate([um0, vm0[e], um1, vm1[e]], axis=0))
        else:
            intra = (_dot(g0r, jnp.concatenate([um0, vm0[e]], axis=0))
                     + _dot(g1r, jnp.concatenate([um1, vm1[e]], axis=0)))
        y_out[q, :, LANES * p:LANES * (p + 1)] = asrs[e][C:] + intra
    for e in ents:
        upd = _dot_tn(jnp.concatenate([u[e], v_p[e]], axis=0), bk[e])
        s_scr[e] = (s_bd[e] + jnp.where(bd, upd, 0.0)) * w_last[e]

    @pl.when(c_idx == pl.num_programs(1) - 1)
    def _():
        for q in range(nq):
            for p in range(N_PAIRS):
                s_fin = s_scr[q * N_PAIRS + p]
                s_out[q, 2 * p] = s_fin[:HEAD_SIZE, :HEAD_SIZE]
                s_out[q, 2 * p + 1] = s_fin[HEAD_SIZE:, HEAD_SIZE:]


def _post_kernel(x_ref, y_ref, bonus_ref, g_ref, ga_ref, gbo_ref, ffn_state_ref, lnx_g_ref, lnx_b_ref,
                 w_br_ref, w_out_ref, n2g_ref, w_up_ref, conv_ref, w_down_ref, fng_ref,
                 out_ref, ffn_out, c_ffn, *, s):
    @pl.when(pl.program_id(1) == 0)
    def _():
        c_ffn[...] = ffn_state_ref[...]

    rows = x_ref.shape[0]
    z = []
    for c in range(D_MODEL // COL_CHUNK):
        cols = slice(COL_CHUNK * c, COL_CHUNK * (c + 1))
        y = y_ref[:, cols]
        dev = y - _head_sum(y) * (1.0 / HEAD_SIZE)
        var = _head_sum(dev * dev) * (1.0 / HEAD_SIZE)
        yn = dev * lax.rsqrt(var + GN_EPS) * lnx_g_ref[:, cols] + lnx_b_ref[:, cols]
        z.append(((yn + bonus_ref[:, cols]) * g_ref[:, cols]).astype(BF16))
    o_a = jnp.dot(jnp.concatenate(z, axis=1), w_br_ref[...], preferred_element_type=F32)
    x1 = x_ref[...] + _dot(ga_ref[...] * o_a + gbo_ref[...], w_out_ref[...])

    xb = _rms_norm(x1, n2g_ref[...]).astype(BF16)
    n_chunk = D_FF // FF_CHUNK

    def up_proj(j):
        return [jnp.dot(xb, w_up_ref[:, base + FF_CHUNK * j:base + FF_CHUNK * (j + 1)],
                        preferred_element_type=F32) for base in (0, D_FF)]

    hidden = []
    ups = up_proj(0)
    for j in range(n_chunk):
        cur = ups
        if j + 1 < n_chunk:
            ups = up_proj(j + 1)
        halves = []
        for base, up in zip((0, D_FF), cur):
            cols = slice(base + FF_CHUNK * j, base + FF_CHUNK * (j + 1))
            carry = c_ffn[:, cols]
            prev1 = _shift_rows(up, carry[s:], s)
            prev2 = _shift_rows(prev1, carry[:s], s)
            cw = conv_ref[:, cols]
            halves.append(prev2 * cw[0:1] + prev1 * cw[1:2] + up * cw[2:3])
            new = up[rows - 2 * s:]
            c_ffn[:, cols] = new
            ffn_out[:, cols] = new
        gate, val = halves
        hidden.append((gate * _sigmoid(gate) * val).astype(BF16))
    down = jnp.dot(jnp.concatenate(hidden, axis=1), w_down_ref[...], preferred_element_type=F32)
    out_ref[...] = _rms_norm(x1 + down, fng_ref[...])


def _const_spec(arr):
    nd = arr.ndim
    return pl.BlockSpec(arr.shape, lambda b, t: (0,) * nd, pipeline_mode=pl.Buffered(1))


def _row_spec(rows, cols):
    return pl.BlockSpec((None, rows, cols), lambda b, t: (b, t, 0))


def _state_spec(rows, cols):
    return pl.BlockSpec((None, rows, cols), lambda b, t: (b, 0, 0))


def _params():
    return pltpu.CompilerParams(dimension_semantics=("arbitrary", "arbitrary"),
                                vmem_limit_bytes=VMEM_LIMIT)


def _rwkv_prep(xb, sh_rkv, sh_lora, w, *, rows, s):
    nb, total, _ = xb.shape
    grid = (nb, total // rows)
    consts = [w["w_rkv"], w["w_lora"], w["mu_rkv"], w["mu_lora"], w["w0"], w["wd"],
              w["a0"], w["wa"], w["wg"], w["k_k"], w["k_a"], w["r_k"]]
    tok = jax.ShapeDtypeStruct((nb, total, D_MODEL), F32)
    return pl.pallas_call(
        functools.partial(_rwkv_prep_kernel, s=s),
        grid=grid,
        in_specs=[_row_spec(rows, D_MODEL), _state_spec(s, 3 * D_MODEL), _state_spec(s, D_LORA_PAD)]
                 + [_const_spec(c) for c in consts],
        out_specs=[_row_spec(rows, D_MODEL)] * 8
                  + [_state_spec(s, 3 * D_MODEL), _state_spec(s, D_LORA_PAD)],
        out_shape=[tok] * 8 + [jax.ShapeDtypeStruct((nb, s, 3 * D_MODEL), F32),
                               jax.ShapeDtypeStruct((nb, s, D_LORA_PAD), F32)],
        scratch_shapes=[pltpu.VMEM((s, 3 * D_MODEL), F32), pltpu.VMEM((s, D_LORA_PAD), F32)],
        compiler_params=_params(),
        name="rwkv_prep",
    )(xb, sh_rkv, sh_lora, *consts)


def _sc_gate(x, sc_state, w, *, rows, s):
    nb, total, _ = x.shape
    grid = (nb, total // rows)
    consts = [w["norm1_g"], w["w_sc"], w["w_gate"], w["b_gate"], w["conv_sc"], w["w_branch_sc"]]
    tok = jax.ShapeDtypeStruct((nb, total, D_MODEL), F32)
    return pl.pallas_call(
        functools.partial(_sc_gate_kernel, s=s),
        grid=grid,
        in_specs=[_row_spec(rows, D_MODEL), _state_spec(2 * s, D_MODEL)] + [_const_spec(c) for c in consts],
        out_specs=[_row_spec(rows, D_MODEL)] * 3 + [_state_spec(2 * s, D_MODEL)],
        out_shape=[tok] * 2 + [jax.ShapeDtypeStruct((nb, total, D_MODEL), BF16),
                               jax.ShapeDtypeStruct((nb, 2 * s, D_MODEL), F32)],
        scratch_shapes=[pltpu.VMEM((2 * s, D_MODEL), F32)],
        compiler_params=_params(),
        name="sc_gate",
    )(x, sc_state, *consts)


def _wkv(r, lw, k, v, kk, b, s0, *, chunk, nq):
    nb, total, _ = r.shape
    grid = (nb // nq, total // chunk)
    tok_spec = pl.BlockSpec((nq, chunk, D_MODEL), lambda i, c: (i, c, 0))
    st_spec = pl.BlockSpec((nq, N_HEADS, HEAD_SIZE, HEAD_SIZE), lambda i, c: (i, 0, 0, 0))
    return pl.pallas_call(
        functools.partial(_wkv_kernel, C=chunk, nq=nq),
        grid=grid,
        in_specs=[tok_spec] * 6 + [st_spec],
        out_specs=[tok_spec, st_spec],
        out_shape=[jax.ShapeDtypeStruct((nb, total, D_MODEL), F32),
                   jax.ShapeDtypeStruct((nb, N_HEADS, HEAD_SIZE, HEAD_SIZE), F32)],
        scratch_shapes=[pltpu.VMEM((nq * N_PAIRS, LANES, LANES), F32)],
        compiler_params=_params(),
        name="wkv",
    )(r, lw, k, v, kk, b, s0)


def _post(x, y, bonus, g, ga, gbo, ffn_state, w, *, rows, s):
    nb, total, _ = x.shape
    grid = (nb, total // rows)
    consts = [w["lnx_g"], w["lnx_b"], w["w_branch_rwkv"], w["w_out"], w["norm2_g"], w["w_up"],
              w["conv_ffn"], w["w_down"], w["final_norm_g"]]
    return pl.pallas_call(
        functools.partial(_post_kernel, s=s),
        grid=grid,
        in_specs=[_row_spec(rows, D_MODEL)] * 6 + [_state_spec(2 * s, 2 * D_FF)]
                 + [_const_spec(c) for c in consts],
        out_specs=[_row_spec(rows, D_MODEL), _state_spec(2 * s, 2 * D_FF)],
        out_shape=[jax.ShapeDtypeStruct((nb, total, D_MODEL), F32),
                   jax.ShapeDtypeStruct((nb, 2 * s, 2 * D_FF), F32)],
        scratch_shapes=[pltpu.VMEM((2 * s, 2 * D_FF), F32)],
        compiler_params=_params(),
        name="post",
    )(x, y, bonus, g, ga, gbo, ffn_state, *consts)


def _pad_lora_cols(a):
    pad = lambda t, n: jnp.pad(t, [(0, 0)] * (t.ndim - 1) + [(0, n - t.shape[-1])])
    return jnp.concatenate([pad(a[..., :64], 128), pad(a[..., 64:128], 128), pad(a[..., 128:], 256)], axis=-1)


def _unpad_lora_cols(a):
    return jnp.concatenate([a[..., :64], a[..., 128:192], a[..., 256:256 + D_GATE_LORA]], axis=-1)


def _prep_weights(norm1_g, w_in, b_gate, mu_shift, w0, w_decay_up, a0, w_aaa_up, w_gate_up, k_k, k_a,
                  r_k, lnx_g, lnx_b, w_branch_rwkv, w_branch_sc, conv_sc, w_out, norm2_g, w_up, conv_ffn,
                  w_down, final_norm_g):
    row = lambda t: t.reshape(1, -1).astype(F32)
    d3 = 3 * D_MODEL
    n_lora = D_DECAY_LORA + D_AAA_LORA + D_GATE_LORA
    pad_rows = lambda t, n: jnp.pad(t, [(0, n - t.shape[0]), (0, 0)])
    return {
        "norm1_g": row(norm1_g),
        "w_rkv": w_in[:, :d3].astype(BF16),
        "w_lora": _pad_lora_cols(w_in[:, d3:d3 + n_lora]).astype(BF16),
        "w_sc": w_in[:, d3 + n_lora:2 * d3 + n_lora].astype(BF16),
        "w_gate": w_in[:, 2 * d3 + n_lora:].astype(BF16),
        "b_gate": row(b_gate),
        "mu_rkv": row(mu_shift[:d3]),
        "mu_lora": _pad_lora_cols(row(mu_shift[d3:])),
        "w0": row(w0),
        "wd": pad_rows(w_decay_up, 128).astype(BF16),
        "a0": row(a0),
        "wa": pad_rows(w_aaa_up, 128).astype(BF16),
        "wg": pad_rows(w_gate_up, 256).astype(BF16),
        "k_k": row(k_k), "k_a": row(k_a), "r_k": row(r_k),
        "lnx_g": row(lnx_g), "lnx_b": row(lnx_b),
        "w_branch_rwkv": w_branch_rwkv.astype(BF16),
        "w_branch_sc": w_branch_sc.astype(BF16),
        "conv_sc": conv_sc.astype(F32),
        "w_out": w_out.astype(BF16),
        "norm2_g": row(norm2_g),
        "w_up": w_up.astype(BF16),
        "conv_ffn": conv_ffn.astype(F32),
        "w_down": w_down.astype(BF16),
        "final_norm_g": row(final_norm_g),
    }


def _layer(x, s_wkv, sh_rkv, sh_lora, sc_state, ffn_state, w, *, rows, s, to_seq, from_seq, chunk, nq):
    ga, gbo, xb, sc_new = _sc_gate(x, sc_state, w, rows=rows, s=s)
    r, lw, k, v, kk, b, g, bonus, shr, shl = _rwkv_prep(xb, sh_rkv, sh_lora, w, rows=rows, s=s)
    y_seq, s_new = _wkv(*[to_seq(t) for t in (r, lw, k, v, kk, b)], s_wkv, chunk=chunk, nq=nq)
    out, ffn_new = _post(x, from_seq(y_seq), bonus, g, ga, gbo, ffn_state, w, rows=rows, s=s)
    return out, s_new, shr, shl, sc_new, ffn_new


def kernel(x_prompt, x_sample, state_wkv, state_shift, state_sc_conv, state_ffn_conv, meta_tokens,
           norm1_g, w_in, b_gate, mu_shift, w0, w_decay_up, a0, w_aaa_up, w_gate_up, k_k, k_a, r_k,
           lnx_g, lnx_b, w_branch_rwkv, w_branch_sc, conv_sc, w_out, norm2_g, w_up, conv_ffn, w_down,
           final_norm_g):
    w = _prep_weights(norm1_g[0], w_in[0], b_gate[0], mu_shift[0], w0[0], w_decay_up[0], a0[0],
                      w_aaa_up[0], w_gate_up[0], k_k[0], k_a[0], r_k[0], lnx_g[0], lnx_b[0],
                      w_branch_rwkv[0], w_branch_sc[0], conv_sc[0], w_out[0], norm2_g[0], w_up[0],
                      conv_ffn[0], w_down[0], final_norm_g)
    d3 = 3 * D_MODEL
    ident = lambda t: t

    bp, seq, _ = x_prompt.shape
    zeros = lambda *shape: jnp.zeros(shape, F32)
    _, m_wkv, m_shr, m_shl, m_sc, m_ffn = _layer(
        meta_tokens.astype(F32)[None], zeros(1, N_HEADS, HEAD_SIZE, HEAD_SIZE), zeros(1, 1, d3),
        zeros(1, 1, D_LORA_PAD), zeros(1, 2, D_MODEL), zeros(1, 2, 2 * D_FF), w,
        rows=N_META, s=1, to_seq=ident, from_seq=ident, chunk=N_META, nq=1)

    rep = lambda t: jnp.broadcast_to(t, (bp,) + t.shape[1:])
    y_prompt, p_wkv, p_shr, p_shl, p_sc, p_ffn = _layer(
        x_prompt, rep(m_wkv), rep(m_shr), rep(m_shl), rep(m_sc), rep(m_ffn), w,
        rows=256, s=1, to_seq=ident, from_seq=ident, chunk=64, nq=2)

    bs, ts, _ = x_sample.shape
    n_grp = 2
    gsz = bs // n_grp
    t_pad = 8

    def to_rows(t):
        n, c = t.shape[1:]
        return t.reshape(n_grp, gsz, n, c).transpose(0, 2, 1, 3).reshape(n_grp, n * gsz, c)

    def to_seq(t):
        c = t.shape[-1]
        t = t.reshape(n_grp, ts, gsz, c).transpose(0, 2, 1, 3).reshape(bs, ts, c)
        return jnp.pad(t, ((0, 0), (0, t_pad - ts), (0, 0)))

    def from_seq(t):
        return to_rows(t[:, :ts])

    conv_rows = lambda st: to_rows(st)
    sh = state_shift[0].reshape(n_grp, gsz, -1)
    y_s, s_wkv, s_shr, s_shl, s_sc, s_ffn = _layer(
        to_rows(x_sample), state_wkv[0], sh[..., :d3], _pad_lora_cols(sh[..., d3:]),
        conv_rows(state_sc_conv[0]), conv_rows(state_ffn_conv[0]), w,
        rows=ts * gsz, s=gsz, to_seq=to_seq, from_seq=from_seq, chunk=t_pad, nq=4)

    def rows_to_batch(t, n):
        c = t.shape[-1]
        return t.reshape(n_grp, n, gsz, c).transpose(0, 2, 1, 3).reshape(bs, n, c)

    y_sample = rows_to_batch(y_s, ts)
    shift_p = jnp.concatenate([p_shr[:, 0], _unpad_lora_cols(p_shl[:, 0])], axis=-1)
    shift_s = jnp.concatenate([rows_to_batch(s_shr, 1)[:, 0], _unpad_lora_cols(rows_to_batch(s_shl, 1)[:, 0])],
                              axis=-1)
    return (y_prompt, y_sample,
            p_wkv[None], s_wkv[None],
            shift_p[None], shift_s[None],
            p_sc[None], rows_to_batch(s_sc, 2)[None],
            p_ffn[None], rows_to_batch(s_ffn, 2)[None])
```

```python
import functools

import jax
import jax.numpy as jnp
from jax import lax
from jax.experimental import pallas as pl
from jax.experimental.pallas import tpu as pltpu

D_MODEL = 1024
N_META = 16
HEAD_SIZE = 64
N_HEADS = D_MODEL // HEAD_SIZE
LANES = 128
N_PAIRS = D_MODEL // LANES
D_DECAY_LORA = 64
D_AAA_LORA = 64
D_GATE_LORA = 160
D_LORA_PAD = 512
D_FF = 2816
FF_CHUNK = 256
COL_CHUNK = 256
CONV_W = 3
NEG_LOG2_E = -1.4426950408889634
EXP_NEG_HALF = 0.6065306597126334
RMS_EPS = 1e-6
GN_EPS = 64e-5
VMEM_LIMIT = 60 * 1024 * 1024

F32 = jnp.float32
BF16 = jnp.bfloat16


def _dot(a, b):
    return jnp.dot(a.astype(BF16), b.astype(BF16), preferred_element_type=F32)


def _dot_nt(a, b):
    return lax.dot_general(a.astype(BF16), b.astype(BF16), (((1,), (1,)), ((), ())),
                           preferred_element_type=F32)


def _dot_tn(a, b):
    return lax.dot_general(a.astype(BF16), b.astype(BF16), (((0,), (0,)), ((), ())),
                           preferred_element_type=F32)


def _head_ones(n):
    r = lax.broadcasted_iota(jnp.int32, (n, n), 0) // HEAD_SIZE
    c = lax.broadcasted_iota(jnp.int32, (n, n), 1) // HEAD_SIZE
    return jnp.where(r == c, 1.0, 0.0).astype(BF16)


def _head_sum(x):
    return jnp.dot(x.astype(BF16), _head_ones(x.shape[1]), preferred_element_type=F32)


def _rms_norm(x, g):
    return x * lax.rsqrt(jnp.mean(x * x, axis=-1, keepdims=True) + RMS_EPS) * g


def _sigmoid(x):
    return 1.0 / (1.0 + jnp.exp2(x * NEG_LOG2_E))


def _shift_rows(cur, carry, s):
    rows = cur.shape[0]
    if s % 8 == 0:
        return jnp.concatenate([carry, cur[:rows - s]], axis=0)
    assert s == 1
    rolled = pltpu.roll(cur, 1, axis=0)
    row = lax.broadcasted_iota(jnp.int32, cur.shape, 0)
    return jnp.where(row == 0, carry, rolled)


def _rwkv_prep_kernel(xb_ref, sh_rkv_ref, sh_lora_ref, w_rkv_ref, w_lora_ref, mu_rkv_ref,
                      mu_lora_ref, w0_ref, wd_ref, a0_ref, wa_ref, wg_ref, kk_ref, ka_ref, rk_ref,
                      r_out, lw_out, k_out, v_out, kk_out, b_out, g_out, bonus_out, shr_out, shl_out,
                      c_rkv, c_lora, *, s):
    @pl.when(pl.program_id(1) == 0)
    def _():
        c_rkv[...] = sh_rkv_ref[...]
        c_lora[...] = sh_lora_ref[...]

    rows = xb_ref.shape[0]
    xb = xb_ref[...]
    p_lora = jnp.dot(xb, w_lora_ref[...], preferred_element_type=F32)
    prev_lora = _shift_rows(p_lora, c_lora[...], s)
    xl = p_lora + (prev_lora - p_lora) * mu_lora_ref[...]
    new_lora = p_lora[rows - s:]
    c_lora[...] = new_lora
    shl_out[...] = new_lora
    t_xw = jnp.tanh(xl[:, :128]).astype(BF16)
    xa = xl[:, 128:256].astype(BF16)
    s_xg = _sigmoid(xl[:, 256:]).astype(BF16)

    def proj(c):
        return [jnp.dot(xb, w_rkv_ref[:, base + COL_CHUNK * c:base + COL_CHUNK * (c + 1)],
                        preferred_element_type=F32) for base in (0, D_MODEL, 2 * D_MODEL)]

    nxt = proj(0)
    for c in range(D_MODEL // COL_CHUNK):
        cur = nxt
        if c + 1 < D_MODEL // COL_CHUNK:
            nxt = proj(c + 1)
        cols = slice(COL_CHUNK * c, COL_CHUNK * (c + 1))
        shifted = []
        for part, p in enumerate(cur):
            pcols = slice(part * D_MODEL + COL_CHUNK * c, part * D_MODEL + COL_CHUNK * (c + 1))
            prev = _shift_rows(p, c_rkv[:, pcols], s)
            shifted.append(p + (prev - p) * mu_rkv_ref[:, pcols])
            new = p[rows - s:]
            c_rkv[:, pcols] = new
            shr_out[:, pcols] = new
        r, k, v = shifted

        zw = w0_ref[:, cols] + jnp.dot(t_xw, wd_ref[:, cols], preferred_element_type=F32)
        lw_out[:, cols] = _sigmoid(zw) * (-EXP_NEG_HALF)
        a = _sigmoid(a0_ref[:, cols] + jnp.dot(xa, wa_ref[:, cols], preferred_element_type=F32))
        g_out[:, cols] = jnp.dot(s_xg, wg_ref[:, cols], preferred_element_type=F32)

        kkr = k * kk_ref[:, cols]
        kk = kkr * lax.rsqrt(jnp.maximum(_head_sum(kkr * kkr), 1e-24))
        k2 = k * (1.0 + (a - 1.0) * ka_ref[:, cols])
        r_out[:, cols] = r
        k_out[:, cols] = k2
        v_out[:, cols] = v
        kk_out[:, cols] = kk
        b_out[:, cols] = kk * a
        bonus_out[:, cols] = _head_sum(r * k2 * rk_ref[:, cols]) * v


def _sc_gate_kernel(x_ref, sc_state_ref, n1g_ref, w_sc_ref, w_gate_ref, b_gate_ref, conv_ref,
                    w_bsc_ref, ga_out, gbo_out, xb_out, sc_out, c_sc, *, s):
    @pl.when(pl.program_id(1) == 0)
    def _():
        c_sc[...] = sc_state_ref[...]

    rows = x_ref.shape[0]
    xb = _rms_norm(x_ref[...], n1g_ref[...]).astype(BF16)
    xb_out[...] = xb

    def proj(c):
        cs = lambda base: slice(base + COL_CHUNK * c, base + COL_CHUNK * (c + 1))
        sc = [jnp.dot(xb, w_sc_ref[:, cs(base)], preferred_element_type=F32)
              for base in (0, D_MODEL, 2 * D_MODEL)]
        gate = [jnp.dot(xb, w_gate_ref[:, cs(base)], preferred_element_type=F32) + b_gate_ref[:, cs(base)]
                for base in (0, D_MODEL)]
        return sc + gate

    mid = []
    gb = []
    nxt = proj(0)
    for c in range(D_MODEL // COL_CHUNK):
        cur = nxt
        if c + 1 < D_MODEL // COL_CHUNK:
            nxt = proj(c + 1)
        cols = slice(COL_CHUNK * c, COL_CHUNK * (c + 1))
        h, b_g, c_g, pg_a, pg_b = cur
        u = c_g * h
        carry = c_sc[:, cols]
        prev1 = _shift_rows(u, carry[s:], s)
        prev2 = _shift_rows(prev1, carry[:s], s)
        cw = conv_ref[:, cols]
        conv = prev2 * cw[0:1] + prev1 * cw[1:2] + u * cw[2:3]
        mid.append((b_g * conv).astype(BF16))
        new = u[rows - 2 * s:]
        c_sc[:, cols] = new
        sc_out[:, cols] = new
        ga_out[:, cols] = _sigmoid(pg_a)
        gb.append(_sigmoid(pg_b))
    o_b = jnp.dot(jnp.concatenate(mid, axis=1), w_bsc_ref[...], preferred_element_type=F32)
    for c in range(D_MODEL // COL_CHUNK):
        cols = slice(COL_CHUNK * c, COL_CHUNK * (c + 1))
        gbo_out[:, cols] = gb[c] * o_b[:, cols]


def _wkv_kernel(r_ref, lw_ref, k_ref, v_ref, kk_ref, b_ref, s0_ref, y_out, s_out, s_scr, *, C, nq):
    c_idx = pl.program_id(1)

    @pl.when(c_idx == 0)
    def _():
        zh = jnp.zeros((HEAD_SIZE, HEAD_SIZE), F32)
        for q in range(nq):
            for p in range(N_PAIRS):
                s_scr[q * N_PAIRS + p] = jnp.concatenate(
                    [jnp.concatenate([s0_ref[q, 2 * p], zh], axis=1),
                     jnp.concatenate([zh, s0_ref[q, 2 * p + 1]], axis=1)], axis=0)

    ti = lax.broadcasted_iota(jnp.int32, (C, C), 0)
    tj = lax.broadcasted_iota(jnp.int32, (C, C), 1)
    tri = jnp.where(tj <= ti, 1.0, 0.0).astype(BF16)
    m0 = lax.broadcasted_iota(jnp.int32, (C, LANES), 1) < HEAD_SIZE
    gi = lax.broadcasted_iota(jnp.int32, (C, 2 * C), 0)
    gj = lax.broadcasted_iota(jnp.int32, (C, 2 * C), 1)
    gjm = jnp.where(gj >= C, gj - C, gj)
    strict = gjm < gi
    incl = gjm <= gi
    left = gj < C
    strict_l = gj < gi
    strict_r = jnp.logical_and(gj >= C, gj - C < gi)
    ei = lax.broadcasted_iota(jnp.int32, (2 * C, 2 * C), 0)
    ej = lax.broadcasted_iota(jnp.int32, (2 * C, 2 * C), 1)
    eye = jnp.where(ei == ej, 1.0, 0.0)
    bi = lax.broadcasted_iota(jnp.int32, (LANES, LANES), 0) // HEAD_SIZE
    bj = lax.broadcasted_iota(jnp.int32, (LANES, LANES), 1) // HEAD_SIZE
    bd = bi == bj

    def swap_halves(x):
        if 2 * C == LANES:
            return pltpu.roll(x, C, axis=1)
        return jnp.concatenate([x[:, C:], x[:, :C]], axis=1)

    a_p, r_p, v_p, bk, w_last = [], [], [], [], []
    for q in range(nq):
        lw = lw_ref[q]
        h1 = lw.astype(BF16)
        h2 = (lw - h1.astype(F32)).astype(BF16)
        cum = jnp.dot(tri, h1, preferred_element_type=F32) + jnp.dot(tri, h2, preferred_element_type=F32)
        w_t = jnp.exp(cum)
        w_inv = jnp.exp(-cum)
        a_t = -kk_ref[q] * jnp.exp(cum - lw)
        b_t = b_ref[q] * w_inv
        k_t = k_ref[q] * w_inv
        r_t = r_ref[q] * w_t
        v_q = v_ref[q]
        for p in range(N_PAIRS):
            sl = slice(LANES * p, LANES * (p + 1))
            a_p.append(a_t[:, sl])
            r_p.append(r_t[:, sl])
            v_p.append(v_q[:, sl])
            bk.append(jnp.concatenate([b_t[:, sl], k_t[:, sl]], axis=0))
            w_last.append(w_t[C - 1:C, sl])
    ents = range(nq * N_PAIRS)

    g = [_dot_nt(jnp.concatenate([jnp.where(m0, a_p[e], 0.0), jnp.where(m0, r_p[e], 0.0),
                                  jnp.where(m0, 0.0, a_p[e]), jnp.where(m0, 0.0, r_p[e])], axis=0), bk[e])
         for e in ents]
    s_bd = [s_scr[e] for e in ents]
    asrs = [_dot_nt(jnp.concatenate([a_p[e], r_p[e]], axis=0), s_bd[e]) for e in ents]
    g1a = [swap_halves(g[e][2 * C:3 * C]) for e in ents]
    xpow = [jnp.concatenate([jnp.where(strict_l, g[e][:C], 0.0), jnp.where(strict_r, g1a[e], 0.0)], axis=0)
            for e in ents]
    tinv = [eye + xpow[e] for e in ents]
    n = 1
    if 2 * n < C:
        xpow = [_dot(xpow[e], xpow[e]) for e in ents]
        n *= 2
    while 2 * n < C:
        z = [_dot(jnp.concatenate([xpow[e], tinv[e]], axis=0), xpow[e]) for e in ents]
        xpow = [z[e][:2 * C] for e in ents]
        tinv = [tinv[e] + z[e][2 * C:] for e in ents]
        n *= 2
    if C > 2:
        tinv = [tinv[e] + _dot(tinv[e], xpow[e]) for e in ents]
    vm0 = [jnp.where(m0, v_p[e], 0.0) for e in ents]
    vm1 = [jnp.where(m0, 0.0, v_p[e]) for e in ents]
    mak = [jnp.where(strict, jnp.where(left, g1a[e], g[e][:C]), 0.0) for e in ents]
    rhs = [asrs[e][:C] + _dot(mak[e], jnp.concatenate([vm1[e], vm0[e]], axis=0)) for e in ents]
    uu = [_dot(tinv[e], jnp.concatenate([jnp.where(m0, rhs[e], 0.0), jnp.where(m0, 0.0, rhs[e])], axis=0))
          for e in ents]
    u = [uu[e][:C] + uu[e][C:] for e in ents]
    for e in ents:
        q, p = divmod(e, N_PAIRS)
        um0 = jnp.where(m0, u[e], 0.0)
        um1 = jnp.where(m0, 0.0, u[e])
        g0r = jnp.where(incl, g[e][C:2 * C], 0.0)
        g1r = jnp.where(incl, g[e][3 * C:], 0.0)
        if 2 * C == LANES:
            intra = _dot(jnp.concatenate([g0r, g1r], axis=1),
                         jnp.concatenate([um0, vm0[e], um1, vm1[e]], axis=0))
        else:
            intra = (_dot(g0r, jnp.concatenate([um0, vm0[e]], axis=0))
                     + _dot(g1r, jnp.concatenate([um1, vm1[e]], axis=0)))
        y_out[q, :, LANES * p:LANES * (p + 1)] = asrs[e][C:] + intra
    for e in ents:
        upd = _dot_tn(jnp.concatenate([u[e], v_p[e]], axis=0), bk[e])
        s_scr[e] = (s_bd[e] + jnp.where(bd, upd, 0.0)) * w_last[e]

    @pl.when(c_idx == pl.num_programs(1) - 1)
    def _():
        for q in range(nq):
            for p in range(N_PAIRS):
                s_fin = s_scr[q * N_PAIRS + p]
                s_out[q, 2 * p] = s_fin[:HEAD_SIZE, :HEAD_SIZE]
                s_out[q, 2 * p + 1] = s_fin[HEAD_SIZE:, HEAD_SIZE:]


def _wkv_lanes_kernel(r_ref, lw_ref, k_ref, v_ref, kk_ref, b_ref, s0_ref, y_out, s_out,
                      r_t, w_t, k_t, v_t, a_t, b_t, y_t, *, n_t, gsz):
    n_grp = r_ref.shape[0]

    def by_channel(ref, t):
        rows = [ref[g, t * gsz:(t + 1) * gsz, :] for g in range(n_grp)]
        return jnp.transpose(jnp.concatenate(rows, axis=0))

    for t in range(n_t):
        r_t[t] = by_channel(r_ref, t)
        w_t[t] = jnp.exp(by_channel(lw_ref, t))
        k_t[t] = by_channel(k_ref, t)
        v_t[t] = by_channel(v_ref, t)
        a_t[t] = -by_channel(kk_ref, t)
        b_t[t] = by_channel(b_ref, t)

    for hh in range(2):
        ch = slice(hh * HEAD_SIZE, (hh + 1) * HEAD_SIZE)

        def row_step(i, carry):
            s = s0_ref[hh, i]
            for t in range(n_t):
                sa = jnp.sum(s * a_t[t, ch, :], axis=0, keepdims=True)
                v_i = v_t[t, pl.ds(hh * HEAD_SIZE + i, 1), :]
                s = s * w_t[t, ch, :] + sa * b_t[t, ch, :] + v_i * k_t[t, ch, :]
                y_t[t, pl.ds(hh * HEAD_SIZE + i, 1), :] = jnp.sum(s * r_t[t, ch, :], axis=0, keepdims=True)
            s_out[hh, i] = s
            return carry

        lax.fori_loop(0, HEAD_SIZE, row_step, 0, unroll=4)

    for t in range(n_t):
        y = jnp.transpose(y_t[t])
        for g in range(n_grp):
            y_out[g, t * gsz:(t + 1) * gsz, :] = y[g * gsz:(g + 1) * gsz]


def _post_kernel(x_ref, y_ref, bonus_ref, g_ref, ga_ref, gbo_ref, ffn_state_ref, lnx_g_ref, lnx_b_ref,
                 w_br_ref, w_out_ref, n2g_ref, w_up_ref, conv_ref, w_down_ref, fng_ref,
                 out_ref, ffn_out, c_ffn, *, s):
    @pl.when(pl.program_id(1) == 0)
    def _():
        c_ffn[...] = ffn_state_ref[...]

    rows = x_ref.shape[0]
    z = []
    for c in range(D_MODEL // COL_CHUNK):
        cols = slice(COL_CHUNK * c, COL_CHUNK * (c + 1))
        y = y_ref[:, cols]
        dev = y - _head_sum(y) * (1.0 / HEAD_SIZE)
        var = _head_sum(dev * dev) * (1.0 / HEAD_SIZE)
        yn = dev * lax.rsqrt(var + GN_EPS) * lnx_g_ref[:, cols] + lnx_b_ref[:, cols]
        z.append(((yn + bonus_ref[:, cols]) * g_ref[:, cols]).astype(BF16))
    o_a = jnp.dot(jnp.concatenate(z, axis=1), w_br_ref[...], preferred_element_type=F32)
    x1 = x_ref[...] + _dot(ga_ref[...] * o_a + gbo_ref[...], w_out_ref[...])

    xb = _rms_norm(x1, n2g_ref[...]).astype(BF16)
    n_chunk = D_FF // FF_CHUNK

    def up_proj(j):
        return [jnp.dot(xb, w_up_ref[:, base + FF_CHUNK * j:base + FF_CHUNK * (j + 1)],
                        preferred_element_type=F32) for base in (0, D_FF)]

    hidden = []
    ups = up_proj(0)
    for j in range(n_chunk):
        cur = ups
        if j + 1 < n_chunk:
            ups = up_proj(j + 1)
        halves = []
        for base, up in zip((0, D_FF), cur):
            cols = slice(base + FF_CHUNK * j, base + FF_CHUNK * (j + 1))
            carry = c_ffn[:, cols]
            prev1 = _shift_rows(up, carry[s:], s)
            prev2 = _shift_rows(prev1, carry[:s], s)
            cw = conv_ref[:, cols]
            halves.append(prev2 * cw[0:1] + prev1 * cw[1:2] + up * cw[2:3])
            new = up[rows - 2 * s:]
            c_ffn[:, cols] = new
            ffn_out[:, cols] = new
        gate, val = halves
        hidden.append((gate * _sigmoid(gate) * val).astype(BF16))
    down = jnp.dot(jnp.concatenate(hidden, axis=1), w_down_ref[...], preferred_element_type=F32)
    out_ref[...] = _rms_norm(x1 + down, fng_ref[...])


def _const_spec(arr):
    nd = arr.ndim
    return pl.BlockSpec(arr.shape, lambda b, t: (0,) * nd, pipeline_mode=pl.Buffered(1))


def _row_spec(rows, cols):
    return pl.BlockSpec((None, rows, cols), lambda b, t: (b, t, 0))


def _state_spec(rows, cols):
    return pl.BlockSpec((None, rows, cols), lambda b, t: (b, 0, 0))


def _params():
    return pltpu.CompilerParams(dimension_semantics=("arbitrary", "arbitrary"),
                                vmem_limit_bytes=VMEM_LIMIT)


def _rwkv_prep(xb, sh_rkv, sh_lora, w, *, rows, s):
    nb, total, _ = xb.shape
    grid = (nb, total // rows)
    consts = [w["w_rkv"], w["w_lora"], w["mu_rkv"], w["mu_lora"], w["w0"], w["wd"],
              w["a0"], w["wa"], w["wg"], w["k_k"], w["k_a"], w["r_k"]]
    tok = jax.ShapeDtypeStruct((nb, total, D_MODEL), F32)
    return pl.pallas_call(
        functools.partial(_rwkv_prep_kernel, s=s),
        grid=grid,
        in_specs=[_row_spec(rows, D_MODEL), _state_spec(s, 3 * D_MODEL), _state_spec(s, D_LORA_PAD)]
                 + [_const_spec(c) for c in consts],
        out_specs=[_row_spec(rows, D_MODEL)] * 8
                  + [_state_spec(s, 3 * D_MODEL), _state_spec(s, D_LORA_PAD)],
        out_shape=[tok] * 8 + [jax.ShapeDtypeStruct((nb, s, 3 * D_MODEL), F32),
                               jax.ShapeDtypeStruct((nb, s, D_LORA_PAD), F32)],
        scratch_shapes=[pltpu.VMEM((s, 3 * D_MODEL), F32), pltpu.VMEM((s, D_LORA_PAD), F32)],
        compiler_params=_params(),
        name="rwkv_prep",
    )(xb, sh_rkv, sh_lora, *consts)


def _sc_gate(x, sc_state, w, *, rows, s):
    nb, total, _ = x.shape
    grid = (nb, total // rows)
    consts = [w["norm1_g"], w["w_sc"], w["w_gate"], w["b_gate"], w["conv_sc"], w["w_branch_sc"]]
    tok = jax.ShapeDtypeStruct((nb, total, D_MODEL), F32)
    return pl.pallas_call(
        functools.partial(_sc_gate_kernel, s=s),
        grid=grid,
        in_specs=[_row_spec(rows, D_MODEL), _state_spec(2 * s, D_MODEL)] + [_const_spec(c) for c in consts],
        out_specs=[_row_spec(rows, D_MODEL)] * 3 + [_state_spec(2 * s, D_MODEL)],
        out_shape=[tok] * 2 + [jax.ShapeDtypeStruct((nb, total, D_MODEL), BF16),
                               jax.ShapeDtypeStruct((nb, 2 * s, D_MODEL), F32)],
        scratch_shapes=[pltpu.VMEM((2 * s, D_MODEL), F32)],
        compiler_params=_params(),
        name="sc_gate",
    )(x, sc_state, *consts)


def _wkv(r, lw, k, v, kk, b, s0, *, chunk, nq):
    nb, total, _ = r.shape
    grid = (nb // nq, total // chunk)
    tok_spec = pl.BlockSpec((nq, chunk, D_MODEL), lambda i, c: (i, c, 0))
    st_spec = pl.BlockSpec((nq, N_HEADS, HEAD_SIZE, HEAD_SIZE), lambda i, c: (i, 0, 0, 0))
    return pl.pallas_call(
        functools.partial(_wkv_kernel, C=chunk, nq=nq),
        grid=grid,
        in_specs=[tok_spec] * 6 + [st_spec],
        out_specs=[tok_spec, st_spec],
        out_shape=[jax.ShapeDtypeStruct((nb, total, D_MODEL), F32),
                   jax.ShapeDtypeStruct((nb, N_HEADS, HEAD_SIZE, HEAD_SIZE), F32)],
        scratch_shapes=[pltpu.VMEM((nq * N_PAIRS, LANES, LANES), F32)],
        compiler_params=_params(),
        name="wkv",
    )(r, lw, k, v, kk, b, s0)


def _wkv_lanes(r, lw, k, v, kk, b, s0, *, n_t):
    n_grp, total, _ = r.shape
    batch = s0.shape[-1]
    tok_spec = pl.BlockSpec((n_grp, total, LANES), lambda p: (0, 0, p))
    st_spec = pl.BlockSpec((2, HEAD_SIZE, HEAD_SIZE, batch), lambda p: (p, 0, 0, 0))
    return pl.pallas_call(
        functools.partial(_wkv_lanes_kernel, n_t=n_t, gsz=total // n_t),
        grid=(N_PAIRS,),
        in_specs=[tok_spec] * 6 + [st_spec],
        out_specs=[tok_spec, st_spec],
        out_shape=[jax.ShapeDtypeStruct(r.shape, F32), jax.ShapeDtypeStruct(s0.shape, F32)],
        scratch_shapes=[pltpu.VMEM((n_t, LANES, batch), F32)] * 7,
        compiler_params=pltpu.CompilerParams(dimension_semantics=("arbitrary",),
                                             vmem_limit_bytes=VMEM_LIMIT),
        name="wkv_lanes",
    )(r, lw, k, v, kk, b, s0)


def _post(x, y, bonus, g, ga, gbo, ffn_state, w, *, rows, s):
    nb, total, _ = x.shape
    grid = (nb, total // rows)
    consts = [w["lnx_g"], w["lnx_b"], w["w_branch_rwkv"], w["w_out"], w["norm2_g"], w["w_up"],
              w["conv_ffn"], w["w_down"], w["final_norm_g"]]
    return pl.pallas_call(
        functools.partial(_post_kernel, s=s),
        grid=grid,
        in_specs=[_row_spec(rows, D_MODEL)] * 6 + [_state_spec(2 * s, 2 * D_FF)]
                 + [_const_spec(c) for c in consts],
        out_specs=[_row_spec(rows, D_MODEL), _state_spec(2 * s, 2 * D_FF)],
        out_shape=[jax.ShapeDtypeStruct((nb, total, D_MODEL), F32),
                   jax.ShapeDtypeStruct((nb, 2 * s, 2 * D_FF), F32)],
        scratch_shapes=[pltpu.VMEM((2 * s, 2 * D_FF), F32)],
        compiler_params=_params(),
        name="post",
    )(x, y, bonus, g, ga, gbo, ffn_state, *consts)


def _pad_lora_cols(a):
    pad = lambda t, n: jnp.pad(t, [(0, 0)] * (t.ndim - 1) + [(0, n - t.shape[-1])])
    return jnp.concatenate([pad(a[..., :64], 128), pad(a[..., 64:128], 128), pad(a[..., 128:], 256)], axis=-1)


def _unpad_lora_cols(a):
    return jnp.concatenate([a[..., :64], a[..., 128:192], a[..., 256:256 + D_GATE_LORA]], axis=-1)


def _prep_weights(norm1_g, w_in, b_gate, mu_shift, w0, w_decay_up, a0, w_aaa_up, w_gate_up, k_k, k_a,
                  r_k, lnx_g, lnx_b, w_branch_rwkv, w_branch_sc, conv_sc, w_out, norm2_g, w_up, conv_ffn,
                  w_down, final_norm_g):
    row = lambda t: t.reshape(1, -1).astype(F32)
    d3 = 3 * D_MODEL
    n_lora = D_DECAY_LORA + D_AAA_LORA + D_GATE_LORA
    pad_rows = lambda t, n: jnp.pad(t, [(0, n - t.shape[0]), (0, 0)])
    return {
        "norm1_g": row(norm1_g),
        "w_rkv": w_in[:, :d3].astype(BF16),
        "w_lora": _pad_lora_cols(w_in[:, d3:d3 + n_lora]).astype(BF16),
        "w_sc": w_in[:, d3 + n_lora:2 * d3 + n_lora].astype(BF16),
        "w_gate": w_in[:, 2 * d3 + n_lora:].astype(BF16),
        "b_gate": row(b_gate),
        "mu_rkv": row(mu_shift[:d3]),
        "mu_lora": _pad_lora_cols(row(mu_shift[d3:])),
        "w0": row(w0),
        "wd": pad_rows(w_decay_up, 128).astype(BF16),
        "a0": row(a0),
        "wa": pad_rows(w_aaa_up, 128).astype(BF16),
        "wg": pad_rows(w_gate_up, 256).astype(BF16),
        "k_k": row(k_k), "k_a": row(k_a), "r_k": row(r_k),
        "lnx_g": row(lnx_g), "lnx_b": row(lnx_b),
        "w_branch_rwkv": w_branch_rwkv.astype(BF16),
        "w_branch_sc": w_branch_sc.astype(BF16),
        "conv_sc": conv_sc.astype(F32),
        "w_out": w_out.astype(BF16),
        "norm2_g": row(norm2_g),
        "w_up": w_up.astype(BF16),
        "conv_ffn": conv_ffn.astype(F32),
        "w_down": w_down.astype(BF16),
        "final_norm_g": row(final_norm_g),
    }


def _layer(x, s_wkv, sh_rkv, sh_lora, sc_state, ffn_state, w, *, rows, s, wkv):
    ga, gbo, xb, sc_new = _sc_gate(x, sc_state, w, rows=rows, s=s)
    r, lw, k, v, kk, b, g, bonus, shr, shl = _rwkv_prep(xb, sh_rkv, sh_lora, w, rows=rows, s=s)
    y, s_new = wkv(r, lw, k, v, kk, b, s_wkv)
    out, ffn_new = _post(x, y, bonus, g, ga, gbo, ffn_state, w, rows=rows, s=s)
    return out, s_new, shr, shl, sc_new, ffn_new


def kernel(x_prompt, x_sample, state_wkv, state_shift, state_sc_conv, state_ffn_conv, meta_tokens,
           norm1_g, w_in, b_gate, mu_shift, w0, w_decay_up, a0, w_aaa_up, w_gate_up, k_k, k_a, r_k,
           lnx_g, lnx_b, w_branch_rwkv, w_branch_sc, conv_sc, w_out, norm2_g, w_up, conv_ffn, w_down,
           final_norm_g):
    w = _prep_weights(norm1_g[0], w_in[0], b_gate[0], mu_shift[0], w0[0], w_decay_up[0], a0[0],
                      w_aaa_up[0], w_gate_up[0], k_k[0], k_a[0], r_k[0], lnx_g[0], lnx_b[0],
                      w_branch_rwkv[0], w_branch_sc[0], conv_sc[0], w_out[0], norm2_g[0], w_up[0],
                      conv_ffn[0], w_down[0], final_norm_g)
    d3 = 3 * D_MODEL

    bp, seq, _ = x_prompt.shape
    zeros = lambda *shape: jnp.zeros(shape, F32)
    _, m_wkv, m_shr, m_shl, m_sc, m_ffn = _layer(
        meta_tokens.astype(F32)[None], zeros(1, N_HEADS, HEAD_SIZE, HEAD_SIZE), zeros(1, 1, d3),
        zeros(1, 1, D_LORA_PAD), zeros(1, 2, D_MODEL), zeros(1, 2, 2 * D_FF), w,
        rows=N_META, s=1, wkv=functools.partial(_wkv, chunk=N_META, nq=1))

    rep = lambda t: jnp.broadcast_to(t, (bp,) + t.shape[1:])
    y_prompt, p_wkv, p_shr, p_shl, p_sc, p_ffn = _layer(
        x_prompt, rep(m_wkv), rep(m_shr), rep(m_shl), rep(m_sc), rep(m_ffn), w,
        rows=512, s=1, wkv=functools.partial(_wkv, chunk=64, nq=2))

    bs, ts, _ = x_sample.shape
    n_grp = 2
    gsz = bs // n_grp

    def to_rows(t):
        n, c = t.shape[1:]
        return t.reshape(n_grp, gsz, n, c).transpose(0, 2, 1, 3).reshape(n_grp, n * gsz, c)

    sh = state_shift[0].reshape(n_grp, gsz, -1)
    y_s, s_wkv, s_shr, s_shl, s_sc, s_ffn = _layer(
        to_rows(x_sample), state_wkv[0].transpose(1, 2, 3, 0), sh[..., :d3], _pad_lora_cols(sh[..., d3:]),
        to_rows(state_sc_conv[0]), to_rows(state_ffn_conv[0]), w,
        rows=ts * gsz, s=gsz, wkv=functools.partial(_wkv_lanes, n_t=ts))
    s_wkv = s_wkv.transpose(3, 0, 1, 2)

    def rows_to_batch(t, n):
        c = t.shape[-1]
        return t.reshape(n_grp, n, gsz, c).transpose(0, 2, 1, 3).reshape(bs, n, c)

    y_sample = rows_to_batch(y_s, ts)
    shift_p = jnp.concatenate([p_shr[:, 0], _unpad_lora_cols(p_shl[:, 0])], axis=-1)
    shift_s = jnp.concatenate([rows_to_batch(s_shr, 1)[:, 0], _unpad_lora_cols(rows_to_batch(s_shl, 1)[:, 0])],
                              axis=-1)
    return (y_prompt, y_sample,
            p_wkv[None], s_wkv[None],
            shift_p[None], shift_s[None],
            p_sc[None], rows_to_batch(s_sc, 2)[None],
            p_ffn[None], rows_to_batch(s_ffn, 2)[None])
```

```python
import functools

import jax
import jax.numpy as jnp
from jax import lax
from jax.experimental import pallas as pl
from jax.experimental.pallas import tpu as pltpu

D_MODEL = 1024
N_META = 16
HEAD_SIZE = 64
N_HEADS = D_MODEL // HEAD_SIZE
LANES = 128
N_PAIRS = D_MODEL // LANES
D_DECAY_LORA = 64
D_AAA_LORA = 64
D_GATE_LORA = 160
D_LORA_PAD = 512
D_FF = 2816
FF_CHUNK = 256
COL_CHUNK = 256
CONV_W = 3
NEG_LOG2_E = -1.4426950408889634
EXP_NEG_HALF = 0.6065306597126334
RMS_EPS = 1e-6
GN_EPS = 64e-5
VMEM_LIMIT = 60 * 1024 * 1024

F32 = jnp.float32
BF16 = jnp.bfloat16


def _dot(a, b):
    return jnp.dot(a.astype(BF16), b.astype(BF16), preferred_element_type=F32)


def _dot_nt(a, b):
    return lax.dot_general(a.astype(BF16), b.astype(BF16), (((1,), (1,)), ((), ())),
                           preferred_element_type=F32)


def _dot_tn(a, b):
    return lax.dot_general(a.astype(BF16), b.astype(BF16), (((0,), (0,)), ((), ())),
                           preferred_element_type=F32)


def _head_ones(n):
    r = lax.broadcasted_iota(jnp.int32, (n, n), 0) // HEAD_SIZE
    c = lax.broadcasted_iota(jnp.int32, (n, n), 1) // HEAD_SIZE
    return jnp.where(r == c, 1.0, 0.0).astype(BF16)


def _head_sum(x, ones):
    return jnp.dot(x.astype(BF16), ones, preferred_element_type=F32)


def _rms_norm(x, g):
    return x * lax.rsqrt(jnp.mean(x * x, axis=-1, keepdims=True) + RMS_EPS) * g


def _sigmoid(x):
    return 1.0 / (1.0 + jnp.exp2(x * NEG_LOG2_E))


def _shift_rows(cur, carry, s):
    rows = cur.shape[0]
    if s % 8 == 0:
        return jnp.concatenate([carry, cur[:rows - s]], axis=0)
    assert s == 1
    rolled = pltpu.roll(cur, 1, axis=0)
    row = lax.broadcasted_iota(jnp.int32, cur.shape, 0)
    return jnp.where(row == 0, carry, rolled)


def _rwkv_prep_kernel(xb_ref, sh_rkv_ref, sh_lora_ref, w_rkv_ref, w_lora_ref, mu_rkv_ref,
                      mu_lora_ref, w0_ref, wd_ref, a0_ref, wa_ref, wg_ref, kk_ref, ka_ref, rk_ref,
                      r_out, lw_out, k_out, v_out, kk_out, b_out, g_out, bonus_out, shr_out, shl_out,
                      c_rkv, c_lora, *, s):
    @pl.when(pl.program_id(1) == 0)
    def _():
        c_rkv[...] = sh_rkv_ref[...]
        c_lora[...] = sh_lora_ref[...]

    rows = xb_ref.shape[0]
    xb = xb_ref[...]
    p_lora = jnp.dot(xb, w_lora_ref[...], preferred_element_type=F32)
    prev_lora = _shift_rows(p_lora, c_lora[...], s)
    xl = p_lora + (prev_lora - p_lora) * mu_lora_ref[...]
    new_lora = p_lora[rows - s:]
    c_lora[...] = new_lora
    shl_out[...] = new_lora
    t_xw = jnp.tanh(xl[:, :128]).astype(BF16)
    xa = xl[:, 128:256].astype(BF16)
    s_xg = _sigmoid(xl[:, 256:]).astype(BF16)

    def proj(c):
        return [jnp.dot(xb, w_rkv_ref[:, base + COL_CHUNK * c:base + COL_CHUNK * (c + 1)],
                        preferred_element_type=F32) for base in (0, D_MODEL, 2 * D_MODEL)]

    ones = _head_ones(COL_CHUNK)
    nxt = proj(0)
    for c in range(D_MODEL // COL_CHUNK):
        cur = nxt
        if c + 1 < D_MODEL // COL_CHUNK:
            nxt = proj(c + 1)
        cols = slice(COL_CHUNK * c, COL_CHUNK * (c + 1))
        shifted = []
        for part, p in enumerate(cur):
            pcols = slice(part * D_MODEL + COL_CHUNK * c, part * D_MODEL + COL_CHUNK * (c + 1))
            prev = _shift_rows(p, c_rkv[:, pcols], s)
            shifted.append(p + (prev - p) * mu_rkv_ref[:, pcols])
            new = p[rows - s:]
            c_rkv[:, pcols] = new
            shr_out[:, pcols] = new
        r, k, v = shifted

        zw = w0_ref[:, cols] + jnp.dot(t_xw, wd_ref[:, cols], preferred_element_type=F32)
        lw_out[:, cols] = _sigmoid(zw) * (-EXP_NEG_HALF)
        a = _sigmoid(a0_ref[:, cols] + jnp.dot(xa, wa_ref[:, cols], preferred_element_type=F32))
        g_out[:, cols] = jnp.dot(s_xg, wg_ref[:, cols], preferred_element_type=F32)

        kkr = k * kk_ref[:, cols]
        kk = kkr * lax.rsqrt(jnp.maximum(_head_sum(kkr * kkr, ones), 1e-24))
        k2 = k * (1.0 + (a - 1.0) * ka_ref[:, cols])
        r_out[:, cols] = r
        k_out[:, cols] = k2
        v_out[:, cols] = v
        kk_out[:, cols] = kk
        b_out[:, cols] = kk * a
        bonus_out[:, cols] = _head_sum(r * k2 * rk_ref[:, cols], ones) * v


def _sc_gate_kernel(x_ref, sc_state_ref, n1g_ref, w_sc_ref, w_gate_ref, b_gate_ref, conv_ref,
                    w_bsc_ref, ga_out, gbo_out, xb_out, sc_out, c_sc, *, s):
    @pl.when(pl.program_id(1) == 0)
    def _():
        c_sc[...] = sc_state_ref[...]

    rows = x_ref.shape[0]
    xb = _rms_norm(x_ref[...], n1g_ref[...]).astype(BF16)
    xb_out[...] = xb

    def proj(c):
        cs = lambda base: slice(base + COL_CHUNK * c, base + COL_CHUNK * (c + 1))
        sc = [jnp.dot(xb, w_sc_ref[:, cs(base)], preferred_element_type=F32)
              for base in (0, D_MODEL, 2 * D_MODEL)]
        gate = [jnp.dot(xb, w_gate_ref[:, cs(base)], preferred_element_type=F32) + b_gate_ref[:, cs(base)]
                for base in (0, D_MODEL)]
        return sc + gate

    mid = []
    gb = []
    nxt = proj(0)
    for c in range(D_MODEL // COL_CHUNK):
        cur = nxt
        if c + 1 < D_MODEL // COL_CHUNK:
            nxt = proj(c + 1)
        cols = slice(COL_CHUNK * c, COL_CHUNK * (c + 1))
        h, b_g, c_g, pg_a, pg_b = cur
        u = c_g * h
        carry = c_sc[:, cols]
        prev1 = _shift_rows(u, carry[s:], s)
        prev2 = _shift_rows(prev1, carry[:s], s)
        cw = conv_ref[:, cols]
        conv = prev2 * cw[0:1] + prev1 * cw[1:2] + u * cw[2:3]
        mid.append((b_g * conv).astype(BF16))
        new = u[rows - 2 * s:]
        c_sc[:, cols] = new
        sc_out[:, cols] = new
        ga_out[:, cols] = _sigmoid(pg_a)
        gb.append(_sigmoid(pg_b))
    o_b = jnp.dot(jnp.concatenate(mid, axis=1), w_bsc_ref[...], preferred_element_type=F32)
    for c in range(D_MODEL // COL_CHUNK):
        cols = slice(COL_CHUNK * c, COL_CHUNK * (c + 1))
        gbo_out[:, cols] = gb[c] * o_b[:, cols]


def _wkv_kernel(r_ref, lw_ref, k_ref, v_ref, kk_ref, b_ref, s0_ref, y_out, s_out, s_scr, *, C, nq):
    c_idx = pl.program_id(1)

    @pl.when(c_idx == 0)
    def _():
        zh = jnp.zeros((HEAD_SIZE, HEAD_SIZE), F32)
        for q in range(nq):
            for p in range(N_PAIRS):
                s_scr[q * N_PAIRS + p] = jnp.concatenate(
                    [jnp.concatenate([s0_ref[q, 2 * p], zh], axis=1),
                     jnp.concatenate([zh, s0_ref[q, 2 * p + 1]], axis=1)], axis=0)

    ti = lax.broadcasted_iota(jnp.int32, (C, C), 0)
    tj = lax.broadcasted_iota(jnp.int32, (C, C), 1)
    tri = jnp.where(tj <= ti, 1.0, 0.0).astype(BF16)
    m0 = lax.broadcasted_iota(jnp.int32, (C, LANES), 1) < HEAD_SIZE
    gi = lax.broadcasted_iota(jnp.int32, (C, 2 * C), 0)
    gj = lax.broadcasted_iota(jnp.int32, (C, 2 * C), 1)
    gjm = jnp.where(gj >= C, gj - C, gj)
    strict = gjm < gi
    incl = gjm <= gi
    left = gj < C
    strict_l = gj < gi
    strict_r = jnp.logical_and(gj >= C, gj - C < gi)
    ei = lax.broadcasted_iota(jnp.int32, (2 * C, 2 * C), 0)
    ej = lax.broadcasted_iota(jnp.int32, (2 * C, 2 * C), 1)
    eye = jnp.where(ei == ej, 1.0, 0.0)
    bi = lax.broadcasted_iota(jnp.int32, (LANES, LANES), 0) // HEAD_SIZE
    bj = lax.broadcasted_iota(jnp.int32, (LANES, LANES), 1) // HEAD_SIZE
    bd = bi == bj

    def swap_halves(x):
        if 2 * C == LANES:
            return pltpu.roll(x, C, axis=1)
        return jnp.concatenate([x[:, C:], x[:, :C]], axis=1)

    a_p, r_p, v_p, bk, w_last = [], [], [], [], []
    for q in range(nq):
        lw = lw_ref[q]
        h1 = lw.astype(BF16)
        h2 = (lw - h1.astype(F32)).astype(BF16)
        cum = jnp.dot(tri, h1, preferred_element_type=F32) + jnp.dot(tri, h2, preferred_element_type=F32)
        w_t = jnp.exp(cum)
        w_inv = jnp.exp(-cum)
        a_t = -kk_ref[q] * jnp.exp(cum - lw)
        b_t = b_ref[q] * w_inv
        k_t = k_ref[q] * w_inv
        r_t = r_ref[q] * w_t
        v_q = v_ref[q]
        for p in range(N_PAIRS):
            sl = slice(LANES * p, LANES * (p + 1))
            a_p.append(a_t[:, sl])
            r_p.append(r_t[:, sl])
            v_p.append(v_q[:, sl])
            bk.append(jnp.concatenate([b_t[:, sl], k_t[:, sl]], axis=0))
            w_last.append(w_t[C - 1:C, sl])
    ents = range(nq * N_PAIRS)

    g = [_dot_nt(jnp.concatenate([jnp.where(m0, a_p[e], 0.0), jnp.where(m0, r_p[e], 0.0),
                                  jnp.where(m0, 0.0, a_p[e]), jnp.where(m0, 0.0, r_p[e])], axis=0), bk[e])
         for e in ents]
    s_bd = [s_scr[e] for e in ents]
    asrs = [_dot_nt(jnp.concatenate([a_p[e], r_p[e]], axis=0), s_bd[e]) for e in ents]
    g1a = [swap_halves(g[e][2 * C:3 * C]) for e in ents]
    xpow = [jnp.concatenate([jnp.where(strict_l, g[e][:C], 0.0), jnp.where(strict_r, g1a[e], 0.0)], axis=0)
            for e in ents]
    tinv = [eye + xpow[e] for e in ents]
    n = 1
    if 2 * n < C:
        xpow = [_dot(xpow[e], xpow[e]) for e in ents]
        n *= 2
    while 2 * n < C:
        z = [_dot(jnp.concatenate([xpow[e], tinv[e]], axis=0), xpow[e]) for e in ents]
        xpow = [z[e][:2 * C] for e in ents]
        tinv = [tinv[e] + z[e][2 * C:] for e in ents]
        n *= 2
    if C > 2:
        tinv = [tinv[e] + _dot(tinv[e], xpow[e]) for e in ents]
    vm0 = [jnp.where(m0, v_p[e], 0.0) for e in ents]
    vm1 = [jnp.where(m0, 0.0, v_p[e]) for e in ents]
    mak = [jnp.where(strict, jnp.where(left, g1a[e], g[e][:C]), 0.0) for e in ents]
    rhs = [asrs[e][:C] + _dot(mak[e], jnp.concatenate([vm1[e], vm0[e]], axis=0)) for e in ents]
    uu = [_dot(tinv[e], jnp.concatenate([jnp.where(m0, rhs[e], 0.0), jnp.where(m0, 0.0, rhs[e])], axis=0))
          for e in ents]
    u = [uu[e][:C] + uu[e][C:] for e in ents]
    for e in ents:
        q, p = divmod(e, N_PAIRS)
        um0 = jnp.where(m0, u[e], 0.0)
        um1 = jnp.where(m0, 0.0, u[e])
        g0r = jnp.where(incl, g[e][C:2 * C], 0.0)
        g1r = jnp.where(incl, g[e][3 * C:], 0.0)
        if 2 * C == LANES:
            intra = _dot(jnp.concatenate([g0r, g1r], axis=1),
                         jnp.concatenate([um0, vm0[e], um1, vm1[e]], axis=0))
        else:
            intra = (_dot(g0r, jnp.concatenate([um0, vm0[e]], axis=0))
                     + _dot(g1r, jnp.concatenate([um1, vm1[e]], axis=0)))
        y_out[q, :, LANES * p:LANES * (p + 1)] = asrs[e][C:] + intra
    for e in ents:
        upd = _dot_tn(jnp.concatenate([u[e], v_p[e]], axis=0), bk[e])
        s_scr[e] = (s_bd[e] + jnp.where(bd, upd, 0.0)) * w_last[e]

    @pl.when(c_idx == pl.num_programs(1) - 1)
    def _():
        for q in range(nq):
            for p in range(N_PAIRS):
                s_fin = s_scr[q * N_PAIRS + p]
                s_out[q, 2 * p] = s_fin[:HEAD_SIZE, :HEAD_SIZE]
                s_out[q, 2 * p + 1] = s_fin[HEAD_SIZE:, HEAD_SIZE:]


def _wkv_lanes_kernel(r_ref, lw_ref, k_ref, v_ref, kk_ref, b_ref, s0_ref, y_out, s_out,
                      r_t, w_t, k_t, v_t, a_t, b_t, y_t, *, n_t, gsz):
    n_grp = r_ref.shape[0]

    def by_channel(ref, t):
        rows = [ref[g, t * gsz:(t + 1) * gsz, :] for g in range(n_grp)]
        return jnp.transpose(jnp.concatenate(rows, axis=0))

    for t in range(n_t):
        r_t[t] = by_channel(r_ref, t)
        w_t[t] = jnp.exp(by_channel(lw_ref, t))
        k_t[t] = by_channel(k_ref, t)
        v_t[t] = by_channel(v_ref, t)
        a_t[t] = -by_channel(kk_ref, t)
        b_t[t] = by_channel(b_ref, t)

    for hh in range(2):
        ch = slice(hh * HEAD_SIZE, (hh + 1) * HEAD_SIZE)

        def row_step(i, carry):
            s = s0_ref[hh, i]
            for t in range(n_t):
                sa = jnp.sum(s * a_t[t, ch, :], axis=0, keepdims=True)
                v_i = v_t[t, pl.ds(hh * HEAD_SIZE + i, 1), :]
                s = s * w_t[t, ch, :] + sa * b_t[t, ch, :] + v_i * k_t[t, ch, :]
                y_t[t, pl.ds(hh * HEAD_SIZE + i, 1), :] = jnp.sum(s * r_t[t, ch, :], axis=0, keepdims=True)
            s_out[hh, i] = s
            return carry

        lax.fori_loop(0, HEAD_SIZE, row_step, 0, unroll=8)

    for t in range(n_t):
        y = jnp.transpose(y_t[t])
        for g in range(n_grp):
            y_out[g, t * gsz:(t + 1) * gsz, :] = y[g * gsz:(g + 1) * gsz]


def _post_kernel(x_ref, y_ref, bonus_ref, g_ref, ga_ref, gbo_ref, ffn_state_ref, lnx_g_ref, lnx_b_ref,
                 w_br_ref, w_out_ref, n2g_ref, w_up_ref, conv_ref, w_down_ref, fng_ref,
                 out_ref, ffn_out, c_ffn, *, s):
    @pl.when(pl.program_id(1) == 0)
    def _():
        c_ffn[...] = ffn_state_ref[...]

    rows = x_ref.shape[0]
    ones = _head_ones(COL_CHUNK)
    z = []
    for c in range(D_MODEL // COL_CHUNK):
        cols = slice(COL_CHUNK * c, COL_CHUNK * (c + 1))
        y = y_ref[:, cols]
        dev = y - _head_sum(y, ones) * (1.0 / HEAD_SIZE)
        var = _head_sum(dev * dev, ones) * (1.0 / HEAD_SIZE)
        yn = dev * lax.rsqrt(var + GN_EPS) * lnx_g_ref[:, cols] + lnx_b_ref[:, cols]
        z.append(((yn + bonus_ref[:, cols]) * g_ref[:, cols]).astype(BF16))
    o_a = jnp.dot(jnp.concatenate(z, axis=1), w_br_ref[...], preferred_element_type=F32)
    x1 = x_ref[...] + _dot(ga_ref[...] * o_a + gbo_ref[...], w_out_ref[...])

    xb = _rms_norm(x1, n2g_ref[...]).astype(BF16)
    n_chunk = D_FF // FF_CHUNK

    def up_proj(j):
        return [jnp.dot(xb, w_up_ref[:, base + FF_CHUNK * j:base + FF_CHUNK * (j + 1)],
                        preferred_element_type=F32) for base in (0, D_FF)]

    hidden = []
    ups = up_proj(0)
    for j in range(n_chunk):
        cur = ups
        if j + 1 < n_chunk:
            ups = up_proj(j + 1)
        halves = []
        for base, up in zip((0, D_FF), cur):
            cols = slice(base + FF_CHUNK * j, base + FF_CHUNK * (j + 1))
            carry = c_ffn[:, cols]
            prev1 = _shift_rows(up, carry[s:], s)
            prev2 = _shift_rows(prev1, carry[:s], s)
            cw = conv_ref[:, cols]
            halves.append(prev2 * cw[0:1] + prev1 * cw[1:2] + up * cw[2:3])
            new = up[rows - 2 * s:]
            c_ffn[:, cols] = new
            ffn_out[:, cols] = new
        gate, val = halves
        hidden.append((gate * _sigmoid(gate) * val).astype(BF16))
    down = jnp.dot(jnp.concatenate(hidden, axis=1), w_down_ref[...], preferred_element_type=F32)
    out_ref[...] = _rms_norm(x1 + down, fng_ref[...])


def _const_spec(arr):
    nd = arr.ndim
    return pl.BlockSpec(arr.shape, lambda b, t: (0,) * nd, pipeline_mode=pl.Buffered(1))


def _row_spec(rows, cols):
    return pl.BlockSpec((None, rows, cols), lambda b, t: (b, t, 0))


def _state_spec(rows, cols):
    return pl.BlockSpec((None, rows, cols), lambda b, t: (b, 0, 0))


def _params():
    return pltpu.CompilerParams(dimension_semantics=("arbitrary", "arbitrary"),
                                vmem_limit_bytes=VMEM_LIMIT)


def _rwkv_prep(xb, sh_rkv, sh_lora, w, *, rows, s):
    nb, total, _ = xb.shape
    grid = (nb, total // rows)
    consts = [w["w_rkv"], w["w_lora"], w["mu_rkv"], w["mu_lora"], w["w0"], w["wd"],
              w["a0"], w["wa"], w["wg"], w["k_k"], w["k_a"], w["r_k"]]
    tok = jax.ShapeDtypeStruct((nb, total, D_MODEL), F32)
    return pl.pallas_call(
        functools.partial(_rwkv_prep_kernel, s=s),
        grid=grid,
        in_specs=[_row_spec(rows, D_MODEL), _state_spec(s, 3 * D_MODEL), _state_spec(s, D_LORA_PAD)]
                 + [_const_spec(c) for c in consts],
        out_specs=[_row_spec(rows, D_MODEL)] * 8
                  + [_state_spec(s, 3 * D_MODEL), _state_spec(s, D_LORA_PAD)],
        out_shape=[tok] * 8 + [jax.ShapeDtypeStruct((nb, s, 3 * D_MODEL), F32),
                               jax.ShapeDtypeStruct((nb, s, D_LORA_PAD), F32)],
        scratch_shapes=[pltpu.VMEM((s, 3 * D_MODEL), F32), pltpu.VMEM((s, D_LORA_PAD), F32)],
        compiler_params=_params(),
        name="rwkv_prep",
    )(xb, sh_rkv, sh_lora, *consts)


def _sc_gate(x, sc_state, w, *, rows, s):
    nb, total, _ = x.shape
    grid = (nb, total // rows)
    consts = [w["norm1_g"], w["w_sc"], w["w_gate"], w["b_gate"], w["conv_sc"], w["w_branch_sc"]]
    tok = jax.ShapeDtypeStruct((nb, total, D_MODEL), F32)
    return pl.pallas_call(
        functools.partial(_sc_gate_kernel, s=s),
        grid=grid,
        in_specs=[_row_spec(rows, D_MODEL), _state_spec(2 * s, D_MODEL)] + [_const_spec(c) for c in consts],
        out_specs=[_row_spec(rows, D_MODEL)] * 3 + [_state_spec(2 * s, D_MODEL)],
        out_shape=[tok] * 2 + [jax.ShapeDtypeStruct((nb, total, D_MODEL), BF16),
                               jax.ShapeDtypeStruct((nb, 2 * s, D_MODEL), F32)],
        scratch_shapes=[pltpu.VMEM((2 * s, D_MODEL), F32)],
        compiler_params=_params(),
        name="sc_gate",
    )(x, sc_state, *consts)


def _wkv(r, lw, k, v, kk, b, s0, *, chunk, nq):
    nb, total, _ = r.shape
    grid = (nb // nq, total // chunk)
    tok_spec = pl.BlockSpec((nq, chunk, D_MODEL), lambda i, c: (i, c, 0))
    st_spec = pl.BlockSpec((nq, N_HEADS, HEAD_SIZE, HEAD_SIZE), lambda i, c: (i, 0, 0, 0))
    return pl.pallas_call(
        functools.partial(_wkv_kernel, C=chunk, nq=nq),
        grid=grid,
        in_specs=[tok_spec] * 6 + [st_spec],
        out_specs=[tok_spec, st_spec],
        out_shape=[jax.ShapeDtypeStruct((nb, total, D_MODEL), F32),
                   jax.ShapeDtypeStruct((nb, N_HEADS, HEAD_SIZE, HEAD_SIZE), F32)],
        scratch_shapes=[pltpu.VMEM((nq * N_PAIRS, LANES, LANES), F32)],
        compiler_params=_params(),
        name="wkv",
    )(r, lw, k, v, kk, b, s0)


def _wkv_lanes(r, lw, k, v, kk, b, s0, *, n_t):
    n_grp, total, _ = r.shape
    batch = s0.shape[-1]
    tok_spec = pl.BlockSpec((n_grp, total, LANES), lambda p: (0, 0, p))
    st_spec = pl.BlockSpec((2, HEAD_SIZE, HEAD_SIZE, batch), lambda p: (p, 0, 0, 0))
    return pl.pallas_call(
        functools.partial(_wkv_lanes_kernel, n_t=n_t, gsz=total // n_t),
        grid=(N_PAIRS,),
        in_specs=[tok_spec] * 6 + [st_spec],
        out_specs=[tok_spec, st_spec],
        out_shape=[jax.ShapeDtypeStruct(r.shape, F32), jax.ShapeDtypeStruct(s0.shape, F32)],
        scratch_shapes=[pltpu.VMEM((n_t, LANES, batch), F32)] * 7,
        compiler_params=pltpu.CompilerParams(dimension_semantics=("arbitrary",),
                                             vmem_limit_bytes=VMEM_LIMIT),
        name="wkv_lanes",
    )(r, lw, k, v, kk, b, s0)


def _post(x, y, bonus, g, ga, gbo, ffn_state, w, *, rows, s):
    nb, total, _ = x.shape
    grid = (nb, total // rows)
    consts = [w["lnx_g"], w["lnx_b"], w["w_branch_rwkv"], w["w_out"], w["norm2_g"], w["w_up"],
              w["conv_ffn"], w["w_down"], w["final_norm_g"]]
    return pl.pallas_call(
        functools.partial(_post_kernel, s=s),
        grid=grid,
        in_specs=[_row_spec(rows, D_MODEL)] * 6 + [_state_spec(2 * s, 2 * D_FF)]
                 + [_const_spec(c) for c in consts],
        out_specs=[_row_spec(rows, D_MODEL), _state_spec(2 * s, 2 * D_FF)],
        out_shape=[jax.ShapeDtypeStruct((nb, total, D_MODEL), F32),
                   jax.ShapeDtypeStruct((nb, 2 * s, 2 * D_FF), F32)],
        scratch_shapes=[pltpu.VMEM((2 * s, 2 * D_FF), F32)],
        compiler_params=_params(),
        name="post",
    )(x, y, bonus, g, ga, gbo, ffn_state, *consts)


def _pad_lora_cols(a):
    pad = lambda t, n: jnp.pad(t, [(0, 0)] * (t.ndim - 1) + [(0, n - t.shape[-1])])
    return jnp.concatenate([pad(a[..., :64], 128), pad(a[..., 64:128], 128), pad(a[..., 128:], 256)], axis=-1)


def _unpad_lora_cols(a):
    return jnp.concatenate([a[..., :64], a[..., 128:192], a[..., 256:256 + D_GATE_LORA]], axis=-1)


def _prep_weights(norm1_g, w_in, b_gate, mu_shift, w0, w_decay_up, a0, w_aaa_up, w_gate_up, k_k, k_a,
                  r_k, lnx_g, lnx_b, w_branch_rwkv, w_branch_sc, conv_sc, w_out, norm2_g, w_up, conv_ffn,
                  w_down, final_norm_g):
    row = lambda t: t.reshape(1, -1).astype(F32)
    d3 = 3 * D_MODEL
    n_lora = D_DECAY_LORA + D_AAA_LORA + D_GATE_LORA
    pad_rows = lambda t, n: jnp.pad(t, [(0, n - t.shape[0]), (0, 0)])
    return {
        "norm1_g": row(norm1_g),
        "w_rkv": w_in[:, :d3].astype(BF16),
        "w_lora": _pad_lora_cols(w_in[:, d3:d3 + n_lora]).astype(BF16),
        "w_sc": w_in[:, d3 + n_lora:2 * d3 + n_lora].astype(BF16),
        "w_gate": w_in[:, 2 * d3 + n_lora:].astype(BF16),
        "b_gate": row(b_gate),
        "mu_rkv": row(mu_shift[:d3]),
        "mu_lora": _pad_lora_cols(row(mu_shift[d3:])),
        "w0": row(w0),
        "wd": pad_rows(w_decay_up, 128).astype(BF16),
        "a0": row(a0),
        "wa": pad_rows(w_aaa_up, 128).astype(BF16),
        "wg": pad_rows(w_gate_up, 256).astype(BF16),
        "k_k": row(k_k), "k_a": row(k_a), "r_k": row(r_k),
        "lnx_g": row(lnx_g), "lnx_b": row(lnx_b),
        "w_branch_rwkv": w_branch_rwkv.astype(BF16),
        "w_branch_sc": w_branch_sc.astype(BF16),
        "conv_sc": conv_sc.astype(F32),
        "w_out": w_out.astype(BF16),
        "norm2_g": row(norm2_g),
        "w_up": w_up.astype(BF16),
        "conv_ffn": conv_ffn.astype(F32),
        "w_down": w_down.astype(BF16),
        "final_norm_g": row(final_norm_g),
    }


def _layer(x, s_wkv, sh_rkv, sh_lora, sc_state, ffn_state, w, *, rows, s, wkv):
    ga, gbo, xb, sc_new = _sc_gate(x, sc_state, w, rows=rows, s=s)
    r, lw, k, v, kk, b, g, bonus, shr, shl = _rwkv_prep(xb, sh_rkv, sh_lora, w, rows=rows, s=s)
    y, s_new = wkv(r, lw, k, v, kk, b, s_wkv)
    out, ffn_new = _post(x, y, bonus, g, ga, gbo, ffn_state, w, rows=rows, s=s)
    return out, s_new, shr, shl, sc_new, ffn_new


def kernel(x_prompt, x_sample, state_wkv, state_shift, state_sc_conv, state_ffn_conv, meta_tokens,
           norm1_g, w_in, b_gate, mu_shift, w0, w_decay_up, a0, w_aaa_up, w_gate_up, k_k, k_a, r_k,
           lnx_g, lnx_b, w_branch_rwkv, w_branch_sc, conv_sc, w_out, norm2_g, w_up, conv_ffn, w_down,
           final_norm_g):
    w = _prep_weights(norm1_g[0], w_in[0], b_gate[0], mu_shift[0], w0[0], w_decay_up[0], a0[0],
                      w_aaa_up[0], w_gate_up[0], k_k[0], k_a[0], r_k[0], lnx_g[0], lnx_b[0],
                      w_branch_rwkv[0], w_branch_sc[0], conv_sc[0], w_out[0], norm2_g[0], w_up[0],
                      conv_ffn[0], w_down[0], final_norm_g)
    d3 = 3 * D_MODEL

    bp, seq, _ = x_prompt.shape
    zeros = lambda *shape: jnp.zeros(shape, F32)
    _, m_wkv, m_shr, m_shl, m_sc, m_ffn = _layer(
        meta_tokens.astype(F32)[None], zeros(1, N_HEADS, HEAD_SIZE, HEAD_SIZE), zeros(1, 1, d3),
        zeros(1, 1, D_LORA_PAD), zeros(1, 2, D_MODEL), zeros(1, 2, 2 * D_FF), w,
        rows=N_META, s=1, wkv=functools.partial(_wkv, chunk=N_META, nq=1))

    rep = lambda t: jnp.broadcast_to(t, (bp,) + t.shape[1:])
    y_prompt, p_wkv, p_shr, p_shl, p_sc, p_ffn = _layer(
        x_prompt, rep(m_wkv), rep(m_shr), rep(m_shl), rep(m_sc), rep(m_ffn), w,
        rows=512, s=1, wkv=functools.partial(_wkv, chunk=64, nq=4))

    bs, ts, _ = x_sample.shape
    n_grp = 2
    gsz = bs // n_grp

    def to_rows(t):
        n, c = t.shape[1:]
        return t.reshape(n_grp, gsz, n, c).transpose(0, 2, 1, 3).reshape(n_grp, n * gsz, c)

    sh = state_shift[0].reshape(n_grp, gsz, -1)
    y_s, s_wkv, s_shr, s_shl, s_sc, s_ffn = _layer(
        to_rows(x_sample), state_wkv[0].transpose(1, 2, 3, 0), sh[..., :d3], _pad_lora_cols(sh[..., d3:]),
        to_rows(state_sc_conv[0]), to_rows(state_ffn_conv[0]), w,
        rows=ts * gsz, s=gsz, wkv=functools.partial(_wkv_lanes, n_t=ts))
    s_wkv = s_wkv.transpose(3, 0, 1, 2)

    def rows_to_batch(t, n):
        c = t.shape[-1]
        return t.reshape(n_grp, n, gsz, c).transpose(0, 2, 1, 3).reshape(bs, n, c)

    y_sample = rows_to_batch(y_s, ts)
    shift_p = jnp.concatenate([p_shr[:, 0], _unpad_lora_cols(p_shl[:, 0])], axis=-1)
    shift_s = jnp.concatenate([rows_to_batch(s_shr, 1)[:, 0], _unpad_lora_cols(rows_to_batch(s_shl, 1)[:, 0])],
                              axis=-1)
    return (y_prompt, y_sample,
            p_wkv[None], s_wkv[None],
            shift_p[None], shift_s[None],
            p_sc[None], rows_to_batch(s_sc, 2)[None],
            p_ffn[None], rows_to_batch(s_ffn, 2)[None])
```

```python
import functools

import jax
import jax.numpy as jnp
from jax import lax
from jax.experimental import pallas as pl
from jax.experimental.pallas import tpu as pltpu

D_MODEL = 1024
N_META = 16
HEAD_SIZE = 64
N_HEADS = D_MODEL // HEAD_SIZE
LANES = 128
N_PAIRS = D_MODEL // LANES
D_DECAY_LORA = 64
D_AAA_LORA = 64
D_GATE_LORA = 160
D_LORA_PAD = 512
D_FF = 2816
FF_CHUNK = 256
COL_CHUNK = 256
CONV_W = 3
NEG_LOG2_E = -1.4426950408889634
EXP_NEG_HALF = 0.6065306597126334
RMS_EPS = 1e-6
GN_EPS = 64e-5
VMEM_LIMIT = 60 * 1024 * 1024

F32 = jnp.float32
BF16 = jnp.bfloat16


def _dot(a, b):
    return jnp.dot(a.astype(BF16), b.astype(BF16), preferred_element_type=F32)


def _dot_nt(a, b):
    return lax.dot_general(a.astype(BF16), b.astype(BF16), (((1,), (1,)), ((), ())),
                           preferred_element_type=F32)


def _dot_tn(a, b):
    return lax.dot_general(a.astype(BF16), b.astype(BF16), (((0,), (0,)), ((), ())),
                           preferred_element_type=F32)


def _head_ones(n):
    r = lax.broadcasted_iota(jnp.int32, (n, n), 0) // HEAD_SIZE
    c = lax.broadcasted_iota(jnp.int32, (n, n), 1) // HEAD_SIZE
    return jnp.where(r == c, 1.0, 0.0).astype(BF16)


def _head_sum(x, ones):
    return jnp.dot(x.astype(BF16), ones, preferred_element_type=F32)


def _rms_norm(x, g):
    return x * lax.rsqrt(jnp.mean(x * x, axis=-1, keepdims=True) + RMS_EPS) * g


def _sigmoid(x):
    return 1.0 / (1.0 + jnp.exp2(x * NEG_LOG2_E))


def _shift_rows(cur, carry, s):
    rows = cur.shape[0]
    if s % 8 == 0:
        return jnp.concatenate([carry, cur[:rows - s]], axis=0)
    assert s == 1
    rolled = pltpu.roll(cur, 1, axis=0)
    row = lax.broadcasted_iota(jnp.int32, cur.shape, 0)
    return jnp.where(row == 0, carry, rolled)


def _rwkv_prep_kernel(xb_ref, sh_rkv_ref, sh_lora_ref, w_rkv_ref, w_lora_ref, mu_rkv_ref,
                      mu_lora_ref, w0_ref, wd_ref, a0_ref, wa_ref, wg_ref, kk_ref, ka_ref, rk_ref,
                      r_out, lw_out, k_out, v_out, kk_out, b_out, g_out, bonus_out, shr_out, shl_out,
                      c_rkv, c_lora, *, s):
    @pl.when(pl.program_id(1) == 0)
    def _():
        c_rkv[...] = sh_rkv_ref[...]
        c_lora[...] = sh_lora_ref[...]

    rows = xb_ref.shape[0]
    xb = xb_ref[...]
    p_lora = jnp.dot(xb, w_lora_ref[...], preferred_element_type=F32)
    prev_lora = _shift_rows(p_lora, c_lora[...], s)
    xl = p_lora + (prev_lora - p_lora) * mu_lora_ref[...]
    new_lora = p_lora[rows - s:]
    c_lora[...] = new_lora
    shl_out[...] = new_lora
    t_xw = jnp.tanh(xl[:, :128]).astype(BF16)
    xa = xl[:, 128:256].astype(BF16)
    s_xg = _sigmoid(xl[:, 256:]).astype(BF16)

    def proj(c):
        return [jnp.dot(xb, w_rkv_ref[:, base + COL_CHUNK * c:base + COL_CHUNK * (c + 1)],
                        preferred_element_type=F32) for base in (0, D_MODEL, 2 * D_MODEL)]

    ones = _head_ones(COL_CHUNK)
    nxt = proj(0)
    for c in range(D_MODEL // COL_CHUNK):
        cur = nxt
        if c + 1 < D_MODEL // COL_CHUNK:
            nxt = proj(c + 1)
        cols = slice(COL_CHUNK * c, COL_CHUNK * (c + 1))
        shifted = []
        for part, p in enumerate(cur):
            pcols = slice(part * D_MODEL + COL_CHUNK * c, part * D_MODEL + COL_CHUNK * (c + 1))
            prev = _shift_rows(p, c_rkv[:, pcols], s)
            shifted.append(p + (prev - p) * mu_rkv_ref[:, pcols])
            new = p[rows - s:]
            c_rkv[:, pcols] = new
            shr_out[:, pcols] = new
        r, k, v = shifted

        zw = w0_ref[:, cols] + jnp.dot(t_xw, wd_ref[:, cols], preferred_element_type=F32)
        lw_out[:, cols] = _sigmoid(zw) * (-EXP_NEG_HALF)
        a = _sigmoid(a0_ref[:, cols] + jnp.dot(xa, wa_ref[:, cols], preferred_element_type=F32))
        g_out[:, cols] = jnp.dot(s_xg, wg_ref[:, cols], preferred_element_type=F32)

        kkr = k * kk_ref[:, cols]
        kk = kkr * lax.rsqrt(jnp.maximum(_head_sum(kkr * kkr, ones), 1e-24))
        k2 = k * (1.0 + (a - 1.0) * ka_ref[:, cols])
        r_out[:, cols] = r
        k_out[:, cols] = k2
        v_out[:, cols] = v
        kk_out[:, cols] = kk
        b_out[:, cols] = kk * a
        bonus_out[:, cols] = _head_sum(r * k2 * rk_ref[:, cols], ones) * v


def _sc_gate_kernel(x_ref, sc_state_ref, n1g_ref, w_sc_ref, w_gate_ref, b_gate_ref, conv_ref,
                    w_bsc_ref, ga_out, gbo_out, xb_out, sc_out, c_sc, *, s):
    @pl.when(pl.program_id(1) == 0)
    def _():
        c_sc[...] = sc_state_ref[...]

    rows = x_ref.shape[0]
    xb = _rms_norm(x_ref[...], n1g_ref[...]).astype(BF16)
    xb_out[...] = xb

    def proj(c):
        cs = lambda base: slice(base + COL_CHUNK * c, base + COL_CHUNK * (c + 1))
        sc = [jnp.dot(xb, w_sc_ref[:, cs(base)], preferred_element_type=F32)
              for base in (0, D_MODEL, 2 * D_MODEL)]
        gate = [jnp.dot(xb, w_gate_ref[:, cs(base)], preferred_element_type=F32) + b_gate_ref[:, cs(base)]
                for base in (0, D_MODEL)]
        return sc + gate

    mid = []
    gb = []
    nxt = proj(0)
    for c in range(D_MODEL // COL_CHUNK):
        cur = nxt
        if c + 1 < D_MODEL // COL_CHUNK:
            nxt = proj(c + 1)
        cols = slice(COL_CHUNK * c, COL_CHUNK * (c + 1))
        h, b_g, c_g, pg_a, pg_b = cur
        u = c_g * h
        carry = c_sc[:, cols]
        prev1 = _shift_rows(u, carry[s:], s)
        prev2 = _shift_rows(prev1, carry[:s], s)
        cw = conv_ref[:, cols]
        conv = prev2 * cw[0:1] + prev1 * cw[1:2] + u * cw[2:3]
        mid.append((b_g * conv).astype(BF16))
        new = u[rows - 2 * s:]
        c_sc[:, cols] = new
        sc_out[:, cols] = new
        ga_out[:, cols] = _sigmoid(pg_a)
        gb.append(_sigmoid(pg_b))
    o_b = jnp.dot(jnp.concatenate(mid, axis=1), w_bsc_ref[...], preferred_element_type=F32)
    for c in range(D_MODEL // COL_CHUNK):
        cols = slice(COL_CHUNK * c, COL_CHUNK * (c + 1))
        gbo_out[:, cols] = gb[c] * o_b[:, cols]


def _wkv_kernel(r_ref, lw_ref, k_ref, v_ref, kk_ref, b_ref, s0_ref, y_out, s_out, s_scr, *, C, nq):
    c_idx = pl.program_id(1)

    @pl.when(c_idx == 0)
    def _():
        zh = jnp.zeros((HEAD_SIZE, HEAD_SIZE), F32)
        for q in range(nq):
            for p in range(N_PAIRS):
                s_scr[q * N_PAIRS + p] = jnp.concatenate(
                    [jnp.concatenate([s0_ref[q, 2 * p], zh], axis=1),
                     jnp.concatenate([zh, s0_ref[q, 2 * p + 1]], axis=1)], axis=0)

    ti = lax.broadcasted_iota(jnp.int32, (C, 2 * C), 0)
    tj = lax.broadcasted_iota(jnp.int32, (C, 2 * C), 1)
    tri2 = jnp.where(jnp.where(tj >= C, tj - C, tj) <= ti, 1.0, 0.0).astype(BF16)
    m0 = lax.broadcasted_iota(jnp.int32, (C, LANES), 1) < HEAD_SIZE
    gi = lax.broadcasted_iota(jnp.int32, (C, 2 * C), 0)
    gj = lax.broadcasted_iota(jnp.int32, (C, 2 * C), 1)
    gjm = jnp.where(gj >= C, gj - C, gj)
    strict = gjm < gi
    incl = gjm <= gi
    left = gj < C
    strict_l = gj < gi
    strict_r = jnp.logical_and(gj >= C, gj - C < gi)
    ei = lax.broadcasted_iota(jnp.int32, (2 * C, 2 * C), 0)
    ej = lax.broadcasted_iota(jnp.int32, (2 * C, 2 * C), 1)
    eye = jnp.where(ei == ej, 1.0, 0.0)
    bi = lax.broadcasted_iota(jnp.int32, (LANES, LANES), 0) // HEAD_SIZE
    bj = lax.broadcasted_iota(jnp.int32, (LANES, LANES), 1) // HEAD_SIZE
    bd = bi == bj

    def swap_halves(x):
        if 2 * C == LANES:
            return pltpu.roll(x, C, axis=1)
        return jnp.concatenate([x[:, C:], x[:, :C]], axis=1)

    a_p, r_p, v_p, bk, w_last = [], [], [], [], []
    for q in range(nq):
        lw = lw_ref[q]
        h1 = lw.astype(BF16)
        h2 = (lw - h1.astype(F32)).astype(BF16)
        cum = jnp.dot(tri2, jnp.concatenate([h1, h2], axis=0), preferred_element_type=F32)
        w_t = jnp.exp(cum)
        w_inv = jnp.exp(-cum)
        a_t = -kk_ref[q] * jnp.exp(cum - lw)
        b_t = b_ref[q] * w_inv
        k_t = k_ref[q] * w_inv
        r_t = r_ref[q] * w_t
        v_q = v_ref[q]
        for p in range(N_PAIRS):
            sl = slice(LANES * p, LANES * (p + 1))
            a_p.append(a_t[:, sl])
            r_p.append(r_t[:, sl])
            v_p.append(v_q[:, sl])
            bk.append(jnp.concatenate([b_t[:, sl], k_t[:, sl]], axis=0))
            w_last.append(w_t[C - 1:C, sl])
    ents = range(nq * N_PAIRS)

    g = [_dot_nt(jnp.concatenate([jnp.where(m0, a_p[e], 0.0), jnp.where(m0, r_p[e], 0.0),
                                  jnp.where(m0, 0.0, a_p[e]), jnp.where(m0, 0.0, r_p[e])], axis=0), bk[e])
         for e in ents]
    s_bd = [s_scr[e] for e in ents]
    asrs = [_dot_nt(jnp.concatenate([a_p[e], r_p[e]], axis=0), s_bd[e]) for e in ents]
    g1a = [swap_halves(g[e][2 * C:3 * C]) for e in ents]
    xpow = [jnp.concatenate([jnp.where(strict_l, g[e][:C], 0.0), jnp.where(strict_r, g1a[e], 0.0)], axis=0)
            for e in ents]
    tinv = [eye + xpow[e] for e in ents]
    n = 1
    if 2 * n < C:
        xpow = [_dot(xpow[e], xpow[e]) for e in ents]
        n *= 2
    while 2 * n < C:
        z = [_dot(jnp.concatenate([xpow[e], tinv[e]], axis=0), xpow[e]) for e in ents]
        xpow = [z[e][:2 * C] for e in ents]
        tinv = [tinv[e] + z[e][2 * C:] for e in ents]
        n *= 2
    if C > 2:
        tinv = [tinv[e] + _dot(tinv[e], xpow[e]) for e in ents]
    vm0 = [jnp.where(m0, v_p[e], 0.0) for e in ents]
    vm1 = [jnp.where(m0, 0.0, v_p[e]) for e in ents]
    mak = [jnp.where(strict, jnp.where(left, g1a[e], g[e][:C]), 0.0) for e in ents]
    rhs = [asrs[e][:C] + _dot(mak[e], jnp.concatenate([vm1[e], vm0[e]], axis=0)) for e in ents]
    uu = [_dot(tinv[e], jnp.concatenate([jnp.where(m0, rhs[e], 0.0), jnp.where(m0, 0.0, rhs[e])], axis=0))
          for e in ents]
    u = [uu[e][:C] + uu[e][C:] for e in ents]
    for e in ents:
        q, p = divmod(e, N_PAIRS)
        um0 = jnp.where(m0, u[e], 0.0)
        um1 = jnp.where(m0, 0.0, u[e])
        g0r = jnp.where(incl, g[e][C:2 * C], 0.0)
        g1r = jnp.where(incl, g[e][3 * C:], 0.0)
        if 2 * C == LANES:
            intra = _dot(jnp.concatenate([g0r, g1r], axis=1),
                         jnp.concatenate([um0, vm0[e], um1, vm1[e]], axis=0))
        else:
            intra = (_dot(g0r, jnp.concatenate([um0, vm0[e]], axis=0))
                     + _dot(g1r, jnp.concatenate([um1, vm1[e]], axis=0)))
        y_out[q, :, LANES * p:LANES * (p + 1)] = asrs[e][C:] + intra
    for e in ents:
        upd = _dot_tn(jnp.concatenate([u[e], v_p[e]], axis=0), bk[e])
        s_scr[e] = (s_bd[e] + jnp.where(bd, upd, 0.0)) * w_last[e]

    @pl.when(c_idx == pl.num_programs(1) - 1)
    def _():
        for q in range(nq):
            for p in range(N_PAIRS):
                s_fin = s_scr[q * N_PAIRS + p]
                s_out[q, 2 * p] = s_fin[:HEAD_SIZE, :HEAD_SIZE]
                s_out[q, 2 * p + 1] = s_fin[HEAD_SIZE:, HEAD_SIZE:]


def _wkv_lanes_kernel(r_ref, lw_ref, k_ref, v_ref, kk_ref, b_ref, s0_ref, y_out, s_out,
                      r_t, w_t, k_t, v_t, a_t, b_t, y_t, *, n_t, gsz):
    n_grp = r_ref.shape[0]

    def by_channel(ref, t):
        rows = [ref[g, t * gsz:(t + 1) * gsz, :] for g in range(n_grp)]
        return jnp.transpose(jnp.concatenate(rows, axis=0))

    for t in range(n_t):
        r_t[t] = by_channel(r_ref, t)
        w_t[t] = jnp.exp(by_channel(lw_ref, t))
        k_t[t] = by_channel(k_ref, t)
        v_t[t] = by_channel(v_ref, t)
        a_t[t] = -by_channel(kk_ref, t)
        b_t[t] = by_channel(b_ref, t)

    for hh in range(2):
        ch = slice(hh * HEAD_SIZE, (hh + 1) * HEAD_SIZE)

        def row_step(i, carry):
            s = s0_ref[hh, i]
            for t in range(n_t):
                sa = jnp.sum(s * a_t[t, ch, :], axis=0, keepdims=True)
                v_i = v_t[t, pl.ds(hh * HEAD_SIZE + i, 1), :]
                s = s * w_t[t, ch, :] + sa * b_t[t, ch, :] + v_i * k_t[t, ch, :]
                y_t[t, pl.ds(hh * HEAD_SIZE + i, 1), :] = jnp.sum(s * r_t[t, ch, :], axis=0, keepdims=True)
            s_out[hh, i] = s
            return carry

        lax.fori_loop(0, HEAD_SIZE, row_step, 0, unroll=8)

    for t in range(n_t):
        y = jnp.transpose(y_t[t])
        for g in range(n_grp):
            y_out[g, t * gsz:(t + 1) * gsz, :] = y[g * gsz:(g + 1) * gsz]


def _post_kernel(x_ref, y_ref, bonus_ref, g_ref, ga_ref, gbo_ref, ffn_state_ref, lnx_g_ref, lnx_b_ref,
                 w_br_ref, w_out_ref, n2g_ref, w_up_ref, conv_ref, w_down_ref, fng_ref,
                 out_ref, ffn_out, c_ffn, *, s):
    @pl.when(pl.program_id(1) == 0)
    def _():
        c_ffn[...] = ffn_state_ref[...]

    rows = x_ref.shape[0]
    ones = _head_ones(COL_CHUNK)
    z = []
    for c in range(D_MODEL // COL_CHUNK):
        cols = slice(COL_CHUNK * c, COL_CHUNK * (c + 1))
        y = y_ref[:, cols]
        dev = y - _head_sum(y, ones) * (1.0 / HEAD_SIZE)
        var = _head_sum(dev * dev, ones) * (1.0 / HEAD_SIZE)
        yn = dev * lax.rsqrt(var + GN_EPS) * lnx_g_ref[:, cols] + lnx_b_ref[:, cols]
        z.append(((yn + bonus_ref[:, cols]) * g_ref[:, cols]).astype(BF16))
    o_a = jnp.dot(jnp.concatenate(z, axis=1), w_br_ref[...], preferred_element_type=F32)
    x1 = x_ref[...] + _dot(ga_ref[...] * o_a + gbo_ref[...], w_out_ref[...])

    xb = _rms_norm(x1, n2g_ref[...]).astype(BF16)
    n_chunk = D_FF // FF_CHUNK

    def up_proj(j):
        return [jnp.dot(xb, w_up_ref[:, base + FF_CHUNK * j:base + FF_CHUNK * (j + 1)],
                        preferred_element_type=F32) for base in (0, D_FF)]

    hidden = []
    ups = up_proj(0)
    for j in range(n_chunk):
        cur = ups
        if j + 1 < n_chunk:
            ups = up_proj(j + 1)
        halves = []
        for base, up in zip((0, D_FF), cur):
            cols = slice(base + FF_CHUNK * j, base + FF_CHUNK * (j + 1))
            carry = c_ffn[:, cols]
            prev1 = _shift_rows(up, carry[s:], s)
            prev2 = _shift_rows(prev1, carry[:s], s)
            cw = conv_ref[:, cols]
            halves.append(prev2 * cw[0:1] + prev1 * cw[1:2] + up * cw[2:3])
            new = up[rows - 2 * s:]
            c_ffn[:, cols] = new
            ffn_out[:, cols] = new
        gate, val = halves
        hidden.append((gate * _sigmoid(gate) * val).astype(BF16))
    down = jnp.dot(jnp.concatenate(hidden, axis=1), w_down_ref[...], preferred_element_type=F32)
    out_ref[...] = _rms_norm(x1 + down, fng_ref[...])


def _const_spec(arr):
    nd = arr.ndim
    return pl.BlockSpec(arr.shape, lambda b, t: (0,) * nd, pipeline_mode=pl.Buffered(1))


def _row_spec(rows, cols):
    return pl.BlockSpec((None, rows, cols), lambda b, t: (b, t, 0))


def _state_spec(rows, cols):
    return pl.BlockSpec((None, rows, cols), lambda b, t: (b, 0, 0))


def _params():
    return pltpu.CompilerParams(dimension_semantics=("arbitrary", "arbitrary"),
                                vmem_limit_bytes=VMEM_LIMIT)


def _rwkv_prep(xb, sh_rkv, sh_lora, w, *, rows, s):
    nb, total, _ = xb.shape
    grid = (nb, total // rows)
    consts = [w["w_rkv"], w["w_lora"], w["mu_rkv"], w["mu_lora"], w["w0"], w["wd"],
              w["a0"], w["wa"], w["wg"], w["k_k"], w["k_a"], w["r_k"]]
    tok = jax.ShapeDtypeStruct((nb, total, D_MODEL), F32)
    return pl.pallas_call(
        functools.partial(_rwkv_prep_kernel, s=s),
        grid=grid,
        in_specs=[_row_spec(rows, D_MODEL), _state_spec(s, 3 * D_MODEL), _state_spec(s, D_LORA_PAD)]
                 + [_const_spec(c) for c in consts],
        out_specs=[_row_spec(rows, D_MODEL)] * 8
                  + [_state_spec(s, 3 * D_MODEL), _state_spec(s, D_LORA_PAD)],
        out_shape=[tok] * 8 + [jax.ShapeDtypeStruct((nb, s, 3 * D_MODEL), F32),
                               jax.ShapeDtypeStruct((nb, s, D_LORA_PAD), F32)],
        scratch_shapes=[pltpu.VMEM((s, 3 * D_MODEL), F32), pltpu.VMEM((s, D_LORA_PAD), F32)],
        compiler_params=_params(),
        name="rwkv_prep",
    )(xb, sh_rkv, sh_lora, *consts)


def _sc_gate(x, sc_state, w, *, rows, s):
    nb, total, _ = x.shape
    grid = (nb, total // rows)
    consts = [w["norm1_g"], w["w_sc"], w["w_gate"], w["b_gate"], w["conv_sc"], w["w_branch_sc"]]
    tok = jax.ShapeDtypeStruct((nb, total, D_MODEL), F32)
    return pl.pallas_call(
        functools.partial(_sc_gate_kernel, s=s),
        grid=grid,
        in_specs=[_row_spec(rows, D_MODEL), _state_spec(2 * s, D_MODEL)] + [_const_spec(c) for c in consts],
        out_specs=[_row_spec(rows, D_MODEL)] * 3 + [_state_spec(2 * s, D_MODEL)],
        out_shape=[tok] * 2 + [jax.ShapeDtypeStruct((nb, total, D_MODEL), BF16),
                               jax.ShapeDtypeStruct((nb, 2 * s, D_MODEL), F32)],
        scratch_shapes=[pltpu.VMEM((2 * s, D_MODEL), F32)],
        compiler_params=_params(),
        name="sc_gate",
    )(x, sc_state, *consts)


def _wkv(r, lw, k, v, kk, b, s0, *, chunk, nq):
    nb, total, _ = r.shape
    grid = (nb // nq, total // chunk)
    tok_spec = pl.BlockSpec((nq, chunk, D_MODEL), lambda i, c: (i, c, 0))
    st_spec = pl.BlockSpec((nq, N_HEADS, HEAD_SIZE, HEAD_SIZE), lambda i, c: (i, 0, 0, 0))
    return pl.pallas_call(
        functools.partial(_wkv_kernel, C=chunk, nq=nq),
        grid=grid,
        in_specs=[tok_spec] * 6 + [st_spec],
        out_specs=[tok_spec, st_spec],
        out_shape=[jax.ShapeDtypeStruct((nb, total, D_MODEL), F32),
                   jax.ShapeDtypeStruct((nb, N_HEADS, HEAD_SIZE, HEAD_SIZE), F32)],
        scratch_shapes=[pltpu.VMEM((nq * N_PAIRS, LANES, LANES), F32)],
        compiler_params=_params(),
        name="wkv",
    )(r, lw, k, v, kk, b, s0)


def _wkv_lanes(r, lw, k, v, kk, b, s0, *, n_t):
    n_grp, total, _ = r.shape
    batch = s0.shape[-1]
    tok_spec = pl.BlockSpec((n_grp, total, LANES), lambda p: (0, 0, p))
    st_spec = pl.BlockSpec((2, HEAD_SIZE, HEAD_SIZE, batch), lambda p: (p, 0, 0, 0))
    return pl.pallas_call(
        functools.partial(_wkv_lanes_kernel, n_t=n_t, gsz=total // n_t),
        grid=(N_PAIRS,),
        in_specs=[tok_spec] * 6 + [st_spec],
        out_specs=[tok_spec, st_spec],
        out_shape=[jax.ShapeDtypeStruct(r.shape, F32), jax.ShapeDtypeStruct(s0.shape, F32)],
        scratch_shapes=[pltpu.VMEM((n_t, LANES, batch), F32)] * 7,
        compiler_params=pltpu.CompilerParams(dimension_semantics=("arbitrary",),
                                             vmem_limit_bytes=VMEM_LIMIT),
        name="wkv_lanes",
    )(r, lw, k, v, kk, b, s0)


def _post(x, y, bonus, g, ga, gbo, ffn_state, w, *, rows, s):
    nb, total, _ = x.shape
    grid = (nb, total // rows)
    consts = [w["lnx_g"], w["lnx_b"], w["w_branch_rwkv"], w["w_out"], w["norm2_g"], w["w_up"],
              w["conv_ffn"], w["w_down"], w["final_norm_g"]]
    return pl.pallas_call(
        functools.partial(_post_kernel, s=s),
        grid=grid,
        in_specs=[_row_spec(rows, D_MODEL)] * 6 + [_state_spec(2 * s, 2 * D_FF)]
                 + [_const_spec(c) for c in consts],
        out_specs=[_row_spec(rows, D_MODEL), _state_spec(2 * s, 2 * D_FF)],
        out_shape=[jax.ShapeDtypeStruct((nb, total, D_MODEL), F32),
                   jax.ShapeDtypeStruct((nb, 2 * s, 2 * D_FF), F32)],
        scratch_shapes=[pltpu.VMEM((2 * s, 2 * D_FF), F32)],
        compiler_params=_params(),
        name="post",
    )(x, y, bonus, g, ga, gbo, ffn_state, *consts)


def _pad_lora_cols(a):
    pad = lambda t, n: jnp.pad(t, [(0, 0)] * (t.ndim - 1) + [(0, n - t.shape[-1])])
    return jnp.concatenate([pad(a[..., :64], 128), pad(a[..., 64:128], 128), pad(a[..., 128:], 256)], axis=-1)


def _unpad_lora_cols(a):
    return jnp.concatenate([a[..., :64], a[..., 128:192], a[..., 256:256 + D_GATE_LORA]], axis=-1)


def _prep_weights(norm1_g, w_in, b_gate, mu_shift, w0, w_decay_up, a0, w_aaa_up, w_gate_up, k_k, k_a,
                  r_k, lnx_g, lnx_b, w_branch_rwkv, w_branch_sc, conv_sc, w_out, norm2_g, w_up, conv_ffn,
                  w_down, final_norm_g):
    row = lambda t: t.reshape(1, -1).astype(F32)
    d3 = 3 * D_MODEL
    n_lora = D_DECAY_LORA + D_AAA_LORA + D_GATE_LORA
    pad_rows = lambda t, n: jnp.pad(t, [(0, n - t.shape[0]), (0, 0)])
    return {
        "norm1_g": row(norm1_g),
        "w_rkv": w_in[:, :d3].astype(BF16),
        "w_lora": _pad_lora_cols(w_in[:, d3:d3 + n_lora]).astype(BF16),
        "w_sc": w_in[:, d3 + n_lora:2 * d3 + n_lora].astype(BF16),
        "w_gate": w_in[:, 2 * d3 + n_lora:].astype(BF16),
        "b_gate": row(b_gate),
        "mu_rkv": row(mu_shift[:d3]),
        "mu_lora": _pad_lora_cols(row(mu_shift[d3:])),
        "w0": row(w0),
        "wd": pad_rows(w_decay_up, 128).astype(BF16),
        "a0": row(a0),
        "wa": pad_rows(w_aaa_up, 128).astype(BF16),
        "wg": pad_rows(w_gate_up, 256).astype(BF16),
        "k_k": row(k_k), "k_a": row(k_a), "r_k": row(r_k),
        "lnx_g": row(lnx_g), "lnx_b": row(lnx_b),
        "w_branch_rwkv": w_branch_rwkv.astype(BF16),
        "w_branch_sc": w_branch_sc.astype(BF16),
        "conv_sc": conv_sc.astype(F32),
        "w_out": w_out.astype(BF16),
        "norm2_g": row(norm2_g),
        "w_up": w_up.astype(BF16),
        "conv_ffn": conv_ffn.astype(F32),
        "w_down": w_down.astype(BF16),
        "final_norm_g": row(final_norm_g),
    }


def _layer(x, s_wkv, sh_rkv, sh_lora, sc_state, ffn_state, w, *, rows, s, wkv):
    ga, gbo, xb, sc_new = _sc_gate(x, sc_state, w, rows=rows, s=s)
    r, lw, k, v, kk, b, g, bonus, shr, shl = _rwkv_prep(xb, sh_rkv, sh_lora, w, rows=rows, s=s)
    y, s_new = wkv(r, lw, k, v, kk, b, s_wkv)
    out, ffn_new = _post(x, y, bonus, g, ga, gbo, ffn_state, w, rows=rows, s=s)
    return out, s_new, shr, shl, sc_new, ffn_new


def kernel(x_prompt, x_sample, state_wkv, state_shift, state_sc_conv, state_ffn_conv, meta_tokens,
           norm1_g, w_in, b_gate, mu_shift, w0, w_decay_up, a0, w_aaa_up, w_gate_up, k_k, k_a, r_k,
           lnx_g, lnx_b, w_branch_rwkv, w_branch_sc, conv_sc, w_out, norm2_g, w_up, conv_ffn, w_down,
           final_norm_g):
    w = _prep_weights(norm1_g[0], w_in[0], b_gate[0], mu_shift[0], w0[0], w_decay_up[0], a0[0],
                      w_aaa_up[0], w_gate_up[0], k_k[0], k_a[0], r_k[0], lnx_g[0], lnx_b[0],
                      w_branch_rwkv[0], w_branch_sc[0], conv_sc[0], w_out[0], norm2_g[0], w_up[0],
                      conv_ffn[0], w_down[0], final_norm_g)
    d3 = 3 * D_MODEL

    bp, seq, _ = x_prompt.shape
    zeros = lambda *shape: jnp.zeros(shape, F32)
    _, m_wkv, m_shr, m_shl, m_sc, m_ffn = _layer(
        meta_tokens.astype(F32)[None], zeros(1, N_HEADS, HEAD_SIZE, HEAD_SIZE), zeros(1, 1, d3),
        zeros(1, 1, D_LORA_PAD), zeros(1, 2, D_MODEL), zeros(1, 2, 2 * D_FF), w,
        rows=N_META, s=1, wkv=functools.partial(_wkv, chunk=N_META, nq=1))

    rep = lambda t: jnp.broadcast_to(t, (bp,) + t.shape[1:])
    y_prompt, p_wkv, p_shr, p_shl, p_sc, p_ffn = _layer(
        x_prompt, rep(m_wkv), rep(m_shr), rep(m_shl), rep(m_sc), rep(m_ffn), w,
        rows=512, s=1, wkv=functools.partial(_wkv, chunk=64, nq=4))

    bs, ts, _ = x_sample.shape
    n_grp = 1
    gsz = bs // n_grp

    def to_rows(t):
        n, c = t.shape[1:]
        return t.reshape(n_grp, gsz, n, c).transpose(0, 2, 1, 3).reshape(n_grp, n * gsz, c)

    sh = state_shift[0].reshape(n_grp, gsz, -1)
    y_s, s_wkv, s_shr, s_shl, s_sc, s_ffn = _layer(
        to_rows(x_sample), state_wkv[0].transpose(1, 2, 3, 0), sh[..., :d3], _pad_lora_cols(sh[..., d3:]),
        to_rows(state_sc_conv[0]), to_rows(state_ffn_conv[0]), w,
        rows=ts * gsz, s=gsz, wkv=functools.partial(_wkv_lanes, n_t=ts))
    s_wkv = s_wkv.transpose(3, 0, 1, 2)

    def rows_to_batch(t, n):
        c = t.shape[-1]
        return t.reshape(n_grp, n, gsz, c).transpose(0, 2, 1, 3).reshape(bs, n, c)

    y_sample = rows_to_batch(y_s, ts)
    shift_p = jnp.concatenate([p_shr[:, 0], _unpad_lora_cols(p_shl[:, 0])], axis=-1)
    shift_s = jnp.concatenate([rows_to_batch(s_shr, 1)[:, 0], _unpad_lora_cols(rows_to_batch(s_shl, 1)[:, 0])],
                              axis=-1)
    return (y_prompt, y_sample,
            p_wkv[None], s_wkv[None],
            shift_p[None], shift_s[None],
            p_sc[None], rows_to_batch(s_sc, 2)[None],
            p_ffn[None], rows_to_batch(s_ffn, 2)[None])
```

```python
import functools

import jax
import jax.numpy as jnp
from jax import lax
from jax.experimental import pallas as pl
from jax.experimental.pallas import tpu as pltpu

D_MODEL = 1024
N_META = 16
HEAD_SIZE = 64
N_HEADS = D_MODEL // HEAD_SIZE
LANES = 128
N_PAIRS = D_MODEL // LANES
D_DECAY_LORA = 64
D_AAA_LORA = 64
D_GATE_LORA = 160
D_LORA_PAD = 512
D_FF = 2816
FF_CHUNK = 256
COL_CHUNK = 256
CONV_W = 3
NEG_LOG2_E = -1.4426950408889634
EXP_NEG_HALF = 0.6065306597126334
RMS_EPS = 1e-6
GN_EPS = 64e-5
VMEM_LIMIT = 60 * 1024 * 1024

F32 = jnp.float32
BF16 = jnp.bfloat16


def _dot(a, b):
    return jnp.dot(a.astype(BF16), b.astype(BF16), preferred_element_type=F32)


def _dot_nt(a, b):
    return lax.dot_general(a.astype(BF16), b.astype(BF16), (((1,), (1,)), ((), ())),
                           preferred_element_type=F32)


def _dot_tn(a, b):
    return lax.dot_general(a.astype(BF16), b.astype(BF16), (((0,), (0,)), ((), ())),
                           preferred_element_type=F32)


def _head_ones(n):
    r = lax.broadcasted_iota(jnp.int32, (n, n), 0) // HEAD_SIZE
    c = lax.broadcasted_iota(jnp.int32, (n, n), 1) // HEAD_SIZE
    return jnp.where(r == c, 1.0, 0.0).astype(BF16)


def _head_sum(x, ones):
    return jnp.dot(x.astype(BF16), ones, preferred_element_type=F32)


def _rms_norm(x, g):
    return x * lax.rsqrt(jnp.mean(x * x, axis=-1, keepdims=True) + RMS_EPS) * g


def _sigmoid(x):
    return 1.0 / (1.0 + jnp.exp2(x * NEG_LOG2_E))


def _shift_rows(cur, carry, s):
    rows = cur.shape[0]
    if s % 8 == 0:
        return jnp.concatenate([carry, cur[:rows - s]], axis=0)
    assert s == 1
    rolled = pltpu.roll(cur, 1, axis=0)
    row = lax.broadcasted_iota(jnp.int32, cur.shape, 0)
    return jnp.where(row == 0, carry, rolled)


def _rwkv_prep_kernel(xb_ref, sh_rkv_ref, sh_lora_ref, w_rkv_ref, w_lora_ref, mu_rkv_ref,
                      mu_lora_ref, w0_ref, wd_ref, a0_ref, wa_ref, wg_ref, kk_ref, ka_ref, rk_ref,
                      r_out, lw_out, k_out, v_out, kk_out, b_out, g_out, bonus_out, shr_out, shl_out,
                      c_rkv, c_lora, *, s):
    @pl.when(pl.program_id(1) == 0)
    def _():
        c_rkv[...] = sh_rkv_ref[...]
        c_lora[...] = sh_lora_ref[...]

    rows = xb_ref.shape[0]
    xb = xb_ref[...]
    p_lora = jnp.dot(xb, w_lora_ref[...], preferred_element_type=F32)
    prev_lora = _shift_rows(p_lora, c_lora[...], s)
    xl = p_lora + (prev_lora - p_lora) * mu_lora_ref[...]
    new_lora = p_lora[rows - s:]
    c_lora[...] = new_lora
    shl_out[...] = new_lora
    t_xw = jnp.tanh(xl[:, :128]).astype(BF16)
    xa = xl[:, 128:256].astype(BF16)
    s_xg = _sigmoid(xl[:, 256:]).astype(BF16)

    def proj(c):
        return [jnp.dot(xb, w_rkv_ref[:, base + COL_CHUNK * c:base + COL_CHUNK * (c + 1)],
                        preferred_element_type=F32) for base in (0, D_MODEL, 2 * D_MODEL)]

    ones = _head_ones(COL_CHUNK)
    nxt = proj(0)
    for c in range(D_MODEL // COL_CHUNK):
        cur = nxt
        if c + 1 < D_MODEL // COL_CHUNK:
            nxt = proj(c + 1)
        cols = slice(COL_CHUNK * c, COL_CHUNK * (c + 1))
        shifted = []
        for part, p in enumerate(cur):
            pcols = slice(part * D_MODEL + COL_CHUNK * c, part * D_MODEL + COL_CHUNK * (c + 1))
            prev = _shift_rows(p, c_rkv[:, pcols], s)
            shifted.append(p + (prev - p) * mu_rkv_ref[:, pcols])
            new = p[rows - s:]
            c_rkv[:, pcols] = new
            shr_out[:, pcols] = new
        r, k, v = shifted

        zw = w0_ref[:, cols] + jnp.dot(t_xw, wd_ref[:, cols], preferred_element_type=F32)
        lw_out[:, cols] = _sigmoid(zw) * (-EXP_NEG_HALF)
        a = _sigmoid(a0_ref[:, cols] + jnp.dot(xa, wa_ref[:, cols], preferred_element_type=F32))
        g_out[:, cols] = jnp.dot(s_xg, wg_ref[:, cols], preferred_element_type=F32)

        kkr = k * kk_ref[:, cols]
        kk = kkr * lax.rsqrt(jnp.maximum(_head_sum(kkr * kkr, ones), 1e-24))
        k2 = k * (1.0 + (a - 1.0) * ka_ref[:, cols])
        r_out[:, cols] = r
        k_out[:, cols] = k2
        v_out[:, cols] = v
        kk_out[:, cols] = kk
        b_out[:, cols] = kk * a
        bonus_out[:, cols] = _head_sum(r * k2 * rk_ref[:, cols], ones) * v


def _sc_gate_kernel(x_ref, sc_state_ref, n1g_ref, w_sc_ref, w_gate_ref, b_gate_ref, conv_ref,
                    w_bsc_ref, ga_out, gbo_out, xb_out, sc_out, c_sc, *, s):
    @pl.when(pl.program_id(1) == 0)
    def _():
        c_sc[...] = sc_state_ref[...]

    rows = x_ref.shape[0]
    xb = _rms_norm(x_ref[...], n1g_ref[...]).astype(BF16)
    xb_out[...] = xb

    def proj(c):
        cs = lambda base: slice(base + COL_CHUNK * c, base + COL_CHUNK * (c + 1))
        sc = [jnp.dot(xb, w_sc_ref[:, cs(base)], preferred_element_type=F32)
              for base in (0, D_MODEL, 2 * D_MODEL)]
        gate = [jnp.dot(xb, w_gate_ref[:, cs(base)], preferred_element_type=F32) + b_gate_ref[:, cs(base)]
                for base in (0, D_MODEL)]
        return sc + gate

    mid = []
    gb = []
    nxt = proj(0)
    for c in range(D_MODEL // COL_CHUNK):
        cur = nxt
        if c + 1 < D_MODEL // COL_CHUNK:
            nxt = proj(c + 1)
        cols = slice(COL_CHUNK * c, COL_CHUNK * (c + 1))
        h, b_g, c_g, pg_a, pg_b = cur
        u = c_g * h
        carry = c_sc[:, cols]
        prev1 = _shift_rows(u, carry[s:], s)
        prev2 = _shift_rows(prev1, carry[:s], s)
        cw = conv_ref[:, cols]
        conv = prev2 * cw[0:1] + prev1 * cw[1:2] + u * cw[2:3]
        mid.append((b_g * conv).astype(BF16))
        new = u[rows - 2 * s:]
        c_sc[:, cols] = new
        sc_out[:, cols] = new
        ga_out[:, cols] = _sigmoid(pg_a)
        gb.append(_sigmoid(pg_b))
    o_b = jnp.dot(jnp.concatenate(mid, axis=1), w_bsc_ref[...], preferred_element_type=F32)
    for c in range(D_MODEL // COL_CHUNK):
        cols = slice(COL_CHUNK * c, COL_CHUNK * (c + 1))
        gbo_out[:, cols] = gb[c] * o_b[:, cols]


def _wkv_kernel(r_ref, lw_ref, k_ref, v_ref, kk_ref, b_ref, s0_ref, y_out, s_out, s_scr, *, C, nq):
    c_idx = pl.program_id(1)

    @pl.when(c_idx == 0)
    def _():
        zh = jnp.zeros((HEAD_SIZE, HEAD_SIZE), F32)
        for q in range(nq):
            for p in range(N_PAIRS):
                s_scr[q * N_PAIRS + p] = jnp.concatenate(
                    [jnp.concatenate([s0_ref[q, 2 * p], zh], axis=1),
                     jnp.concatenate([zh, s0_ref[q, 2 * p + 1]], axis=1)], axis=0)

    ti = lax.broadcasted_iota(jnp.int32, (C, 2 * C), 0)
    tj = lax.broadcasted_iota(jnp.int32, (C, 2 * C), 1)
    tri2 = jnp.where(jnp.where(tj >= C, tj - C, tj) <= ti, 1.0, 0.0).astype(BF16)
    m0 = lax.broadcasted_iota(jnp.int32, (C, LANES), 1) < HEAD_SIZE
    gi = lax.broadcasted_iota(jnp.int32, (C, 2 * C), 0)
    gj = lax.broadcasted_iota(jnp.int32, (C, 2 * C), 1)
    gjm = jnp.where(gj >= C, gj - C, gj)
    strict = gjm < gi
    incl = gjm <= gi
    left = gj < C
    strict_l = gj < gi
    strict_r = jnp.logical_and(gj >= C, gj - C < gi)
    ei = lax.broadcasted_iota(jnp.int32, (2 * C, 2 * C), 0)
    ej = lax.broadcasted_iota(jnp.int32, (2 * C, 2 * C), 1)
    eye = jnp.where(ei == ej, 1.0, 0.0)
    bi = lax.broadcasted_iota(jnp.int32, (LANES, LANES), 0) // HEAD_SIZE
    bj = lax.broadcasted_iota(jnp.int32, (LANES, LANES), 1) // HEAD_SIZE
    bd = bi == bj

    def swap_halves(x):
        if 2 * C == LANES:
            return pltpu.roll(x, C, axis=1)
        return jnp.concatenate([x[:, C:], x[:, :C]], axis=1)

    a_p, r_p, v_p, bk, w_last = [], [], [], [], []
    for q in range(nq):
        lw = lw_ref[q]
        h1 = lw.astype(BF16)
        h2 = (lw - h1.astype(F32)).astype(BF16)
        cum = jnp.dot(tri2, jnp.concatenate([h1, h2], axis=0), preferred_element_type=F32)
        w_t = jnp.exp(cum)
        w_inv = jnp.exp(-cum)
        a_t = -kk_ref[q] * jnp.exp(cum - lw)
        b_t = b_ref[q] * w_inv
        k_t = k_ref[q] * w_inv
        r_t = r_ref[q] * w_t
        v_q = v_ref[q]
        for p in range(N_PAIRS):
            sl = slice(LANES * p, LANES * (p + 1))
            a_p.append(a_t[:, sl])
            r_p.append(r_t[:, sl])
            v_p.append(v_q[:, sl])
            bk.append(jnp.concatenate([b_t[:, sl], k_t[:, sl]], axis=0))
            w_last.append(w_t[C - 1:C, sl])
    ents = range(nq * N_PAIRS)

    g = [_dot_nt(jnp.concatenate([jnp.where(m0, a_p[e], 0.0), jnp.where(m0, r_p[e], 0.0),
                                  jnp.where(m0, 0.0, a_p[e]), jnp.where(m0, 0.0, r_p[e])], axis=0), bk[e])
         for e in ents]
    s_bd = [s_scr[e] for e in ents]
    s_t = [jnp.transpose(s_bd[e]) for e in ents]
    rs = [_dot(r_p[e], s_t[e]) for e in ents]
    g1a = [swap_halves(g[e][2 * C:3 * C]) for e in ents]
    xpow = [jnp.concatenate([jnp.where(strict_l, g[e][:C], 0.0), jnp.where(strict_r, g1a[e], 0.0)], axis=0)
            for e in ents]
    tinv = [eye + xpow[e] for e in ents]
    n = 1
    if 2 * n < C:
        xpow = [_dot(xpow[e], xpow[e]) for e in ents]
        n *= 2
    while 2 * n < C:
        z = [_dot(jnp.concatenate([xpow[e], tinv[e]], axis=0), xpow[e]) for e in ents]
        xpow = [z[e][:2 * C] for e in ents]
        tinv = [tinv[e] + z[e][2 * C:] for e in ents]
        n *= 2
    if C > 2:
        tinv = [tinv[e] + _dot(tinv[e], xpow[e]) for e in ents]
    vm0 = [jnp.where(m0, v_p[e], 0.0) for e in ents]
    vm1 = [jnp.where(m0, 0.0, v_p[e]) for e in ents]
    mak = [jnp.where(strict, jnp.where(left, g1a[e], g[e][:C]), 0.0) for e in ents]
    rhs = [_dot(jnp.concatenate([a_p[e], mak[e]], axis=1), jnp.concatenate([s_t[e], vm1[e], vm0[e]], axis=0))
           for e in ents]
    uu = [_dot(tinv[e], jnp.concatenate([jnp.where(m0, rhs[e], 0.0), jnp.where(m0, 0.0, rhs[e])], axis=0))
          for e in ents]
    u = [uu[e][:C] + uu[e][C:] for e in ents]
    for e in ents:
        q, p = divmod(e, N_PAIRS)
        um0 = jnp.where(m0, u[e], 0.0)
        um1 = jnp.where(m0, 0.0, u[e])
        g0r = jnp.where(incl, g[e][C:2 * C], 0.0)
        g1r = jnp.where(incl, g[e][3 * C:], 0.0)
        if 2 * C == LANES:
            intra = _dot(jnp.concatenate([g0r, g1r], axis=1),
                         jnp.concatenate([um0, vm0[e], um1, vm1[e]], axis=0))
        else:
            intra = (_dot(g0r, jnp.concatenate([um0, vm0[e]], axis=0))
                     + _dot(g1r, jnp.concatenate([um1, vm1[e]], axis=0)))
        y_out[q, :, LANES * p:LANES * (p + 1)] = rs[e] + intra
    for e in ents:
        upd = _dot_tn(jnp.concatenate([u[e], v_p[e]], axis=0), bk[e])
        s_scr[e] = (s_bd[e] + jnp.where(bd, upd, 0.0)) * w_last[e]

    @pl.when(c_idx == pl.num_programs(1) - 1)
    def _():
        for q in range(nq):
            for p in range(N_PAIRS):
                s_fin = s_scr[q * N_PAIRS + p]
                s_out[q, 2 * p] = s_fin[:HEAD_SIZE, :HEAD_SIZE]
                s_out[q, 2 * p + 1] = s_fin[HEAD_SIZE:, HEAD_SIZE:]


def _wkv_lanes_kernel(r_ref, lw_ref, k_ref, v_ref, kk_ref, b_ref, s0_ref, y_out, s_out,
                      r_t, w_t, k_t, v_t, a_t, b_t, y_t, *, n_t, gsz):
    n_grp = r_ref.shape[0]

    def by_channel(ref, t):
        rows = [ref[g, t * gsz:(t + 1) * gsz, :] for g in range(n_grp)]
        return jnp.transpose(jnp.concatenate(rows, axis=0))

    for t in range(n_t):
        r_t[t] = by_channel(r_ref, t)
        w_t[t] = jnp.exp(by_channel(lw_ref, t))
        k_t[t] = by_channel(k_ref, t)
        v_t[t] = by_channel(v_ref, t)
        a_t[t] = -by_channel(kk_ref, t)
        b_t[t] = by_channel(b_ref, t)

    for hh in range(2):
        ch = slice(hh * HEAD_SIZE, (hh + 1) * HEAD_SIZE)

        def row_step(i, carry):
            s = s0_ref[hh, i]
            for t in range(n_t):
                sa = jnp.sum(s * a_t[t, ch, :], axis=0, keepdims=True)
                v_i = v_t[t, pl.ds(hh * HEAD_SIZE + i, 1), :]
                s = s * w_t[t, ch, :] + sa * b_t[t, ch, :] + v_i * k_t[t, ch, :]
                y_t[t, pl.ds(hh * HEAD_SIZE + i, 1), :] = jnp.sum(s * r_t[t, ch, :], axis=0, keepdims=True)
            s_out[hh, i] = s
            return carry

        lax.fori_loop(0, HEAD_SIZE, row_step, 0, unroll=8)

    for t in range(n_t):
        y = jnp.transpose(y_t[t])
        for g in range(n_grp):
            y_out[g, t * gsz:(t + 1) * gsz, :] = y[g * gsz:(g + 1) * gsz]


def _post_kernel(x_ref, y_ref, bonus_ref, g_ref, ga_ref, gbo_ref, ffn_state_ref, lnx_g_ref, lnx_b_ref,
                 w_br_ref, w_out_ref, n2g_ref, w_up_ref, conv_ref, w_down_ref, fng_ref,
                 out_ref, ffn_out, c_ffn, *, s):
    @pl.when(pl.program_id(1) == 0)
    def _():
        c_ffn[...] = ffn_state_ref[...]

    rows = x_ref.shape[0]
    ones = _head_ones(COL_CHUNK)
    z = []
    for c in range(D_MODEL // COL_CHUNK):
        cols = slice(COL_CHUNK * c, COL_CHUNK * (c + 1))
        y = y_ref[:, cols]
        dev = y - _head_sum(y, ones) * (1.0 / HEAD_SIZE)
        var = _head_sum(dev * dev, ones) * (1.0 / HEAD_SIZE)
        yn = dev * lax.rsqrt(var + GN_EPS) * lnx_g_ref[:, cols] + lnx_b_ref[:, cols]
        z.append(((yn + bonus_ref[:, cols]) * g_ref[:, cols]).astype(BF16))
    o_a = jnp.dot(jnp.concatenate(z, axis=1), w_br_ref[...], preferred_element_type=F32)
    x1 = x_ref[...] + _dot(ga_ref[...] * o_a + gbo_ref[...], w_out_ref[...])

    xb = _rms_norm(x1, n2g_ref[...]).astype(BF16)
    n_chunk = D_FF // FF_CHUNK

    def up_proj(j):
        return [jnp.dot(xb, w_up_ref[:, base + FF_CHUNK * j:base + FF_CHUNK * (j + 1)],
                        preferred_element_type=F32) for base in (0, D_FF)]

    hidden = []
    ups = up_proj(0)
    for j in range(n_chunk):
        cur = ups
        if j + 1 < n_chunk:
            ups = up_proj(j + 1)
        halves = []
        for base, up in zip((0, D_FF), cur):
            cols = slice(base + FF_CHUNK * j, base + FF_CHUNK * (j + 1))
            carry = c_ffn[:, cols]
            prev1 = _shift_rows(up, carry[s:], s)
            prev2 = _shift_rows(prev1, carry[:s], s)
            cw = conv_ref[:, cols]
            halves.append(prev2 * cw[0:1] + prev1 * cw[1:2] + up * cw[2:3])
            new = up[rows - 2 * s:]
            c_ffn[:, cols] = new
            ffn_out[:, cols] = new
        gate, val = halves
        hidden.append((gate * _sigmoid(gate) * val).astype(BF16))
    down = jnp.dot(jnp.concatenate(hidden, axis=1), w_down_ref[...], preferred_element_type=F32)
    out_ref[...] = _rms_norm(x1 + down, fng_ref[...])


def _const_spec(arr):
    nd = arr.ndim
    return pl.BlockSpec(arr.shape, lambda b, t: (0,) * nd, pipeline_mode=pl.Buffered(1))


def _row_spec(rows, cols):
    return pl.BlockSpec((None, rows, cols), lambda b, t: (b, t, 0))


def _state_spec(rows, cols):
    return pl.BlockSpec((None, rows, cols), lambda b, t: (b, 0, 0))


def _params():
    return pltpu.CompilerParams(dimension_semantics=("arbitrary", "arbitrary"),
                                vmem_limit_bytes=VMEM_LIMIT)


def _rwkv_prep(xb, sh_rkv, sh_lora, w, *, rows, s):
    nb, total, _ = xb.shape
    grid = (nb, total // rows)
    consts = [w["w_rkv"], w["w_lora"], w["mu_rkv"], w["mu_lora"], w["w0"], w["wd"],
              w["a0"], w["wa"], w["wg"], w["k_k"], w["k_a"], w["r_k"]]
    tok = jax.ShapeDtypeStruct((nb, total, D_MODEL), F32)
    return pl.pallas_call(
        functools.partial(_rwkv_prep_kernel, s=s),
        grid=grid,
        in_specs=[_row_spec(rows, D_MODEL), _state_spec(s, 3 * D_MODEL), _state_spec(s, D_LORA_PAD)]
                 + [_const_spec(c) for c in consts],
        out_specs=[_row_spec(rows, D_MODEL)] * 8
                  + [_state_spec(s, 3 * D_MODEL), _state_spec(s, D_LORA_PAD)],
        out_shape=[tok] * 8 + [jax.ShapeDtypeStruct((nb, s, 3 * D_MODEL), F32),
                               jax.ShapeDtypeStruct((nb, s, D_LORA_PAD), F32)],
        scratch_shapes=[pltpu.VMEM((s, 3 * D_MODEL), F32), pltpu.VMEM((s, D_LORA_PAD), F32)],
        compiler_params=_params(),
        name="rwkv_prep",
    )(xb, sh_rkv, sh_lora, *consts)


def _sc_gate(x, sc_state, w, *, rows, s):
    nb, total, _ = x.shape
    grid = (nb, total // rows)
    consts = [w["norm1_g"], w["w_sc"], w["w_gate"], w["b_gate"], w["conv_sc"], w["w_branch_sc"]]
    tok = jax.ShapeDtypeStruct((nb, total, D_MODEL), F32)
    return pl.pallas_call(
        functools.partial(_sc_gate_kernel, s=s),
        grid=grid,
        in_specs=[_row_spec(rows, D_MODEL), _state_spec(2 * s, D_MODEL)] + [_const_spec(c) for c in consts],
        out_specs=[_row_spec(rows, D_MODEL)] * 3 + [_state_spec(2 * s, D_MODEL)],
        out_shape=[tok] * 2 + [jax.ShapeDtypeStruct((nb, total, D_MODEL), BF16),
                               jax.ShapeDtypeStruct((nb, 2 * s, D_MODEL), F32)],
        scratch_shapes=[pltpu.VMEM((2 * s, D_MODEL), F32)],
        compiler_params=_params(),
        name="sc_gate",
    )(x, sc_state, *consts)


def _wkv(r, lw, k, v, kk, b, s0, *, chunk, nq):
    nb, total, _ = r.shape
    grid = (nb // nq, total // chunk)
    tok_spec = pl.BlockSpec((nq, chunk, D_MODEL), lambda i, c: (i, c, 0))
    st_spec = pl.BlockSpec((nq, N_HEADS, HEAD_SIZE, HEAD_SIZE), lambda i, c: (i, 0, 0, 0))
    return pl.pallas_call(
        functools.partial(_wkv_kernel, C=chunk, nq=nq),
        grid=grid,
        in_specs=[tok_spec] * 6 + [st_spec],
        out_specs=[tok_spec, st_spec],
        out_shape=[jax.ShapeDtypeStruct((nb, total, D_MODEL), F32),
                   jax.ShapeDtypeStruct((nb, N_HEADS, HEAD_SIZE, HEAD_SIZE), F32)],
        scratch_shapes=[pltpu.VMEM((nq * N_PAIRS, LANES, LANES), F32)],
        compiler_params=_params(),
        name="wkv",
    )(r, lw, k, v, kk, b, s0)


def _wkv_lanes(r, lw, k, v, kk, b, s0, *, n_t):
    n_grp, total, _ = r.shape
    batch = s0.shape[-1]
    tok_spec = pl.BlockSpec((n_grp, total, LANES), lambda p: (0, 0, p))
    st_spec = pl.BlockSpec((2, HEAD_SIZE, HEAD_SIZE, batch), lambda p: (p, 0, 0, 0))
    return pl.pallas_call(
        functools.partial(_wkv_lanes_kernel, n_t=n_t, gsz=total // n_t),
        grid=(N_PAIRS,),
        in_specs=[tok_spec] * 6 + [st_spec],
        out_specs=[tok_spec, st_spec],
        out_shape=[jax.ShapeDtypeStruct(r.shape, F32), jax.ShapeDtypeStruct(s0.shape, F32)],
        scratch_shapes=[pltpu.VMEM((n_t, LANES, batch), F32)] * 7,
        compiler_params=pltpu.CompilerParams(dimension_semantics=("arbitrary",),
                                             vmem_limit_bytes=VMEM_LIMIT),
        name="wkv_lanes",
    )(r, lw, k, v, kk, b, s0)


def _post(x, y, bonus, g, ga, gbo, ffn_state, w, *, rows, s):
    nb, total, _ = x.shape
    grid = (nb, total // rows)
    consts = [w["lnx_g"], w["lnx_b"], w["w_branch_rwkv"], w["w_out"], w["norm2_g"], w["w_up"],
              w["conv_ffn"], w["w_down"], w["final_norm_g"]]
    return pl.pallas_call(
        functools.partial(_post_kernel, s=s),
        grid=grid,
        in_specs=[_row_spec(rows, D_MODEL)] * 6 + [_state_spec(2 * s, 2 * D_FF)]
                 + [_const_spec(c) for c in consts],
        out_specs=[_row_spec(rows, D_MODEL), _state_spec(2 * s, 2 * D_FF)],
        out_shape=[jax.ShapeDtypeStruct((nb, total, D_MODEL), F32),
                   jax.ShapeDtypeStruct((nb, 2 * s, 2 * D_FF), F32)],
        scratch_shapes=[pltpu.VMEM((2 * s, 2 * D_FF), F32)],
        compiler_params=_params(),
        name="post",
    )(x, y, bonus, g, ga, gbo, ffn_state, *consts)


def _pad_lora_cols(a):
    pad = lambda t, n: jnp.pad(t, [(0, 0)] * (t.ndim - 1) + [(0, n - t.shape[-1])])
    return jnp.concatenate([pad(a[..., :64], 128), pad(a[..., 64:128], 128), pad(a[..., 128:], 256)], axis=-1)


def _unpad_lora_cols(a):
    return jnp.concatenate([a[..., :64], a[..., 128:192], a[..., 256:256 + D_GATE_LORA]], axis=-1)


def _prep_weights(norm1_g, w_in, b_gate, mu_shift, w0, w_decay_up, a0, w_aaa_up, w_gate_up, k_k, k_a,
                  r_k, lnx_g, lnx_b, w_branch_rwkv, w_branch_sc, conv_sc, w_out, norm2_g, w_up, conv_ffn,
                  w_down, final_norm_g):
    row = lambda t: t.reshape(1, -1).astype(F32)
    d3 = 3 * D_MODEL
    n_lora = D_DECAY_LORA + D_AAA_LORA + D_GATE_LORA
    pad_rows = lambda t, n: jnp.pad(t, [(0, n - t.shape[0]), (0, 0)])
    return {
        "norm1_g": row(norm1_g),
        "w_rkv": w_in[:, :d3].astype(BF16),
        "w_lora": _pad_lora_cols(w_in[:, d3:d3 + n_lora]).astype(BF16),
        "w_sc": w_in[:, d3 + n_lora:2 * d3 + n_lora].astype(BF16),
        "w_gate": w_in[:, 2 * d3 + n_lora:].astype(BF16),
        "b_gate": row(b_gate),
        "mu_rkv": row(mu_shift[:d3]),
        "mu_lora": _pad_lora_cols(row(mu_shift[d3:])),
        "w0": row(w0),
        "wd": pad_rows(w_decay_up, 128).astype(BF16),
        "a0": row(a0),
        "wa": pad_rows(w_aaa_up, 128).astype(BF16),
        "wg": pad_rows(w_gate_up, 256).astype(BF16),
        "k_k": row(k_k), "k_a": row(k_a), "r_k": row(r_k),
        "lnx_g": row(lnx_g), "lnx_b": row(lnx_b),
        "w_branch_rwkv": w_branch_rwkv.astype(BF16),
        "w_branch_sc": w_branch_sc.astype(BF16),
        "conv_sc": conv_sc.astype(F32),
        "w_out": w_out.astype(BF16),
        "norm2_g": row(norm2_g),
        "w_up": w_up.astype(BF16),
        "conv_ffn": conv_ffn.astype(F32),
        "w_down": w_down.astype(BF16),
        "final_norm_g": row(final_norm_g),
    }


def _layer(x, s_wkv, sh_rkv, sh_lora, sc_state, ffn_state, w, *, rows, s, wkv):
    ga, gbo, xb, sc_new = _sc_gate(x, sc_state, w, rows=rows, s=s)
    r, lw, k, v, kk, b, g, bonus, shr, shl = _rwkv_prep(xb, sh_rkv, sh_lora, w, rows=rows, s=s)
    y, s_new = wkv(r, lw, k, v, kk, b, s_wkv)
    out, ffn_new = _post(x, y, bonus, g, ga, gbo, ffn_state, w, rows=rows, s=s)
    return out, s_new, shr, shl, sc_new, ffn_new


def kernel(x_prompt, x_sample, state_wkv, state_shift, state_sc_conv, state_ffn_conv, meta_tokens,
           norm1_g, w_in, b_gate, mu_shift, w0, w_decay_up, a0, w_aaa_up, w_gate_up, k_k, k_a, r_k,
           lnx_g, lnx_b, w_branch_rwkv, w_branch_sc, conv_sc, w_out, norm2_g, w_up, conv_ffn, w_down,
           final_norm_g):
    w = _prep_weights(norm1_g[0], w_in[0], b_gate[0], mu_shift[0], w0[0], w_decay_up[0], a0[0],
                      w_aaa_up[0], w_gate_up[0], k_k[0], k_a[0], r_k[0], lnx_g[0], lnx_b[0],
                      w_branch_rwkv[0], w_branch_sc[0], conv_sc[0], w_out[0], norm2_g[0], w_up[0],
                      conv_ffn[0], w_down[0], final_norm_g)
    d3 = 3 * D_MODEL

    bp, seq, _ = x_prompt.shape
    zeros = lambda *shape: jnp.zeros(shape, F32)
    _, m_wkv, m_shr, m_shl, m_sc, m_ffn = _layer(
        meta_tokens.astype(F32)[None], zeros(1, N_HEADS, HEAD_SIZE, HEAD_SIZE), zeros(1, 1, d3),
        zeros(1, 1, D_LORA_PAD), zeros(1, 2, D_MODEL), zeros(1, 2, 2 * D_FF), w,
        rows=N_META, s=1, wkv=functools.partial(_wkv, chunk=N_META, nq=1))

    rep = lambda t: jnp.broadcast_to(t, (bp,) + t.shape[1:])
    y_prompt, p_wkv, p_shr, p_shl, p_sc, p_ffn = _layer(
        x_prompt, rep(m_wkv), rep(m_shr), rep(m_shl), rep(m_sc), rep(m_ffn), w,
        rows=512, s=1, wkv=functools.partial(_wkv, chunk=64, nq=4))

    bs, ts, _ = x_sample.shape
    n_grp = 2
    gsz = bs // n_grp

    def to_rows(t):
        n, c = t.shape[1:]
        return t.reshape(n_grp, gsz, n, c).transpose(0, 2, 1, 3).reshape(n_grp, n * gsz, c)

    sh = state_shift[0].reshape(n_grp, gsz, -1)
    y_s, s_wkv, s_shr, s_shl, s_sc, s_ffn = _layer(
        to_rows(x_sample), state_wkv[0].transpose(1, 2, 3, 0), sh[..., :d3], _pad_lora_cols(sh[..., d3:]),
        to_rows(state_sc_conv[0]), to_rows(state_ffn_conv[0]), w,
        rows=ts * gsz, s=gsz, wkv=functools.partial(_wkv_lanes, n_t=ts))
    s_wkv = s_wkv.transpose(3, 0, 1, 2)

    def rows_to_batch(t, n):
        c = t.shape[-1]
        return t.reshape(n_grp, n, gsz, c).transpose(0, 2, 1, 3).reshape(bs, n, c)

    y_sample = rows_to_batch(y_s, ts)
    shift_p = jnp.concatenate([p_shr[:, 0], _unpad_lora_cols(p_shl[:, 0])], axis=-1)
    shift_s = jnp.concatenate([rows_to_batch(s_shr, 1)[:, 0], _unpad_lora_cols(rows_to_batch(s_shl, 1)[:, 0])],
                              axis=-1)
    return (y_prompt, y_sample,
            p_wkv[None], s_wkv[None],
            shift_p[None], shift_s[None],
            p_sc[None], rows_to_batch(s_sc, 2)[None],
            p_ffn[None], rows_to_batch(s_ffn, 2)[None])
```

```python
import functools

import jax
import jax.numpy as jnp
from jax import lax
from jax.experimental import pallas as pl
from jax.experimental.pallas import tpu as pltpu

D_MODEL = 1024
N_META = 16
HEAD_SIZE = 64
N_HEADS = D_MODEL // HEAD_SIZE
LANES = 128
N_PAIRS = D_MODEL // LANES
D_DECAY_LORA = 64
D_AAA_LORA = 64
D_GATE_LORA = 160
D_LORA_PAD = 512
D_FF = 2816
FF_CHUNK = 256
COL_CHUNK = 256
CONV_W = 3
NEG_LOG2_E = -1.4426950408889634
EXP_NEG_HALF = 0.6065306597126334
RMS_EPS = 1e-6
GN_EPS = 64e-5
VMEM_LIMIT = 60 * 1024 * 1024

F32 = jnp.float32
BF16 = jnp.bfloat16


def _dot(a, b):
    return jnp.dot(a.astype(BF16), b.astype(BF16), preferred_element_type=F32)


def _dot_nt(a, b):
    return lax.dot_general(a.astype(BF16), b.astype(BF16), (((1,), (1,)), ((), ())),
                           preferred_element_type=F32)


def _dot_tn(a, b):
    return lax.dot_general(a.astype(BF16), b.astype(BF16), (((0,), (0,)), ((), ())),
                           preferred_element_type=F32)


def _head_ones(n):
    r = lax.broadcasted_iota(jnp.int32, (n, n), 0) // HEAD_SIZE
    c = lax.broadcasted_iota(jnp.int32, (n, n), 1) // HEAD_SIZE
    return jnp.where(r == c, 1.0, 0.0).astype(BF16)


def _head_sum(x, ones):
    return jnp.dot(x.astype(BF16), ones, preferred_element_type=F32)


def _rms_norm(x, g):
    return x * lax.rsqrt(jnp.mean(x * x, axis=-1, keepdims=True) + RMS_EPS) * g


def _sigmoid(x):
    return 1.0 / (1.0 + jnp.exp2(x * NEG_LOG2_E))


def _shift_rows(cur, carry, s):
    rows = cur.shape[0]
    if s % 8 == 0:
        return jnp.concatenate([carry, cur[:rows - s]], axis=0)
    assert s == 1
    rolled = pltpu.roll(cur, 1, axis=0)
    row = lax.broadcasted_iota(jnp.int32, cur.shape, 0)
    return jnp.where(row == 0, carry, rolled)


def _rwkv_prep_kernel(xb_ref, sh_rkv_ref, sh_lora_ref, w_rkv_ref, w_lora_ref, mu_rkv_ref,
                      mu_lora_ref, w0_ref, wd_ref, a0_ref, wa_ref, wg_ref, kk_ref, ka_ref, rk_ref,
                      r_out, lw_out, k_out, v_out, kk_out, b_out, g_out, bonus_out, shr_out, shl_out,
                      c_rkv, c_lora, *, s):
    @pl.when(pl.program_id(1) == 0)
    def _():
        c_rkv[...] = sh_rkv_ref[...]
        c_lora[...] = sh_lora_ref[...]

    rows = xb_ref.shape[0]
    xb = xb_ref[...]
    p_lora = jnp.dot(xb, w_lora_ref[...], preferred_element_type=F32)
    prev_lora = _shift_rows(p_lora, c_lora[...], s)
    xl = p_lora + (prev_lora - p_lora) * mu_lora_ref[...]
    new_lora = p_lora[rows - s:]
    c_lora[...] = new_lora
    shl_out[...] = new_lora
    t_xw = jnp.tanh(xl[:, :128]).astype(BF16)
    xa = xl[:, 128:256].astype(BF16)
    s_xg = _sigmoid(xl[:, 256:]).astype(BF16)

    def proj(c):
        return [jnp.dot(xb, w_rkv_ref[:, base + COL_CHUNK * c:base + COL_CHUNK * (c + 1)],
                        preferred_element_type=F32) for base in (0, D_MODEL, 2 * D_MODEL)]

    ones = _head_ones(COL_CHUNK)
    nxt = proj(0)
    for c in range(D_MODEL // COL_CHUNK):
        cur = nxt
        if c + 1 < D_MODEL // COL_CHUNK:
            nxt = proj(c + 1)
        cols = slice(COL_CHUNK * c, COL_CHUNK * (c + 1))
        shifted = []
        for part, p in enumerate(cur):
            pcols = slice(part * D_MODEL + COL_CHUNK * c, part * D_MODEL + COL_CHUNK * (c + 1))
            prev = _shift_rows(p, c_rkv[:, pcols], s)
            shifted.append(p + (prev - p) * mu_rkv_ref[:, pcols])
            new = p[rows - s:]
            c_rkv[:, pcols] = new
            shr_out[:, pcols] = new
        r, k, v = shifted

        zw = w0_ref[:, cols] + jnp.dot(t_xw, wd_ref[:, cols], preferred_element_type=F32)
        lw_out[:, cols] = _sigmoid(zw) * (-EXP_NEG_HALF)
        a = _sigmoid(a0_ref[:, cols] + jnp.dot(xa, wa_ref[:, cols], preferred_element_type=F32))
        g_out[:, cols] = jnp.dot(s_xg, wg_ref[:, cols], preferred_element_type=F32)

        kkr = k * kk_ref[:, cols]
        kk = kkr * lax.rsqrt(jnp.maximum(_head_sum(kkr * kkr, ones), 1e-24))
        k2 = k * (1.0 + (a - 1.0) * ka_ref[:, cols])
        r_out[:, cols] = r
        k_out[:, cols] = k2
        v_out[:, cols] = v
        kk_out[:, cols] = kk
        b_out[:, cols] = kk * a
        bonus_out[:, cols] = _head_sum(r * k2 * rk_ref[:, cols], ones) * v


def _sc_gate_kernel(x_ref, sc_state_ref, n1g_ref, w_sc_ref, w_gate_ref, b_gate_ref, conv_ref,
                    w_bsc_ref, ga_out, gbo_out, xb_out, sc_out, c_sc, *, s):
    @pl.when(pl.program_id(1) == 0)
    def _():
        c_sc[...] = sc_state_ref[...]

    rows = x_ref.shape[0]
    xb = _rms_norm(x_ref[...], n1g_ref[...]).astype(BF16)
    xb_out[...] = xb

    def proj(c):
        cs = lambda base: slice(base + COL_CHUNK * c, base + COL_CHUNK * (c + 1))
        sc = [jnp.dot(xb, w_sc_ref[:, cs(base)], preferred_element_type=F32)
              for base in (0, D_MODEL, 2 * D_MODEL)]
        gate = [jnp.dot(xb, w_gate_ref[:, cs(base)], preferred_element_type=F32) + b_gate_ref[:, cs(base)]
                for base in (0, D_MODEL)]
        return sc + gate

    mid = []
    gb = []
    nxt = proj(0)
    for c in range(D_MODEL // COL_CHUNK):
        cur = nxt
        if c + 1 < D_MODEL // COL_CHUNK:
            nxt = proj(c + 1)
        cols = slice(COL_CHUNK * c, COL_CHUNK * (c + 1))
        h, b_g, c_g, pg_a, pg_b = cur
        u = c_g * h
        carry = c_sc[:, cols]
        prev1 = _shift_rows(u, carry[s:], s)
        prev2 = _shift_rows(prev1, carry[:s], s)
        cw = conv_ref[:, cols]
        conv = prev2 * cw[0:1] + prev1 * cw[1:2] + u * cw[2:3]
        mid.append((b_g * conv).astype(BF16))
        new = u[rows - 2 * s:]
        c_sc[:, cols] = new
        sc_out[:, cols] = new
        ga_out[:, cols] = _sigmoid(pg_a)
        gb.append(_sigmoid(pg_b))
    o_b = jnp.dot(jnp.concatenate(mid, axis=1), w_bsc_ref[...], preferred_element_type=F32)
    for c in range(D_MODEL // COL_CHUNK):
        cols = slice(COL_CHUNK * c, COL_CHUNK * (c + 1))
        gbo_out[:, cols] = gb[c] * o_b[:, cols]


def _wkv_kernel(r_ref, lw_ref, k_ref, v_ref, kk_ref, b_ref, s0_ref, y_out, s_out, s_scr, *, C, nq):
    c_idx = pl.program_id(1)

    @pl.when(c_idx == 0)
    def _():
        zh = jnp.zeros((HEAD_SIZE, HEAD_SIZE), F32)
        for q in range(nq):
            for p in range(N_PAIRS):
                s_scr[q * N_PAIRS + p] = jnp.concatenate(
                    [jnp.concatenate([s0_ref[q, 2 * p], zh], axis=1),
                     jnp.concatenate([zh, s0_ref[q, 2 * p + 1]], axis=1)], axis=0)

    ti = lax.broadcasted_iota(jnp.int32, (C, 2 * C), 0)
    tj = lax.broadcasted_iota(jnp.int32, (C, 2 * C), 1)
    tri2 = jnp.where(jnp.where(tj >= C, tj - C, tj) <= ti, 1.0, 0.0).astype(BF16)
    m0 = lax.broadcasted_iota(jnp.int32, (C, LANES), 1) < HEAD_SIZE
    gi = lax.broadcasted_iota(jnp.int32, (C, 2 * C), 0)
    gj = lax.broadcasted_iota(jnp.int32, (C, 2 * C), 1)
    gjm = jnp.where(gj >= C, gj - C, gj)
    strict = gjm < gi
    incl = gjm <= gi
    left = gj < C
    eye2 = jnp.where(gjm == gi, 1.0, 0.0)
    bi = lax.broadcasted_iota(jnp.int32, (LANES, LANES), 0) // HEAD_SIZE
    bj = lax.broadcasted_iota(jnp.int32, (LANES, LANES), 1) // HEAD_SIZE
    bd = bi == bj

    def swap_halves(x):
        if 2 * C == LANES:
            return pltpu.roll(x, C, axis=1)
        return jnp.concatenate([x[:, C:], x[:, :C]], axis=1)

    a_p, r_p, v_p, bk, w_last = [], [], [], [], []
    for q in range(nq):
        lw = lw_ref[q]
        h1 = lw.astype(BF16)
        h2 = (lw - h1.astype(F32)).astype(BF16)
        cum = jnp.dot(tri2, jnp.concatenate([h1, h2], axis=0), preferred_element_type=F32)
        w_t = jnp.exp(cum)
        w_inv = jnp.exp(-cum)
        a_t = -kk_ref[q] * jnp.exp(cum - lw)
        b_t = b_ref[q] * w_inv
        k_t = k_ref[q] * w_inv
        r_t = r_ref[q] * w_t
        v_q = v_ref[q]
        for p in range(N_PAIRS):
            sl = slice(LANES * p, LANES * (p + 1))
            a_p.append(a_t[:, sl])
            r_p.append(r_t[:, sl])
            v_p.append(v_q[:, sl])
            bk.append(jnp.concatenate([b_t[:, sl], k_t[:, sl]], axis=0))
            w_last.append(w_t[C - 1:C, sl])
    ents = range(nq * N_PAIRS)

    g = [_dot_nt(jnp.concatenate([jnp.where(m0, a_p[e], 0.0), jnp.where(m0, r_p[e], 0.0),
                                  jnp.where(m0, 0.0, a_p[e]), jnp.where(m0, 0.0, r_p[e])], axis=0), bk[e])
         for e in ents]
    s_bd = [s_scr[e] for e in ents]
    s_t = [jnp.transpose(s_bd[e]) for e in ents]
    rs = [_dot(r_p[e], s_t[e]) for e in ents]
    g1a = [swap_halves(g[e][2 * C:3 * C]) for e in ents]
    def block_diag(x):
        return jnp.concatenate([jnp.where(left, x, 0.0), jnp.where(left, 0.0, x)], axis=0)

    xpow = [jnp.where(strict, jnp.where(left, g[e][:C], g1a[e]), 0.0) for e in ents]
    tinv = [eye2 + xpow[e] for e in ents]
    n = 1
    if 2 * n < C:
        xpow = [_dot(xpow[e], block_diag(xpow[e])) for e in ents]
        n *= 2
    while 2 * n < C:
        z = [_dot(jnp.concatenate([xpow[e], tinv[e]], axis=0), block_diag(xpow[e])) for e in ents]
        xpow = [z[e][:C] for e in ents]
        tinv = [tinv[e] + z[e][C:] for e in ents]
        n *= 2
    if C > 2:
        tinv = [tinv[e] + _dot(tinv[e], block_diag(xpow[e])) for e in ents]
    vm0 = [jnp.where(m0, v_p[e], 0.0) for e in ents]
    vm1 = [jnp.where(m0, 0.0, v_p[e]) for e in ents]
    mak = [jnp.where(strict, jnp.where(left, g1a[e], g[e][:C]), 0.0) for e in ents]
    rhs = [_dot(jnp.concatenate([a_p[e], mak[e]], axis=1), jnp.concatenate([s_t[e], vm1[e], vm0[e]], axis=0))
           for e in ents]
    u = [_dot(tinv[e], jnp.concatenate([jnp.where(m0, rhs[e], 0.0), jnp.where(m0, 0.0, rhs[e])], axis=0))
         for e in ents]
    for e in ents:
        q, p = divmod(e, N_PAIRS)
        um0 = jnp.where(m0, u[e], 0.0)
        um1 = jnp.where(m0, 0.0, u[e])
        g0r = jnp.where(incl, g[e][C:2 * C], 0.0)
        g1r = jnp.where(incl, g[e][3 * C:], 0.0)
        if 2 * C == LANES:
            intra = _dot(jnp.concatenate([g0r, g1r], axis=1),
                         jnp.concatenate([um0, vm0[e], um1, vm1[e]], axis=0))
        else:
            intra = (_dot(g0r, jnp.concatenate([um0, vm0[e]], axis=0))
                     + _dot(g1r, jnp.concatenate([um1, vm1[e]], axis=0)))
        y_out[q, :, LANES * p:LANES * (p + 1)] = rs[e] + intra
    for e in ents:
        upd = _dot_tn(jnp.concatenate([u[e], v_p[e]], axis=0), bk[e])
        s_scr[e] = (s_bd[e] + jnp.where(bd, upd, 0.0)) * w_last[e]

    @pl.when(c_idx == pl.num_programs(1) - 1)
    def _():
        for q in range(nq):
            for p in range(N_PAIRS):
                s_fin = s_scr[q * N_PAIRS + p]
                s_out[q, 2 * p] = s_fin[:HEAD_SIZE, :HEAD_SIZE]
                s_out[q, 2 * p + 1] = s_fin[HEAD_SIZE:, HEAD_SIZE:]


def _wkv_lanes_kernel(r_ref, lw_ref, k_ref, v_ref, kk_ref, b_ref, s0_ref, y_out, s_out,
                      r_t, w_t, k_t, v_t, a_t, b_t, y_t, *, n_t, gsz):
    n_grp = r_ref.shape[0]

    def by_channel(ref, t):
        rows = [ref[g, t * gsz:(t + 1) * gsz, :] for g in range(n_grp)]
        return jnp.transpose(jnp.concatenate(rows, axis=0))

    for t in range(n_t):
        r_t[t] = by_channel(r_ref, t)
        w_t[t] = jnp.exp(by_channel(lw_ref, t))
        k_t[t] = by_channel(k_ref, t)
        v_t[t] = by_channel(v_ref, t)
        a_t[t] = -by_channel(kk_ref, t)
        b_t[t] = by_channel(b_ref, t)

    for hh in range(2):
        ch = slice(hh * HEAD_SIZE, (hh + 1) * HEAD_SIZE)

        def row_step(i, carry):
            s = s0_ref[hh, i]
            for t in range(n_t):
                sa = jnp.sum(s * a_t[t, ch, :], axis=0, keepdims=True)
                v_i = v_t[t, pl.ds(hh * HEAD_SIZE + i, 1), :]
                s = s * w_t[t, ch, :] + sa * b_t[t, ch, :] + v_i * k_t[t, ch, :]
                y_t[t, pl.ds(hh * HEAD_SIZE + i, 1), :] = jnp.sum(s * r_t[t, ch, :], axis=0, keepdims=True)
            s_out[hh, i] = s
            return carry

        lax.fori_loop(0, HEAD_SIZE, row_step, 0, unroll=8)

    for t in range(n_t):
        y = jnp.transpose(y_t[t])
        for g in range(n_grp):
            y_out[g, t * gsz:(t + 1) * gsz, :] = y[g * gsz:(g + 1) * gsz]


def _post_kernel(x_ref, y_ref, bonus_ref, g_ref, ga_ref, gbo_ref, ffn_state_ref, lnx_g_ref, lnx_b_ref,
                 w_br_ref, w_out_ref, n2g_ref, w_up_ref, conv_ref, w_down_ref, fng_ref,
                 out_ref, ffn_out, c_ffn, *, s):
    @pl.when(pl.program_id(1) == 0)
    def _():
        c_ffn[...] = ffn_state_ref[...]

    rows = x_ref.shape[0]
    ones = _head_ones(COL_CHUNK)
    z = []
    for c in range(D_MODEL // COL_CHUNK):
        cols = slice(COL_CHUNK * c, COL_CHUNK * (c + 1))
        y = y_ref[:, cols]
        dev = y - _head_sum(y, ones) * (1.0 / HEAD_SIZE)
        var = _head_sum(dev * dev, ones) * (1.0 / HEAD_SIZE)
        yn = dev * lax.rsqrt(var + GN_EPS) * lnx_g_ref[:, cols] + lnx_b_ref[:, cols]
        z.append(((yn + bonus_ref[:, cols]) * g_ref[:, cols]).astype(BF16))
    o_a = jnp.dot(jnp.concatenate(z, axis=1), w_br_ref[...], preferred_element_type=F32)
    x1 = x_ref[...] + _dot(ga_ref[...] * o_a + gbo_ref[...], w_out_ref[...])

    xb = _rms_norm(x1, n2g_ref[...]).astype(BF16)
    n_chunk = D_FF // FF_CHUNK

    def up_proj(j):
        return [jnp.dot(xb, w_up_ref[:, base + FF_CHUNK * j:base + FF_CHUNK * (j + 1)],
                        preferred_element_type=F32) for base in (0, D_FF)]

    hidden = []
    ups = up_proj(0)
    for j in range(n_chunk):
        cur = ups
        if j + 1 < n_chunk:
            ups = up_proj(j + 1)
        halves = []
        for base, up in zip((0, D_FF), cur):
            cols = slice(base + FF_CHUNK * j, base + FF_CHUNK * (j + 1))
            carry = c_ffn[:, cols]
            prev1 = _shift_rows(up, carry[s:], s)
            prev2 = _shift_rows(prev1, carry[:s], s)
            cw = conv_ref[:, cols]
            halves.append(prev2 * cw[0:1] + prev1 * cw[1:2] + up * cw[2:3])
            new = up[rows - 2 * s:]
            c_ffn[:, cols] = new
            ffn_out[:, cols] = new
        gate, val = halves
        hidden.append((gate * _sigmoid(gate) * val).astype(BF16))
    down = jnp.dot(jnp.concatenate(hidden, axis=1), w_down_ref[...], preferred_element_type=F32)
    out_ref[...] = _rms_norm(x1 + down, fng_ref[...])


def _const_spec(arr):
    nd = arr.ndim
    return pl.BlockSpec(arr.shape, lambda b, t: (0,) * nd, pipeline_mode=pl.Buffered(1))


def _row_spec(rows, cols):
    return pl.BlockSpec((None, rows, cols), lambda b, t: (b, t, 0))


def _state_spec(rows, cols):
    return pl.BlockSpec((None, rows, cols), lambda b, t: (b, 0, 0))


def _params():
    return pltpu.CompilerParams(dimension_semantics=("arbitrary", "arbitrary"),
                                vmem_limit_bytes=VMEM_LIMIT)


def _rwkv_prep(xb, sh_rkv, sh_lora, w, *, rows, s):
    nb, total, _ = xb.shape
    grid = (nb, total // rows)
    consts = [w["w_rkv"], w["w_lora"], w["mu_rkv"], w["mu_lora"], w["w0"], w["wd"],
              w["a0"], w["wa"], w["wg"], w["k_k"], w["k_a"], w["r_k"]]
    tok = jax.ShapeDtypeStruct((nb, total, D_MODEL), F32)
    return pl.pallas_call(
        functools.partial(_rwkv_prep_kernel, s=s),
        grid=grid,
        in_specs=[_row_spec(rows, D_MODEL), _state_spec(s, 3 * D_MODEL), _state_spec(s, D_LORA_PAD)]
                 + [_const_spec(c) for c in consts],
        out_specs=[_row_spec(rows, D_MODEL)] * 8
                  + [_state_spec(s, 3 * D_MODEL), _state_spec(s, D_LORA_PAD)],
        out_shape=[tok] * 8 + [jax.ShapeDtypeStruct((nb, s, 3 * D_MODEL), F32),
                               jax.ShapeDtypeStruct((nb, s, D_LORA_PAD), F32)],
        scratch_shapes=[pltpu.VMEM((s, 3 * D_MODEL), F32), pltpu.VMEM((s, D_LORA_PAD), F32)],
        compiler_params=_params(),
        name="rwkv_prep",
    )(xb, sh_rkv, sh_lora, *consts)


def _sc_gate(x, sc_state, w, *, rows, s):
    nb, total, _ = x.shape
    grid = (nb, total // rows)
    consts = [w["norm1_g"], w["w_sc"], w["w_gate"], w["b_gate"], w["conv_sc"], w["w_branch_sc"]]
    tok = jax.ShapeDtypeStruct((nb, total, D_MODEL), F32)
    return pl.pallas_call(
        functools.partial(_sc_gate_kernel, s=s),
        grid=grid,
        in_specs=[_row_spec(rows, D_MODEL), _state_spec(2 * s, D_MODEL)] + [_const_spec(c) for c in consts],
        out_specs=[_row_spec(rows, D_MODEL)] * 3 + [_state_spec(2 * s, D_MODEL)],
        out_shape=[tok] * 2 + [jax.ShapeDtypeStruct((nb, total, D_MODEL), BF16),
                               jax.ShapeDtypeStruct((nb, 2 * s, D_MODEL), F32)],
        scratch_shapes=[pltpu.VMEM((2 * s, D_MODEL), F32)],
        compiler_params=_params(),
        name="sc_gate",
    )(x, sc_state, *consts)


def _wkv(r, lw, k, v, kk, b, s0, *, chunk, nq):
    nb, total, _ = r.shape
    grid = (nb // nq, total // chunk)
    tok_spec = pl.BlockSpec((nq, chunk, D_MODEL), lambda i, c: (i, c, 0))
    st_spec = pl.BlockSpec((nq, N_HEADS, HEAD_SIZE, HEAD_SIZE), lambda i, c: (i, 0, 0, 0))
    return pl.pallas_call(
        functools.partial(_wkv_kernel, C=chunk, nq=nq),
        grid=grid,
        in_specs=[tok_spec] * 6 + [st_spec],
        out_specs=[tok_spec, st_spec],
        out_shape=[jax.ShapeDtypeStruct((nb, total, D_MODEL), F32),
                   jax.ShapeDtypeStruct((nb, N_HEADS, HEAD_SIZE, HEAD_SIZE), F32)],
        scratch_shapes=[pltpu.VMEM((nq * N_PAIRS, LANES, LANES), F32)],
        compiler_params=_params(),
        name="wkv",
    )(r, lw, k, v, kk, b, s0)


def _wkv_lanes(r, lw, k, v, kk, b, s0, *, n_t):
    n_grp, total, _ = r.shape
    batch = s0.shape[-1]
    tok_spec = pl.BlockSpec((n_grp, total, LANES), lambda p: (0, 0, p))
    st_spec = pl.BlockSpec((2, HEAD_SIZE, HEAD_SIZE, batch), lambda p: (p, 0, 0, 0))
    return pl.pallas_call(
        functools.partial(_wkv_lanes_kernel, n_t=n_t, gsz=total // n_t),
        grid=(N_PAIRS,),
        in_specs=[tok_spec] * 6 + [st_spec],
        out_specs=[tok_spec, st_spec],
        out_shape=[jax.ShapeDtypeStruct(r.shape, F32), jax.ShapeDtypeStruct(s0.shape, F32)],
        scratch_shapes=[pltpu.VMEM((n_t, LANES, batch), F32)] * 7,
        compiler_params=pltpu.CompilerParams(dimension_semantics=("arbitrary",),
                                             vmem_limit_bytes=VMEM_LIMIT),
        name="wkv_lanes",
    )(r, lw, k, v, kk, b, s0)


def _post(x, y, bonus, g, ga, gbo, ffn_state, w, *, rows, s):
    nb, total, _ = x.shape
    grid = (nb, total // rows)
    consts = [w["lnx_g"], w["lnx_b"], w["w_branch_rwkv"], w["w_out"], w["norm2_g"], w["w_up"],
              w["conv_ffn"], w["w_down"], w["final_norm_g"]]
    return pl.pallas_call(
        functools.partial(_post_kernel, s=s),
        grid=grid,
        in_specs=[_row_spec(rows, D_MODEL)] * 6 + [_state_spec(2 * s, 2 * D_FF)]
                 + [_const_spec(c) for c in consts],
        out_specs=[_row_spec(rows, D_MODEL), _state_spec(2 * s, 2 * D_FF)],
        out_shape=[jax.ShapeDtypeStruct((nb, total, D_MODEL), F32),
                   jax.ShapeDtypeStruct((nb, 2 * s, 2 * D_FF), F32)],
        scratch_shapes=[pltpu.VMEM((2 * s, 2 * D_FF), F32)],
        compiler_params=_params(),
        name="post",
    )(x, y, bonus, g, ga, gbo, ffn_state, *consts)


def _pad_lora_cols(a):
    pad = lambda t, n: jnp.pad(t, [(0, 0)] * (t.ndim - 1) + [(0, n - t.shape[-1])])
    return jnp.concatenate([pad(a[..., :64], 128), pad(a[..., 64:128], 128), pad(a[..., 128:], 256)], axis=-1)


def _unpad_lora_cols(a):
    return jnp.concatenate([a[..., :64], a[..., 128:192], a[..., 256:256 + D_GATE_LORA]], axis=-1)


def _prep_weights(norm1_g, w_in, b_gate, mu_shift, w0, w_decay_up, a0, w_aaa_up, w_gate_up, k_k, k_a,
                  r_k, lnx_g, lnx_b, w_branch_rwkv, w_branch_sc, conv_sc, w_out, norm2_g, w_up, conv_ffn,
                  w_down, final_norm_g):
    row = lambda t: t.reshape(1, -1).astype(F32)
    d3 = 3 * D_MODEL
    n_lora = D_DECAY_LORA + D_AAA_LORA + D_GATE_LORA
    pad_rows = lambda t, n: jnp.pad(t, [(0, n - t.shape[0]), (0, 0)])
    return {
        "norm1_g": row(norm1_g),
        "w_rkv": w_in[:, :d3].astype(BF16),
        "w_lora": _pad_lora_cols(w_in[:, d3:d3 + n_lora]).astype(BF16),
        "w_sc": w_in[:, d3 + n_lora:2 * d3 + n_lora].astype(BF16),
        "w_gate": w_in[:, 2 * d3 + n_lora:].astype(BF16),
        "b_gate": row(b_gate),
        "mu_rkv": row(mu_shift[:d3]),
        "mu_lora": _pad_lora_cols(row(mu_shift[d3:])),
        "w0": row(w0),
        "wd": pad_rows(w_decay_up, 128).astype(BF16),
        "a0": row(a0),
        "wa": pad_rows(w_aaa_up, 128).astype(BF16),
        "wg": pad_rows(w_gate_up, 256).astype(BF16),
        "k_k": row(k_k), "k_a": row(k_a), "r_k": row(r_k),
        "lnx_g": row(lnx_g), "lnx_b": row(lnx_b),
        "w_branch_rwkv": w_branch_rwkv.astype(BF16),
        "w_branch_sc": w_branch_sc.astype(BF16),
        "conv_sc": conv_sc.astype(F32),
        "w_out": w_out.astype(BF16),
        "norm2_g": row(norm2_g),
        "w_up": w_up.astype(BF16),
        "conv_ffn": conv_ffn.astype(F32),
        "w_down": w_down.astype(BF16),
        "final_norm_g": row(final_norm_g),
    }


def _layer(x, s_wkv, sh_rkv, sh_lora, sc_state, ffn_state, w, *, rows, s, wkv):
    ga, gbo, xb, sc_new = _sc_gate(x, sc_state, w, rows=rows, s=s)
    r, lw, k, v, kk, b, g, bonus, shr, shl = _rwkv_prep(xb, sh_rkv, sh_lora, w, rows=rows, s=s)
    y, s_new = wkv(r, lw, k, v, kk, b, s_wkv)
    out, ffn_new = _post(x, y, bonus, g, ga, gbo, ffn_state, w, rows=rows, s=s)
    return out, s_new, shr, shl, sc_new, ffn_new


def kernel(x_prompt, x_sample, state_wkv, state_shift, state_sc_conv, state_ffn_conv, meta_tokens,
           norm1_g, w_in, b_gate, mu_shift, w0, w_decay_up, a0, w_aaa_up, w_gate_up, k_k, k_a, r_k,
           lnx_g, lnx_b, w_branch_rwkv, w_branch_sc, conv_sc, w_out, norm2_g, w_up, conv_ffn, w_down,
           final_norm_g):
    w = _prep_weights(norm1_g[0], w_in[0], b_gate[0], mu_shift[0], w0[0], w_decay_up[0], a0[0],
                      w_aaa_up[0], w_gate_up[0], k_k[0], k_a[0], r_k[0], lnx_g[0], lnx_b[0],
                      w_branch_rwkv[0], w_branch_sc[0], conv_sc[0], w_out[0], norm2_g[0], w_up[0],
                      conv_ffn[0], w_down[0], final_norm_g)
    d3 = 3 * D_MODEL

    bp, seq, _ = x_prompt.shape
    zeros = lambda *shape: jnp.zeros(shape, F32)
    _, m_wkv, m_shr, m_shl, m_sc, m_ffn = _layer(
        meta_tokens.astype(F32)[None], zeros(1, N_HEADS, HEAD_SIZE, HEAD_SIZE), zeros(1, 1, d3),
        zeros(1, 1, D_LORA_PAD), zeros(1, 2, D_MODEL), zeros(1, 2, 2 * D_FF), w,
        rows=N_META, s=1, wkv=functools.partial(_wkv, chunk=N_META, nq=1))

    rep = lambda t: jnp.broadcast_to(t, (bp,) + t.shape[1:])
    y_prompt, p_wkv, p_shr, p_shl, p_sc, p_ffn = _layer(
        x_prompt, rep(m_wkv), rep(m_shr), rep(m_shl), rep(m_sc), rep(m_ffn), w,
        rows=512, s=1, wkv=functools.partial(_wkv, chunk=64, nq=4))

    bs, ts, _ = x_sample.shape
    n_grp = 2
    gsz = bs // n_grp

    def to_rows(t):
        n, c = t.shape[1:]
        return t.reshape(n_grp, gsz, n, c).transpose(0, 2, 1, 3).reshape(n_grp, n * gsz, c)

    sh = state_shift[0].reshape(n_grp, gsz, -1)
    y_s, s_wkv, s_shr, s_shl, s_sc, s_ffn = _layer(
        to_rows(x_sample), state_wkv[0].transpose(1, 2, 3, 0), sh[..., :d3], _pad_lora_cols(sh[..., d3:]),
        to_rows(state_sc_conv[0]), to_rows(state_ffn_conv[0]), w,
        rows=ts * gsz, s=gsz, wkv=functools.partial(_wkv_lanes, n_t=ts))
    s_wkv = s_wkv.transpose(3, 0, 1, 2)

    def rows_to_batch(t, n):
        c = t.shape[-1]
        return t.reshape(n_grp, n, gsz, c).transpose(0, 2, 1, 3).reshape(bs, n, c)

    y_sample = rows_to_batch(y_s, ts)
    shift_p = jnp.concatenate([p_shr[:, 0], _unpad_lora_cols(p_shl[:, 0])], axis=-1)
    shift_s = jnp.concatenate([rows_to_batch(s_shr, 1)[:, 0], _unpad_lora_cols(rows_to_batch(s_shl, 1)[:, 0])],
                              axis=-1)
    return (y_prompt, y_sample,
            p_wkv[None], s_wkv[None],
            shift_p[None], shift_s[None],
            p_sc[None], rows_to_batch(s_sc, 2)[None],
            p_ffn[None], rows_to_batch(s_ffn, 2)[None])
```

```python
import functools

import jax
import jax.numpy as jnp
from jax import lax
from jax.experimental import pallas as pl
from jax.experimental.pallas import tpu as pltpu

D_MODEL = 1024
N_META = 16
HEAD_SIZE = 64
N_HEADS = D_MODEL // HEAD_SIZE
LANES = 128
N_PAIRS = D_MODEL // LANES
D_DECAY_LORA = 64
D_AAA_LORA = 64
D_GATE_LORA = 160
D_LORA_PAD = 512
D_FF = 2816
FF_CHUNK = 256
COL_CHUNK = 256
CONV_W = 3
NEG_LOG2_E = -1.4426950408889634
EXP_NEG_HALF = 0.6065306597126334
RMS_EPS = 1e-6
GN_EPS = 64e-5
VMEM_LIMIT = 60 * 1024 * 1024

F32 = jnp.float32
BF16 = jnp.bfloat16


def _dot(a, b):
    return jnp.dot(a.astype(BF16), b.astype(BF16), preferred_element_type=F32)


def _dot_nt(a, b):
    return lax.dot_general(a.astype(BF16), b.astype(BF16), (((1,), (1,)), ((), ())),
                           preferred_element_type=F32)


def _dot_tn(a, b):
    return lax.dot_general(a.astype(BF16), b.astype(BF16), (((0,), (0,)), ((), ())),
                           preferred_element_type=F32)


def _head_ones(n):
    r = lax.broadcasted_iota(jnp.int32, (n, n), 0) // HEAD_SIZE
    c = lax.broadcasted_iota(jnp.int32, (n, n), 1) // HEAD_SIZE
    return jnp.where(r == c, 1.0, 0.0).astype(BF16)


def _head_sum(x, ones):
    return jnp.dot(x.astype(BF16), ones, preferred_element_type=F32)


def _rms_norm(x, g):
    return x * lax.rsqrt(jnp.mean(x * x, axis=-1, keepdims=True) + RMS_EPS) * g


def _sigmoid(x):
    return 1.0 / (1.0 + jnp.exp2(x * NEG_LOG2_E))


def _shift_rows(cur, carry, s):
    rows = cur.shape[0]
    if s % 8 == 0:
        return jnp.concatenate([carry, cur[:rows - s]], axis=0)
    assert s == 1
    rolled = pltpu.roll(cur, 1, axis=0)
    row = lax.broadcasted_iota(jnp.int32, cur.shape, 0)
    return jnp.where(row == 0, carry, rolled)


def _cumsum_rows(x):
    rows, cols = x.shape
    row = lax.broadcasted_iota(jnp.int32, x.shape, 0)
    sh = 1
    while sh < rows:
        if sh % 8 == 0:
            shifted = jnp.concatenate([jnp.zeros((sh, cols), x.dtype), x[:rows - sh]], axis=0)
        else:
            shifted = jnp.where(row >= sh, pltpu.roll(x, sh, axis=0), 0.0)
        x = x + shifted
        sh *= 2
    return x


def _rwkv_prep_kernel(xb_ref, sh_rkv_ref, sh_lora_ref, w_rkv_ref, w_lora_ref, mu_rkv_ref,
                      mu_lora_ref, w0_ref, wd_ref, a0_ref, wa_ref, wg_ref, kk_ref, ka_ref, rk_ref,
                      r_out, lw_out, k_out, v_out, kk_out, b_out, g_out, bonus_out, shr_out, shl_out,
                      c_rkv, c_lora, *, s):
    @pl.when(pl.program_id(1) == 0)
    def _():
        c_rkv[...] = sh_rkv_ref[...]
        c_lora[...] = sh_lora_ref[...]

    rows = xb_ref.shape[0]
    xb = xb_ref[...]
    p_lora = jnp.dot(xb, w_lora_ref[...], preferred_element_type=F32)
    prev_lora = _shift_rows(p_lora, c_lora[...], s)
    xl = p_lora + (prev_lora - p_lora) * mu_lora_ref[...]
    new_lora = p_lora[rows - s:]
    c_lora[...] = new_lora
    shl_out[...] = new_lora
    t_xw = jnp.tanh(xl[:, :128]).astype(BF16)
    xa = xl[:, 128:256].astype(BF16)
    s_xg = _sigmoid(xl[:, 256:]).astype(BF16)

    def proj(c):
        return [jnp.dot(xb, w_rkv_ref[:, base + COL_CHUNK * c:base + COL_CHUNK * (c + 1)],
                        preferred_element_type=F32) for base in (0, D_MODEL, 2 * D_MODEL)]

    ones = _head_ones(COL_CHUNK)
    nxt = proj(0)
    for c in range(D_MODEL // COL_CHUNK):
        cur = nxt
        if c + 1 < D_MODEL // COL_CHUNK:
            nxt = proj(c + 1)
        cols = slice(COL_CHUNK * c, COL_CHUNK * (c + 1))
        shifted = []
        for part, p in enumerate(cur):
            pcols = slice(part * D_MODEL + COL_CHUNK * c, part * D_MODEL + COL_CHUNK * (c + 1))
            prev = _shift_rows(p, c_rkv[:, pcols], s)
            shifted.append(p + (prev - p) * mu_rkv_ref[:, pcols])
            new = p[rows - s:]
            c_rkv[:, pcols] = new
            shr_out[:, pcols] = new
        r, k, v = shifted

        zw = w0_ref[:, cols] + jnp.dot(t_xw, wd_ref[:, cols], preferred_element_type=F32)
        lw_out[:, cols] = _sigmoid(zw) * (-EXP_NEG_HALF)
        a = _sigmoid(a0_ref[:, cols] + jnp.dot(xa, wa_ref[:, cols], preferred_element_type=F32))
        g_out[:, cols] = jnp.dot(s_xg, wg_ref[:, cols], preferred_element_type=F32)

        kkr = k * kk_ref[:, cols]
        kk = kkr * lax.rsqrt(jnp.maximum(_head_sum(kkr * kkr, ones), 1e-24))
        k2 = k * (1.0 + (a - 1.0) * ka_ref[:, cols])
        r_out[:, cols] = r
        k_out[:, cols] = k2
        v_out[:, cols] = v
        kk_out[:, cols] = kk
        b_out[:, cols] = kk * a
        bonus_out[:, cols] = _head_sum(r * k2 * rk_ref[:, cols], ones) * v


def _sc_gate_kernel(x_ref, sc_state_ref, n1g_ref, w_sc_ref, w_gate_ref, b_gate_ref, conv_ref,
                    w_bsc_ref, ga_out, gbo_out, xb_out, sc_out, c_sc, *, s):
    @pl.when(pl.program_id(1) == 0)
    def _():
        c_sc[...] = sc_state_ref[...]

    rows = x_ref.shape[0]
    xb = _rms_norm(x_ref[...], n1g_ref[...]).astype(BF16)
    xb_out[...] = xb

    def proj(c):
        cs = lambda base: slice(base + COL_CHUNK * c, base + COL_CHUNK * (c + 1))
        sc = [jnp.dot(xb, w_sc_ref[:, cs(base)], preferred_element_type=F32)
              for base in (0, D_MODEL, 2 * D_MODEL)]
        gate = [jnp.dot(xb, w_gate_ref[:, cs(base)], preferred_element_type=F32) + b_gate_ref[:, cs(base)]
                for base in (0, D_MODEL)]
        return sc + gate

    mid = []
    gb = []
    nxt = proj(0)
    for c in range(D_MODEL // COL_CHUNK):
        cur = nxt
        if c + 1 < D_MODEL // COL_CHUNK:
            nxt = proj(c + 1)
        cols = slice(COL_CHUNK * c, COL_CHUNK * (c + 1))
        h, b_g, c_g, pg_a, pg_b = cur
        u = c_g * h
        carry = c_sc[:, cols]
        prev1 = _shift_rows(u, carry[s:], s)
        prev2 = _shift_rows(prev1, carry[:s], s)
        cw = conv_ref[:, cols]
        conv = prev2 * cw[0:1] + prev1 * cw[1:2] + u * cw[2:3]
        mid.append((b_g * conv).astype(BF16))
        new = u[rows - 2 * s:]
        c_sc[:, cols] = new
        sc_out[:, cols] = new
        ga_out[:, cols] = _sigmoid(pg_a)
        gb.append(_sigmoid(pg_b))
    o_b = jnp.dot(jnp.concatenate(mid, axis=1), w_bsc_ref[...], preferred_element_type=F32)
    for c in range(D_MODEL // COL_CHUNK):
        cols = slice(COL_CHUNK * c, COL_CHUNK * (c + 1))
        gbo_out[:, cols] = gb[c] * o_b[:, cols]


def _wkv_kernel(r_ref, lw_ref, k_ref, v_ref, kk_ref, b_ref, s0_ref, y_out, s_out, s_scr, *, C, nq):
    c_idx = pl.program_id(1)

    @pl.when(c_idx == 0)
    def _():
        zh = jnp.zeros((HEAD_SIZE, HEAD_SIZE), F32)
        for q in range(nq):
            for p in range(N_PAIRS):
                s_scr[q * N_PAIRS + p] = jnp.concatenate(
                    [jnp.concatenate([s0_ref[q, 2 * p], zh], axis=1),
                     jnp.concatenate([zh, s0_ref[q, 2 * p + 1]], axis=1)], axis=0)

    m0 =lax.broadcasted_iota(jnp.int32, (C, LANES), 1) < HEAD_SIZE
    gi = lax.broadcasted_iota(jnp.int32, (C, 2 * C), 0)
    gj = lax.broadcasted_iota(jnp.int32, (C, 2 * C), 1)
    gjm = jnp.where(gj >= C, gj - C, gj)
    strict = gjm < gi
    incl = gjm <= gi
    left = gj < C
    eye2 = jnp.where(gjm == gi, 1.0, 0.0)
    bi = lax.broadcasted_iota(jnp.int32, (LANES, LANES), 0) // HEAD_SIZE
    bj = lax.broadcasted_iota(jnp.int32, (LANES, LANES), 1) // HEAD_SIZE
    bd = bi == bj

    def swap_halves(x):
        if 2 * C == LANES:
            return pltpu.roll(x, C, axis=1)
        return jnp.concatenate([x[:, C:], x[:, :C]], axis=1)

    a_p, r_p, v_p, bk, w_last = [], [], [], [], []
    for q in range(nq):
        lw = lw_ref[q]
        cum = _cumsum_rows(lw)
        w_t = jnp.exp(cum)
        w_inv = jnp.exp(-cum)
        a_t = -kk_ref[q] * jnp.exp(cum - lw)
        b_t = b_ref[q] * w_inv
        k_t = k_ref[q] * w_inv
        r_t = r_ref[q] * w_t
        v_q = v_ref[q]
        for p in range(N_PAIRS):
            sl = slice(LANES * p, LANES * (p + 1))
            a_p.append(a_t[:, sl])
            r_p.append(r_t[:, sl])
            v_p.append(v_q[:, sl])
            bk.append(jnp.concatenate([b_t[:, sl], k_t[:, sl]], axis=0))
            w_last.append(w_t[C - 1:C, sl])
    ents = range(nq * N_PAIRS)

    g = [_dot_nt(jnp.concatenate([jnp.where(m0, a_p[e], 0.0), jnp.where(m0, r_p[e], 0.0),
                                  jnp.where(m0, 0.0, a_p[e]), jnp.where(m0, 0.0, r_p[e])], axis=0), bk[e])
         for e in ents]
    s_bd = [s_scr[e] for e in ents]
    s_t = [jnp.transpose(s_bd[e]) for e in ents]
    g1a = [swap_halves(g[e][2 * C:3 * C]) for e in ents]
    def block_diag(x):
        return jnp.concatenate([jnp.where(left, x, 0.0), jnp.where(left, 0.0, x)], axis=0)

    xpow = [jnp.where(strict, jnp.where(left, g[e][:C], g1a[e]), 0.0) for e in ents]
    tinv = [eye2 + xpow[e] for e in ents]
    n = 1
    if 2 * n < C:
        xpow = [_dot(xpow[e], block_diag(xpow[e])) for e in ents]
        n *= 2
    while 2 * n < C:
        z = [_dot(jnp.concatenate([xpow[e], tinv[e]], axis=0), block_diag(xpow[e])) for e in ents]
        xpow = [z[e][:C] for e in ents]
        tinv = [tinv[e] + z[e][C:] for e in ents]
        n *= 2
    if C > 2:
        tinv = [tinv[e] + _dot(tinv[e], block_diag(xpow[e])) for e in ents]
    vm0 = [jnp.where(m0, v_p[e], 0.0) for e in ents]
    vm1 = [jnp.where(m0, 0.0, v_p[e]) for e in ents]
    mak = [jnp.where(strict, jnp.where(left, g1a[e], g[e][:C]), 0.0) for e in ents]
    zmak = jnp.zeros((C, 2 * C), F32)
    sv = [_dot(jnp.concatenate([jnp.concatenate([a_p[e], mak[e]], axis=1),
                                jnp.concatenate([r_p[e], zmak], axis=1)], axis=0),
               jnp.concatenate([s_t[e], vm1[e], vm0[e]], axis=0)) for e in ents]
    rhs = [sv[e][:C] for e in ents]
    rs = [sv[e][C:] for e in ents]
    u = [_dot(tinv[e], jnp.concatenate([jnp.where(m0, rhs[e], 0.0), jnp.where(m0, 0.0, rhs[e])], axis=0))
         for e in ents]
    for e in ents:
        q, p = divmod(e, N_PAIRS)
        um0 = jnp.where(m0, u[e], 0.0)
        um1 = jnp.where(m0, 0.0, u[e])
        g0r = jnp.where(incl, g[e][C:2 * C], 0.0)
        g1r = jnp.where(incl, g[e][3 * C:], 0.0)
        if 2 * C == LANES:
            intra = _dot(jnp.concatenate([g0r, g1r], axis=1),
                         jnp.concatenate([um0, vm0[e], um1, vm1[e]], axis=0))
        else:
            intra = (_dot(g0r, jnp.concatenate([um0, vm0[e]], axis=0))
                     + _dot(g1r, jnp.concatenate([um1, vm1[e]], axis=0)))
        y_out[q, :, LANES * p:LANES * (p + 1)] = rs[e] + intra
    for e in ents:
        upd = _dot_tn(jnp.concatenate([u[e], v_p[e]], axis=0), bk[e])
        s_scr[e] = (s_bd[e] + jnp.where(bd, upd, 0.0)) * w_last[e]

    @pl.when(c_idx == pl.num_programs(1) - 1)
    def _():
        for q in range(nq):
            for p in range(N_PAIRS):
                s_fin = s_scr[q * N_PAIRS + p]
                s_out[q, 2 * p] = s_fin[:HEAD_SIZE, :HEAD_SIZE]
                s_out[q, 2 * p + 1] = s_fin[HEAD_SIZE:, HEAD_SIZE:]


def _wkv_lanes_kernel(r_ref, lw_ref, k_ref, v_ref, kk_ref, b_ref, s0_ref, y_out, s_out,
                      r_t, w_t, k_t, v_t, a_t, b_t, y_t, *, n_t, gsz):
    n_grp = r_ref.shape[0]

    def by_channel(ref, t):
        rows = [ref[g, t * gsz:(t + 1) * gsz, :] for g in range(n_grp)]
        return jnp.transpose(jnp.concatenate(rows, axis=0))

    for t in range(n_t):
        r_t[t] = by_channel(r_ref, t)
        w_t[t] = jnp.exp(by_channel(lw_ref, t))
        k_t[t] = by_channel(k_ref, t)
        v_t[t] = by_channel(v_ref, t)
        a_t[t] = -by_channel(kk_ref, t)
        b_t[t] = by_channel(b_ref, t)

    for hh in range(2):
        ch = slice(hh * HEAD_SIZE, (hh + 1) * HEAD_SIZE)

        def row_step(i, carry):
            s = s0_ref[hh, i]
            for t in range(n_t):
                sa = jnp.sum(s * a_t[t, ch, :], axis=0, keepdims=True)
                v_i = v_t[t, pl.ds(hh * HEAD_SIZE + i, 1), :]
                s = s * w_t[t, ch, :] + sa * b_t[t, ch, :] + v_i * k_t[t, ch, :]
                y_t[t, pl.ds(hh * HEAD_SIZE + i, 1), :] = jnp.sum(s * r_t[t, ch, :], axis=0, keepdims=True)
            s_out[hh, i] = s
            return carry

        lax.fori_loop(0, HEAD_SIZE, row_step, 0, unroll=8)

    for t in range(n_t):
        y = jnp.transpose(y_t[t])
        for g in range(n_grp):
            y_out[g, t * gsz:(t + 1) * gsz, :] = y[g * gsz:(g + 1) * gsz]


def _post_kernel(x_ref, y_ref, bonus_ref, g_ref, ga_ref, gbo_ref, ffn_state_ref, lnx_g_ref, lnx_b_ref,
                 w_br_ref, w_out_ref, n2g_ref, w_up_ref, conv_ref, w_down_ref, fng_ref,
                 out_ref, ffn_out, c_ffn, *, s):
    @pl.when(pl.program_id(1) == 0)
    def _():
        c_ffn[...] = ffn_state_ref[...]

    rows = x_ref.shape[0]
    ones = _head_ones(COL_CHUNK)
    z = []
    for c in range(D_MODEL // COL_CHUNK):
        cols = slice(COL_CHUNK * c, COL_CHUNK * (c + 1))
        y = y_ref[:, cols]
        dev = y - _head_sum(y, ones) * (1.0 / HEAD_SIZE)
        var = _head_sum(dev * dev, ones) * (1.0 / HEAD_SIZE)
        yn = dev * lax.rsqrt(var + GN_EPS) * lnx_g_ref[:, cols] + lnx_b_ref[:, cols]
        z.append(((yn + bonus_ref[:, cols]) * g_ref[:, cols]).astype(BF16))
    o_a = jnp.dot(jnp.concatenate(z, axis=1), w_br_ref[...], preferred_element_type=F32)
    x1 = x_ref[...] + _dot(ga_ref[...] * o_a + gbo_ref[...], w_out_ref[...])

    xb = _rms_norm(x1, n2g_ref[...]).astype(BF16)
    n_chunk = D_FF // FF_CHUNK

    def up_proj(j):
        return [jnp.dot(xb, w_up_ref[:, base + FF_CHUNK * j:base + FF_CHUNK * (j + 1)],
                        preferred_element_type=F32) for base in (0, D_FF)]

    hidden = []
    ups = up_proj(0)
    for j in range(n_chunk):
        cur = ups
        if j + 1 < n_chunk:
            ups = up_proj(j + 1)
        halves = []
        for base, up in zip((0, D_FF), cur):
            cols = slice(base + FF_CHUNK * j, base + FF_CHUNK * (j + 1))
            carry = c_ffn[:, cols]
            prev1 = _shift_rows(up, carry[s:], s)
            prev2 = _shift_rows(prev1, carry[:s], s)
            cw = conv_ref[:, cols]
            halves.append(prev2 * cw[0:1] + prev1 * cw[1:2] + up * cw[2:3])
            new = up[rows - 2 * s:]
            c_ffn[:, cols] = new
            ffn_out[:, cols] = new
        gate, val = halves
        hidden.append((gate * _sigmoid(gate) * val).astype(BF16))
    down = jnp.dot(jnp.concatenate(hidden, axis=1), w_down_ref[...], preferred_element_type=F32)
    out_ref[...] = _rms_norm(x1 + down, fng_ref[...])


def _const_spec(arr):
    nd = arr.ndim
    return pl.BlockSpec(arr.shape, lambda b, t: (0,) * nd, pipeline_mode=pl.Buffered(1))


def _row_spec(rows, cols):
    return pl.BlockSpec((None, rows, cols), lambda b, t: (b, t, 0))


def _state_spec(rows, cols):
    return pl.BlockSpec((None, rows, cols), lambda b, t: (b, 0, 0))


def _params():
    return pltpu.CompilerParams(dimension_semantics=("arbitrary", "arbitrary"),
                                vmem_limit_bytes=VMEM_LIMIT)


def _rwkv_prep(xb, sh_rkv, sh_lora, w, *, rows, s):
    nb, total, _ = xb.shape
    grid = (nb, total // rows)
    consts = [w["w_rkv"], w["w_lora"], w["mu_rkv"], w["mu_lora"], w["w0"], w["wd"],
              w["a0"], w["wa"], w["wg"], w["k_k"], w["k_a"], w["r_k"]]
    tok = jax.ShapeDtypeStruct((nb, total, D_MODEL), F32)
    return pl.pallas_call(
        functools.partial(_rwkv_prep_kernel, s=s),
        grid=grid,
        in_specs=[_row_spec(rows, D_MODEL), _state_spec(s, 3 * D_MODEL), _state_spec(s, D_LORA_PAD)]
                 + [_const_spec(c) for c in consts],
        out_specs=[_row_spec(rows, D_MODEL)] * 8
                  + [_state_spec(s, 3 * D_MODEL), _state_spec(s, D_LORA_PAD)],
        out_shape=[tok] * 8 + [jax.ShapeDtypeStruct((nb, s, 3 * D_MODEL), F32),
                               jax.ShapeDtypeStruct((nb, s, D_LORA_PAD), F32)],
        scratch_shapes=[pltpu.VMEM((s, 3 * D_MODEL), F32), pltpu.VMEM((s, D_LORA_PAD), F32)],
        compiler_params=_params(),
        name="rwkv_prep",
    )(xb, sh_rkv, sh_lora, *consts)


def _sc_gate(x, sc_state, w, *, rows, s):
    nb, total, _ = x.shape
    grid = (nb, total // rows)
    consts = [w["norm1_g"], w["w_sc"], w["w_gate"], w["b_gate"], w["conv_sc"], w["w_branch_sc"]]
    tok = jax.ShapeDtypeStruct((nb, total, D_MODEL), F32)
    return pl.pallas_call(
        functools.partial(_sc_gate_kernel, s=s),
        grid=grid,
        in_specs=[_row_spec(rows, D_MODEL), _state_spec(2 * s, D_MODEL)] + [_const_spec(c) for c in consts],
        out_specs=[_row_spec(rows, D_MODEL)] * 3 + [_state_spec(2 * s, D_MODEL)],
        out_shape=[tok] * 2 + [jax.ShapeDtypeStruct((nb, total, D_MODEL), BF16),
                               jax.ShapeDtypeStruct((nb, 2 * s, D_MODEL), F32)],
        scratch_shapes=[pltpu.VMEM((2 * s, D_MODEL), F32)],
        compiler_params=_params(),
        name="sc_gate",
    )(x, sc_state, *consts)


def _wkv(r, lw, k, v, kk, b, s0, *, chunk, nq):
    nb, total, _ = r.shape
    grid = (nb // nq, total // chunk)
    tok_spec = pl.BlockSpec((nq, chunk, D_MODEL), lambda i, c: (i, c, 0))
    st_spec = pl.BlockSpec((nq, N_HEADS, HEAD_SIZE, HEAD_SIZE), lambda i, c: (i, 0, 0, 0))
    return pl.pallas_call(
        functools.partial(_wkv_kernel, C=chunk, nq=nq),
        grid=grid,
        in_specs=[tok_spec] * 6 + [st_spec],
        out_specs=[tok_spec, st_spec],
        out_shape=[jax.ShapeDtypeStruct((nb, total, D_MODEL), F32),
                   jax.ShapeDtypeStruct((nb, N_HEADS, HEAD_SIZE, HEAD_SIZE), F32)],
        scratch_shapes=[pltpu.VMEM((nq * N_PAIRS, LANES, LANES), F32)],
        compiler_params=_params(),
        name="wkv",
    )(r, lw, k, v, kk, b, s0)


def _wkv_lanes(r, lw, k, v, kk, b, s0, *, n_t):
    n_grp, total, _ = r.shape
    batch = s0.shape[-1]
    tok_spec = pl.BlockSpec((n_grp, total, LANES), lambda p: (0, 0, p))
    st_spec = pl.BlockSpec((2, HEAD_SIZE, HEAD_SIZE, batch), lambda p: (p, 0, 0, 0))
    return pl.pallas_call(
        functools.partial(_wkv_lanes_kernel, n_t=n_t, gsz=total // n_t),
        grid=(N_PAIRS,),
        in_specs=[tok_spec] * 6 + [st_spec],
        out_specs=[tok_spec, st_spec],
        out_shape=[jax.ShapeDtypeStruct(r.shape, F32), jax.ShapeDtypeStruct(s0.shape, F32)],
        scratch_shapes=[pltpu.VMEM((n_t, LANES, batch), F32)] * 7,
        compiler_params=pltpu.CompilerParams(dimension_semantics=("arbitrary",),
                                             vmem_limit_bytes=VMEM_LIMIT),
        name="wkv_lanes",
    )(r, lw, k, v, kk, b, s0)


def _post(x, y, bonus, g, ga, gbo, ffn_state, w, *, rows, s):
    nb, total, _ = x.shape
    grid = (nb, total // rows)
    consts = [w["lnx_g"], w["lnx_b"], w["w_branch_rwkv"], w["w_out"], w["norm2_g"], w["w_up"],
              w["conv_ffn"], w["w_down"], w["final_norm_g"]]
    return pl.pallas_call(
        functools.partial(_post_kernel, s=s),
        grid=grid,
        in_specs=[_row_spec(rows, D_MODEL)] * 6 + [_state_spec(2 * s, 2 * D_FF)]
                 + [_const_spec(c) for c in consts],
        out_specs=[_row_spec(rows, D_MODEL), _state_spec(2 * s, 2 * D_FF)],
        out_shape=[jax.ShapeDtypeStruct((nb, total, D_MODEL), F32),
                   jax.ShapeDtypeStruct((nb, 2 * s, 2 * D_FF), F32)],
        scratch_shapes=[pltpu.VMEM((2 * s, 2 * D_FF), F32)],
        compiler_params=_params(),
        name="post",
    )(x, y, bonus, g, ga, gbo, ffn_state, *consts)


def _pad_lora_cols(a):
    pad = lambda t, n: jnp.pad(t, [(0, 0)] * (t.ndim - 1) + [(0, n - t.shape[-1])])
    return jnp.concatenate([pad(a[..., :64], 128), pad(a[..., 64:128], 128), pad(a[..., 128:], 256)], axis=-1)


def _unpad_lora_cols(a):
    return jnp.concatenate([a[..., :64], a[..., 128:192], a[..., 256:256 + D_GATE_LORA]], axis=-1)


def _prep_weights(norm1_g, w_in, b_gate, mu_shift, w0, w_decay_up, a0, w_aaa_up, w_gate_up, k_k, k_a,
                  r_k, lnx_g, lnx_b, w_branch_rwkv, w_branch_sc, conv_sc, w_out, norm2_g, w_up, conv_ffn,
                  w_down, final_norm_g):
    row = lambda t: t.reshape(1, -1).astype(F32)
    d3 = 3 * D_MODEL
    n_lora = D_DECAY_LORA + D_AAA_LORA + D_GATE_LORA
    pad_rows = lambda t, n: jnp.pad(t, [(0, n - t.shape[0]), (0, 0)])
    return {
        "norm1_g": row(norm1_g),
        "w_rkv": w_in[:, :d3].astype(BF16),
        "w_lora": _pad_lora_cols(w_in[:, d3:d3 + n_lora]).astype(BF16),
        "w_sc": w_in[:, d3 + n_lora:2 * d3 + n_lora].astype(BF16),
        "w_gate": w_in[:, 2 * d3 + n_lora:].astype(BF16),
        "b_gate": row(b_gate),
        "mu_rkv": row(mu_shift[:d3]),
        "mu_lora": _pad_lora_cols(row(mu_shift[d3:])),
        "w0": row(w0),
        "wd": pad_rows(w_decay_up, 128).astype(BF16),
        "a0": row(a0),
        "wa": pad_rows(w_aaa_up, 128).astype(BF16),
        "wg": pad_rows(w_gate_up, 256).astype(BF16),
        "k_k": row(k_k), "k_a": row(k_a), "r_k": row(r_k),
        "lnx_g": row(lnx_g), "lnx_b": row(lnx_b),
        "w_branch_rwkv": w_branch_rwkv.astype(BF16),
        "w_branch_sc": w_branch_sc.astype(BF16),
        "conv_sc": conv_sc.astype(F32),
        "w_out": w_out.astype(BF16),
        "norm2_g": row(norm2_g),
        "w_up": w_up.astype(BF16),
        "conv_ffn": conv_ffn.astype(F32),
        "w_down": w_down.astype(BF16),
        "final_norm_g": row(final_norm_g),
    }


def _layer(x, s_wkv, sh_rkv, sh_lora, sc_state, ffn_state, w, *, rows, s, wkv):
    ga, gbo, xb, sc_new = _sc_gate(x, sc_state, w, rows=rows, s=s)
    r, lw, k, v, kk, b, g, bonus, shr, shl = _rwkv_prep(xb, sh_rkv, sh_lora, w, rows=rows, s=s)
    y, s_new = wkv(r, lw, k, v, kk, b, s_wkv)
    out, ffn_new = _post(x, y, bonus, g, ga, gbo, ffn_state, w, rows=rows, s=s)
    return out, s_new, shr, shl, sc_new, ffn_new


def kernel(x_prompt, x_sample, state_wkv, state_shift, state_sc_conv, state_ffn_conv, meta_tokens,
           norm1_g, w_in, b_gate, mu_shift, w0, w_decay_up, a0, w_aaa_up, w_gate_up, k_k, k_a, r_k,
           lnx_g, lnx_b, w_branch_rwkv, w_branch_sc, conv_sc, w_out, norm2_g, w_up, conv_ffn, w_down,
           final_norm_g):
    w = _prep_weights(norm1_g[0], w_in[0], b_gate[0], mu_shift[0], w0[0], w_decay_up[0], a0[0],
                      w_aaa_up[0], w_gate_up[0], k_k[0], k_a[0], r_k[0], lnx_g[0], lnx_b[0],
                      w_branch_rwkv[0], w_branch_sc[0], conv_sc[0], w_out[0], norm2_g[0], w_up[0],
                      conv_ffn[0], w_down[0], final_norm_g)
    d3 = 3 * D_MODEL

    bp, seq, _ = x_prompt.shape
    zeros = lambda *shape: jnp.zeros(shape, F32)
    _, m_wkv, m_shr, m_shl, m_sc, m_ffn = _layer(
        meta_tokens.astype(F32)[None], zeros(1, N_HEADS, HEAD_SIZE, HEAD_SIZE), zeros(1, 1, d3),
        zeros(1, 1, D_LORA_PAD), zeros(1, 2, D_MODEL), zeros(1, 2, 2 * D_FF), w,
        rows=N_META, s=1, wkv=functools.partial(_wkv, chunk=N_META, nq=1))

    rep = lambda t: jnp.broadcast_to(t, (bp,) + t.shape[1:])
    y_prompt, p_wkv, p_shr, p_shl, p_sc, p_ffn = _layer(
        x_prompt, rep(m_wkv), rep(m_shr), rep(m_shl), rep(m_sc), rep(m_ffn), w,
        rows=512, s=1, wkv=functools.partial(_wkv, chunk=64, nq=4))

    bs, ts, _ = x_sample.shape
    n_grp = 2
    gsz = bs // n_grp

    def to_rows(t):
        n, c = t.shape[1:]
        return t.reshape(n_grp, gsz, n, c).transpose(0, 2, 1, 3).reshape(n_grp, n * gsz, c)

    sh = state_shift[0].reshape(n_grp, gsz, -1)
    y_s, s_wkv, s_shr, s_shl, s_sc, s_ffn = _layer(
        to_rows(x_sample), state_wkv[0].transpose(1, 2, 3, 0), sh[..., :d3], _pad_lora_cols(sh[..., d3:]),
        to_rows(state_sc_conv[0]), to_rows(state_ffn_conv[0]), w,
        rows=ts * gsz, s=gsz, wkv=functools.partial(_wkv_lanes, n_t=ts))
    s_wkv = s_wkv.transpose(3, 0, 1, 2)

    def rows_to_batch(t, n):
        c = t.shape[-1]
        return t.reshape(n_grp, n, gsz, c).transpose(0, 2, 1, 3).reshape(bs, n, c)

    y_sample = rows_to_batch(y_s, ts)
    shift_p = jnp.concatenate([p_shr[:, 0], _unpad_lora_cols(p_shl[:, 0])], axis=-1)
    shift_s = jnp.concatenate([rows_to_batch(s_shr, 1)[:, 0], _unpad_lora_cols(rows_to_batch(s_shl, 1)[:, 0])],
                              axis=-1)
    return (y_prompt, y_sample,
            p_wkv[None], s_wkv[None],
            shift_p[None], shift_s[None],
            p_sc[None], rows_to_batch(s_sc, 2)[None],
            p_ffn[None], rows_to_batch(s_ffn, 2)[None])
```

```python
import functools

import jax
import jax.numpy as jnp
from jax import lax
from jax.experimental import pallas as pl
from jax.experimental.pallas import tpu as pltpu

D_MODEL = 1024
N_META = 16
HEAD_SIZE = 64
N_HEADS = D_MODEL // HEAD_SIZE
LANES = 128
N_PAIRS = D_MODEL // LANES
D_DECAY_LORA = 64
D_AAA_LORA = 64
D_GATE_LORA = 160
D_LORA_PAD = 512
D_FF = 2816
FF_CHUNK = 256
COL_CHUNK = 256
CONV_W = 3
NEG_LOG2_E = -1.4426950408889634
EXP_NEG_HALF = 0.6065306597126334
RMS_EPS = 1e-6
GN_EPS = 64e-5
VMEM_LIMIT = 60 * 1024 * 1024

F32 = jnp.float32
BF16 = jnp.bfloat16


def _dot(a, b):
    return jnp.dot(a.astype(BF16), b.astype(BF16), preferred_element_type=F32)


def _dot_nt(a, b):
    return lax.dot_general(a.astype(BF16), b.astype(BF16), (((1,), (1,)), ((), ())),
                           preferred_element_type=F32)


def _dot_tn(a, b):
    return lax.dot_general(a.astype(BF16), b.astype(BF16), (((0,), (0,)), ((), ())),
                           preferred_element_type=F32)


def _head_ones(n):
    r = lax.broadcasted_iota(jnp.int32, (n, n), 0) // HEAD_SIZE
    c = lax.broadcasted_iota(jnp.int32, (n, n), 1) // HEAD_SIZE
    return jnp.where(r == c, 1.0, 0.0).astype(BF16)


def _head_sum(x, ones):
    return jnp.dot(x.astype(BF16), ones, preferred_element_type=F32)


def _rms_norm(x, g):
    return x * lax.rsqrt(jnp.mean(x * x, axis=-1, keepdims=True) + RMS_EPS) * g


def _sigmoid(x):
    return 1.0 / (1.0 + jnp.exp2(x * NEG_LOG2_E))


def _shift_rows(cur, carry, s):
    rows = cur.shape[0]
    if s % 8 == 0:
        return jnp.concatenate([carry, cur[:rows - s]], axis=0)
    assert s == 1
    rolled = pltpu.roll(cur, 1, axis=0)
    row = lax.broadcasted_iota(jnp.int32, cur.shape, 0)
    return jnp.where(row == 0, carry, rolled)


def _cumsum_rows(x):
    rows, cols = x.shape
    row = lax.broadcasted_iota(jnp.int32, x.shape, 0)
    sh = 1
    while sh < rows:
        if sh % 8 == 0:
            shifted = jnp.concatenate([jnp.zeros((sh, cols), x.dtype), x[:rows - sh]], axis=0)
        else:
            shifted = jnp.where(row >= sh, pltpu.roll(x, sh, axis=0), 0.0)
        x = x + shifted
        sh *= 2
    return x


def _rwkv_prep_kernel(xb_ref, sh_rkv_ref, sh_lora_ref, w_rkv_ref, w_lora_ref, mu_rkv_ref,
                      mu_lora_ref, w0_ref, wd_ref, a0_ref, wa_ref, wg_ref, kk_ref, ka_ref, rk_ref,
                      r_out, lw_out, k_out, v_out, kk_out, b_out, g_out, bonus_out, shr_out, shl_out,
                      c_rkv, c_lora, *, s):
    @pl.when(pl.program_id(1) == 0)
    def _():
        c_rkv[...] = sh_rkv_ref[...]
        c_lora[...] = sh_lora_ref[...]

    rows = xb_ref.shape[0]
    xb = xb_ref[...]
    p_lora = jnp.dot(xb, w_lora_ref[...], preferred_element_type=F32)
    prev_lora = _shift_rows(p_lora, c_lora[...], s)
    xl = p_lora + (prev_lora - p_lora) * mu_lora_ref[...]
    new_lora = p_lora[rows - s:]
    c_lora[...] = new_lora
    shl_out[...] = new_lora
    t_xw = jnp.tanh(xl[:, :128]).astype(BF16)
    xa = xl[:, 128:256].astype(BF16)
    s_xg = _sigmoid(xl[:, 256:]).astype(BF16)

    def proj(c):
        return [jnp.dot(xb, w_rkv_ref[:, base + COL_CHUNK * c:base + COL_CHUNK * (c + 1)],
                        preferred_element_type=F32) for base in (0, D_MODEL, 2 * D_MODEL)]

    ones = _head_ones(COL_CHUNK)
    nxt = proj(0)
    for c in range(D_MODEL // COL_CHUNK):
        cur = nxt
        if c + 1 < D_MODEL // COL_CHUNK:
            nxt = proj(c + 1)
        cols = slice(COL_CHUNK * c, COL_CHUNK * (c + 1))
        shifted = []
        for part, p in enumerate(cur):
            pcols = slice(part * D_MODEL + COL_CHUNK * c, part * D_MODEL + COL_CHUNK * (c + 1))
            prev = _shift_rows(p, c_rkv[:, pcols], s)
            shifted.append(p + (prev - p) * mu_rkv_ref[:, pcols])
            new = p[rows - s:]
            c_rkv[:, pcols] = new
            shr_out[:, pcols] = new
        r, k, v = shifted

        zw = w0_ref[:, cols] + jnp.dot(t_xw, wd_ref[:, cols], preferred_element_type=F32)
        lw_out[:, cols] = _sigmoid(zw) * (-EXP_NEG_HALF)
        a = _sigmoid(a0_ref[:, cols] + jnp.dot(xa, wa_ref[:, cols], preferred_element_type=F32))
        g_out[:, cols] = jnp.dot(s_xg, wg_ref[:, cols], preferred_element_type=F32)

        kkr = k * kk_ref[:, cols]
        kk = kkr * lax.rsqrt(jnp.maximum(_head_sum(kkr * kkr, ones), 1e-24))
        k2 = k * (1.0 + (a - 1.0) * ka_ref[:, cols])
        r_out[:, cols] = r
        k_out[:, cols] = k2
        v_out[:, cols] = v
        kk_out[:, cols] = kk
        b_out[:, cols] = kk * a
        bonus_out[:, cols] = _head_sum(r * k2 * rk_ref[:, cols], ones) * v


def _sc_gate_kernel(x_ref, sc_state_ref, n1g_ref, w_sc_ref, w_gate_ref, b_gate_ref, conv_ref,
                    w_bsc_ref, ga_out, gbo_out, xb_out, sc_out, c_sc, *, s):
    @pl.when(pl.program_id(1) == 0)
    def _():
        c_sc[...] = sc_state_ref[...]

    rows = x_ref.shape[0]
    xb = _rms_norm(x_ref[...], n1g_ref[...]).astype(BF16)
    xb_out[...] = xb

    def proj(c):
        cs = lambda base: slice(base + COL_CHUNK * c, base + COL_CHUNK * (c + 1))
        sc = [jnp.dot(xb, w_sc_ref[:, cs(base)], preferred_element_type=F32)
              for base in (0, D_MODEL, 2 * D_MODEL)]
        gate = [jnp.dot(xb, w_gate_ref[:, cs(base)], preferred_element_type=F32) + b_gate_ref[:, cs(base)]
                for base in (0, D_MODEL)]
        return sc + gate

    mid = []
    gb = []
    nxt = proj(0)
    for c in range(D_MODEL // COL_CHUNK):
        cur = nxt
        if c + 1 < D_MODEL // COL_CHUNK:
            nxt = proj(c + 1)
        cols = slice(COL_CHUNK * c, COL_CHUNK * (c + 1))
        h, b_g, c_g, pg_a, pg_b = cur
        u = c_g * h
        carry = c_sc[:, cols]
        prev1 = _shift_rows(u, carry[s:], s)
        prev2 = _shift_rows(prev1, carry[:s], s)
        cw = conv_ref[:, cols]
        conv = prev2 * cw[0:1] + prev1 * cw[1:2] + u * cw[2:3]
        mid.append((b_g * conv).astype(BF16))
        new = u[rows - 2 * s:]
        c_sc[:, cols] = new
        sc_out[:, cols] = new
        ga_out[:, cols] = _sigmoid(pg_a)
        gb.append(_sigmoid(pg_b))
    o_b = jnp.dot(jnp.concatenate(mid, axis=1), w_bsc_ref[...], preferred_element_type=F32)
    for c in range(D_MODEL // COL_CHUNK):
        cols = slice(COL_CHUNK * c, COL_CHUNK * (c + 1))
        gbo_out[:, cols] = gb[c] * o_b[:, cols]


def _wkv_kernel(r_ref, lw_ref, k_ref, v_ref, kk_ref, b_ref, s0_ref, y_out, s_out, s_scr, *, C, nq):
    c_idx = pl.program_id(1)

    @pl.when(c_idx == 0)
    def _():
        zh = jnp.zeros((HEAD_SIZE, HEAD_SIZE), F32)
        for q in range(nq):
            for p in range(N_PAIRS):
                s_scr[q * N_PAIRS + p] = jnp.concatenate(
                    [jnp.concatenate([s0_ref[q, 2 * p], zh], axis=1),
                     jnp.concatenate([zh, s0_ref[q, 2 * p + 1]], axis=1)], axis=0)

    m0 =lax.broadcasted_iota(jnp.int32, (C, LANES), 1) < HEAD_SIZE
    gi = lax.broadcasted_iota(jnp.int32, (C, 2 * C), 0)
    gj = lax.broadcasted_iota(jnp.int32, (C, 2 * C), 1)
    gjm = jnp.where(gj >= C, gj - C, gj)
    strict = gjm < gi
    incl = gjm <= gi
    left = gj < C
    eye2 = jnp.where(gjm == gi, 1.0, 0.0)
    bi = lax.broadcasted_iota(jnp.int32, (LANES, LANES), 0) // HEAD_SIZE
    bj = lax.broadcasted_iota(jnp.int32, (LANES, LANES), 1) // HEAD_SIZE
    bd = bi == bj

    def swap_halves(x):
        if 2 * C == LANES:
            return pltpu.roll(x, C, axis=1)
        return jnp.concatenate([x[:, C:], x[:, :C]], axis=1)

    a_p, r_p, v_p, bk, w_last = [], [], [], [], []
    for q in range(nq):
        lw = lw_ref[q]
        cum = _cumsum_rows(lw)
        w_t = jnp.exp(cum)
        w_inv = jnp.exp(-cum)
        a_t = -kk_ref[q] * jnp.exp(cum - lw)
        b_t = b_ref[q] * w_inv
        k_t = k_ref[q] * w_inv
        r_t = r_ref[q] * w_t
        v_q = v_ref[q]
        for p in range(N_PAIRS):
            sl = slice(LANES * p, LANES * (p + 1))
            a_p.append(a_t[:, sl])
            r_p.append(r_t[:, sl])
            v_p.append(v_q[:, sl])
            bk.append(jnp.concatenate([b_t[:, sl], k_t[:, sl]], axis=0))
            w_last.append(w_t[C - 1:C, sl])
    ents = range(nq * N_PAIRS)

    g = [_dot_nt(jnp.concatenate([jnp.where(m0, a_p[e], 0.0), jnp.where(m0, r_p[e], 0.0),
                                  jnp.where(m0, 0.0, a_p[e]), jnp.where(m0, 0.0, r_p[e])], axis=0), bk[e])
         for e in ents]
    s_bd = [s_scr[e] for e in ents]
    s_t = [jnp.transpose(s_bd[e]) for e in ents]
    g1a = [swap_halves(g[e][2 * C:3 * C]) for e in ents]
    def block_diag(x):
        return jnp.concatenate([jnp.where(left, x, 0.0), jnp.where(left, 0.0, x)], axis=0)

    xpow = [jnp.where(strict, jnp.where(left, g[e][:C], g1a[e]), 0.0) for e in ents]
    tinv = [eye2 + xpow[e] for e in ents]
    n = 1
    if 2 * n < C:
        xpow = [_dot(xpow[e], block_diag(xpow[e])) for e in ents]
        n *= 2
    while 2 * n < C:
        z = [_dot(jnp.concatenate([xpow[e], tinv[e]], axis=0), block_diag(xpow[e])) for e in ents]
        xpow = [z[e][:C] for e in ents]
        tinv = [tinv[e] + z[e][C:] for e in ents]
        n *= 2
    if C > 2:
        tinv = [tinv[e] + _dot(tinv[e], block_diag(xpow[e])) for e in ents]
    vm0 = [jnp.where(m0, v_p[e], 0.0) for e in ents]
    vm1 = [jnp.where(m0, 0.0, v_p[e]) for e in ents]
    mak = [jnp.where(strict, jnp.where(left, g1a[e], g[e][:C]), 0.0) for e in ents]
    zmak = jnp.zeros((C, 2 * C), F32)
    sv = [_dot(jnp.concatenate([jnp.concatenate([a_p[e], mak[e]], axis=1),
                                jnp.concatenate([r_p[e], zmak], axis=1)], axis=0),
               jnp.concatenate([s_t[e], vm1[e], vm0[e]], axis=0)) for e in ents]
    rhs = [sv[e][:C] for e in ents]
    rs = [sv[e][C:] for e in ents]
    u = [_dot(tinv[e], jnp.concatenate([jnp.where(m0, rhs[e], 0.0), jnp.where(m0, 0.0, rhs[e])], axis=0))
         for e in ents]
    for e in ents:
        q, p = divmod(e, N_PAIRS)
        um0 = jnp.where(m0, u[e], 0.0)
        um1 = jnp.where(m0, 0.0, u[e])
        g0r = jnp.where(incl, g[e][C:2 * C], 0.0)
        g1r = jnp.where(incl, g[e][3 * C:], 0.0)
        if 2 * C == LANES:
            intra = _dot(jnp.concatenate([g0r, g1r], axis=1),
                         jnp.concatenate([um0, vm0[e], um1, vm1[e]], axis=0))
        else:
            intra = (_dot(g0r, jnp.concatenate([um0, vm0[e]], axis=0))
                     + _dot(g1r, jnp.concatenate([um1, vm1[e]], axis=0)))
        y_out[q, :, LANES * p:LANES * (p + 1)] = rs[e] + intra
    for e in ents:
        upd = _dot_tn(jnp.concatenate([u[e], v_p[e]], axis=0), bk[e])
        s_scr[e] = (s_bd[e] + jnp.where(bd, upd, 0.0)) * w_last[e]

    @pl.when(c_idx == pl.num_programs(1) - 1)
    def _():
        for q in range(nq):
            for p in range(N_PAIRS):
                s_fin = s_scr[q * N_PAIRS + p]
                s_out[q, 2 * p] = s_fin[:HEAD_SIZE, :HEAD_SIZE]
                s_out[q, 2 * p + 1] = s_fin[HEAD_SIZE:, HEAD_SIZE:]


def _wkv_lanes_kernel(r_ref, lw_ref, k_ref, v_ref, kk_ref, b_ref, s0_ref, y_out, s_out,
                      r_t, w_t, k_t, v_t, a_t, b_t, y_t, *, n_t, gsz):
    n_grp = r_ref.shape[0]

    def by_channel(ref, t):
        rows = [ref[g, t * gsz:(t + 1) * gsz, :] for g in range(n_grp)]
        return jnp.transpose(jnp.concatenate(rows, axis=0))

    for t in range(n_t):
        r_t[t] = by_channel(r_ref, t)
        w_t[t] = jnp.exp(by_channel(lw_ref, t))
        k_t[t] = by_channel(k_ref, t)
        v_t[t] = by_channel(v_ref, t)
        a_t[t] = -by_channel(kk_ref, t)
        b_t[t] = by_channel(b_ref, t)

    for hh in range(2):
        ch = slice(hh * HEAD_SIZE, (hh + 1) * HEAD_SIZE)

        def row_step(i, carry):
            s = s0_ref[hh, i]
            for t in range(n_t):
                sa = jnp.sum(s * a_t[t, ch, :], axis=0, keepdims=True)
                v_i = v_t[t, pl.ds(hh * HEAD_SIZE + i, 1), :]
                s = s * w_t[t, ch, :] + sa * b_t[t, ch, :] + v_i * k_t[t, ch, :]
                y_t[t, pl.ds(hh * HEAD_SIZE + i, 1), :] = jnp.sum(s * r_t[t, ch, :], axis=0, keepdims=True)
            s_out[hh, i] = s
            return carry

        lax.fori_loop(0, HEAD_SIZE, row_step, 0, unroll=8)

    for t in range(n_t):
        y = jnp.transpose(y_t[t])
        for g in range(n_grp):
            y_out[g, t * gsz:(t + 1) * gsz, :] = y[g * gsz:(g + 1) * gsz]


def _post_kernel(x_ref, y_ref, bonus_ref, g_ref, ga_ref, gbo_ref, ffn_state_ref, lnx_g_ref, lnx_b_ref,
                 w_br_ref, w_out_ref, n2g_ref, w_up_ref, conv_ref, w_down_ref, fng_ref,
                 out_ref, ffn_out, c_ffn, *, s):
    @pl.when(pl.program_id(1) == 0)
    def _():
        c_ffn[...] = ffn_state_ref[...]

    rows = x_ref.shape[0]
    ones = _head_ones(COL_CHUNK)
    z = []
    for c in range(D_MODEL // COL_CHUNK):
        cols = slice(COL_CHUNK * c, COL_CHUNK * (c + 1))
        y = y_ref[:, cols]
        dev = y - _head_sum(y, ones) * (1.0 / HEAD_SIZE)
        var = _head_sum(dev * dev, ones) * (1.0 / HEAD_SIZE)
        yn = dev * lax.rsqrt(var + GN_EPS) * lnx_g_ref[:, cols] + lnx_b_ref[:, cols]
        z.append(((yn + bonus_ref[:, cols]) * g_ref[:, cols]).astype(BF16))
    o_a = jnp.dot(jnp.concatenate(z, axis=1), w_br_ref[...], preferred_element_type=F32)
    x1 = x_ref[...] + _dot(ga_ref[...] * o_a + gbo_ref[...], w_out_ref[...])

    xb = _rms_norm(x1, n2g_ref[...]).astype(BF16)
    n_chunk = D_FF // FF_CHUNK

    def up_proj(j):
        return [jnp.dot(xb, w_up_ref[:, base + FF_CHUNK * j:base + FF_CHUNK * (j + 1)],
                        preferred_element_type=F32) for base in (0, D_FF)]

    hidden = []
    ups = up_proj(0)
    for j in range(n_chunk):
        cur = ups
        if j + 1 < n_chunk:
            ups = up_proj(j + 1)
        halves = []
        for base, up in zip((0, D_FF), cur):
            cols = slice(base + FF_CHUNK * j, base + FF_CHUNK * (j + 1))
            carry = c_ffn[:, cols]
            prev1 = _shift_rows(up, carry[s:], s)
            prev2 = _shift_rows(prev1, carry[:s], s)
            cw = conv_ref[:, cols]
            halves.append(prev2 * cw[0:1] + prev1 * cw[1:2] + up * cw[2:3])
            new = up[rows - 2 * s:]
            c_ffn[:, cols] = new
            ffn_out[:, cols] = new
        gate, val = halves
        hidden.append((gate * _sigmoid(gate) * val).astype(BF16))
    down = jnp.dot(jnp.concatenate(hidden, axis=1), w_down_ref[...], preferred_element_type=F32)
    out_ref[...] = _rms_norm(x1 + down, fng_ref[...])


def _const_spec(arr):
    nd = arr.ndim
    return pl.BlockSpec(arr.shape, lambda b, t: (0,) * nd, pipeline_mode=pl.Buffered(1))


def _row_spec(rows, cols):
    return pl.BlockSpec((None, rows, cols), lambda b, t: (b, t, 0))


def _state_spec(rows, cols):
    return pl.BlockSpec((None, rows, cols), lambda b, t: (b, 0, 0))


def _params():
    return pltpu.CompilerParams(dimension_semantics=("arbitrary", "arbitrary"),
                                vmem_limit_bytes=VMEM_LIMIT)


def _rwkv_prep(xb, sh_rkv, sh_lora, w, *, rows, s):
    nb, total, _ = xb.shape
    grid = (nb, total // rows)
    consts = [w["w_rkv"], w["w_lora"], w["mu_rkv"], w["mu_lora"], w["w0"], w["wd"],
              w["a0"], w["wa"], w["wg"], w["k_k"], w["k_a"], w["r_k"]]
    tok = jax.ShapeDtypeStruct((nb, total, D_MODEL), F32)
    return pl.pallas_call(
        functools.partial(_rwkv_prep_kernel, s=s),
        grid=grid,
        in_specs=[_row_spec(rows, D_MODEL), _state_spec(s, 3 * D_MODEL), _state_spec(s, D_LORA_PAD)]
                 + [_const_spec(c) for c in consts],
        out_specs=[_row_spec(rows, D_MODEL)] * 8
                  + [_state_spec(s, 3 * D_MODEL), _state_spec(s, D_LORA_PAD)],
        out_shape=[tok] * 8 + [jax.ShapeDtypeStruct((nb, s, 3 * D_MODEL), F32),
                               jax.ShapeDtypeStruct((nb, s, D_LORA_PAD), F32)],
        scratch_shapes=[pltpu.VMEM((s, 3 * D_MODEL), F32), pltpu.VMEM((s, D_LORA_PAD), F32)],
        compiler_params=_params(),
        name="rwkv_prep",
    )(xb, sh_rkv, sh_lora, *consts)


def _sc_gate(x, sc_state, w, *, rows, s):
    nb, total, _ = x.shape
    grid = (nb, total // rows)
    consts = [w["norm1_g"], w["w_sc"], w["w_gate"], w["b_gate"], w["conv_sc"], w["w_branch_sc"]]
    tok = jax.ShapeDtypeStruct((nb, total, D_MODEL), F32)
    return pl.pallas_call(
        functools.partial(_sc_gate_kernel, s=s),
        grid=grid,
        in_specs=[_row_spec(rows, D_MODEL), _state_spec(2 * s, D_MODEL)] + [_const_spec(c) for c in consts],
        out_specs=[_row_spec(rows, D_MODEL)] * 3 + [_state_spec(2 * s, D_MODEL)],
        out_shape=[tok] * 2 + [jax.ShapeDtypeStruct((nb, total, D_MODEL), BF16),
                               jax.ShapeDtypeStruct((nb, 2 * s, D_MODEL), F32)],
        scratch_shapes=[pltpu.VMEM((2 * s, D_MODEL), F32)],
        compiler_params=_params(),
        name="sc_gate",
    )(x, sc_state, *consts)


def _wkv(r, lw, k, v, kk, b, s0, *, chunk, nq):
    nb, total, _ = r.shape
    grid = (nb // nq, total // chunk)
    tok_spec = pl.BlockSpec((nq, chunk, D_MODEL), lambda i, c: (i, c, 0))
    st_spec = pl.BlockSpec((nq, N_HEADS, HEAD_SIZE, HEAD_SIZE), lambda i, c: (i, 0, 0, 0))
    return pl.pallas_call(
        functools.partial(_wkv_kernel, C=chunk, nq=nq),
        grid=grid,
        in_specs=[tok_spec] * 6 + [st_spec],
        out_specs=[tok_spec, st_spec],
        out_shape=[jax.ShapeDtypeStruct((nb, total, D_MODEL), F32),
                   jax.ShapeDtypeStruct((nb, N_HEADS, HEAD_SIZE, HEAD_SIZE), F32)],
        scratch_shapes=[pltpu.VMEM((nq * N_PAIRS, LANES, LANES), F32)],
        compiler_params=_params(),
        name="wkv",
    )(r, lw, k, v, kk, b, s0)


def _wkv_lanes(r, lw, k, v, kk, b, s0, *, n_t):
    n_grp, total, _ = r.shape
    batch = s0.shape[-1]
    tok_spec = pl.BlockSpec((n_grp, total, LANES), lambda p: (0, 0, p))
    st_spec = pl.BlockSpec((2, HEAD_SIZE, HEAD_SIZE, batch), lambda p: (p, 0, 0, 0))
    return pl.pallas_call(
        functools.partial(_wkv_lanes_kernel, n_t=n_t, gsz=total // n_t),
        grid=(N_PAIRS,),
        in_specs=[tok_spec] * 6 + [st_spec],
        out_specs=[tok_spec, st_spec],
        out_shape=[jax.ShapeDtypeStruct(r.shape, F32), jax.ShapeDtypeStruct(s0.shape, F32)],
        scratch_shapes=[pltpu.VMEM((n_t, LANES, batch), F32)] * 7,
        compiler_params=pltpu.CompilerParams(dimension_semantics=("arbitrary",),
                                             vmem_limit_bytes=VMEM_LIMIT),
        name="wkv_lanes",
    )(r, lw, k, v, kk, b, s0)


def _post(x, y, bonus, g, ga, gbo, ffn_state, w, *, rows, s):
    nb, total, _ = x.shape
    grid = (nb, total // rows)
    consts = [w["lnx_g"], w["lnx_b"], w["w_branch_rwkv"], w["w_out"], w["norm2_g"], w["w_up"],
              w["conv_ffn"], w["w_down"], w["final_norm_g"]]
    return pl.pallas_call(
        functools.partial(_post_kernel, s=s),
        grid=grid,
        in_specs=[_row_spec(rows, D_MODEL)] * 6 + [_state_spec(2 * s, 2 * D_FF)]
                 + [_const_spec(c) for c in consts],
        out_specs=[_row_spec(rows, D_MODEL), _state_spec(2 * s, 2 * D_FF)],
        out_shape=[jax.ShapeDtypeStruct((nb, total, D_MODEL), F32),
                   jax.ShapeDtypeStruct((nb, 2 * s, 2 * D_FF), F32)],
        scratch_shapes=[pltpu.VMEM((2 * s, 2 * D_FF), F32)],
        compiler_params=_params(),
        name="post",
    )(x, y, bonus, g, ga, gbo, ffn_state, *consts)


def _pad_lora_cols(a):
    pad = lambda t, n: jnp.pad(t, [(0, 0)] * (t.ndim - 1) + [(0, n - t.shape[-1])])
    return jnp.concatenate([pad(a[..., :64], 128), pad(a[..., 64:128], 128), pad(a[..., 128:], 256)], axis=-1)


def _unpad_lora_cols(a):
    return jnp.concatenate([a[..., :64], a[..., 128:192], a[..., 256:256 + D_GATE_LORA]], axis=-1)


def _prep_weights(norm1_g, w_in, b_gate, mu_shift, w0, w_decay_up, a0, w_aaa_up, w_gate_up, k_k, k_a,
                  r_k, lnx_g, lnx_b, w_branch_rwkv, w_branch_sc, conv_sc, w_out, norm2_g, w_up, conv_ffn,
                  w_down, final_norm_g):
    row = lambda t: t.reshape(1, -1).astype(F32)
    d3 = 3 * D_MODEL
    n_lora = D_DECAY_LORA + D_AAA_LORA + D_GATE_LORA
    pad_rows = lambda t, n: jnp.pad(t, [(0, n - t.shape[0]), (0, 0)])
    return {
        "norm1_g": row(norm1_g),
        "w_rkv": w_in[:, :d3].astype(BF16),
        "w_lora": _pad_lora_cols(w_in[:, d3:d3 + n_lora]).astype(BF16),
        "w_sc": w_in[:, d3 + n_lora:2 * d3 + n_lora].astype(BF16),
        "w_gate": w_in[:, 2 * d3 + n_lora:].astype(BF16),
        "b_gate": row(b_gate),
        "mu_rkv": row(mu_shift[:d3]),
        "mu_lora": _pad_lora_cols(row(mu_shift[d3:])),
        "w0": row(w0),
        "wd": pad_rows(w_decay_up, 128).astype(BF16),
        "a0": row(a0),
        "wa": pad_rows(w_aaa_up, 128).astype(BF16),
        "wg": pad_rows(w_gate_up, 256).astype(BF16),
        "k_k": row(k_k), "k_a": row(k_a), "r_k": row(r_k),
        "lnx_g": row(lnx_g), "lnx_b": row(lnx_b),
        "w_branch_rwkv": w_branch_rwkv.astype(BF16),
        "w_branch_sc": w_branch_sc.astype(BF16),
        "conv_sc": conv_sc.astype(F32),
        "w_out": w_out.astype(BF16),
        "norm2_g": row(norm2_g),
        "w_up": w_up.astype(BF16),
        "conv_ffn": conv_ffn.astype(F32),
        "w_down": w_down.astype(BF16),
        "final_norm_g": row(final_norm_g),
    }


def _layer(x, s_wkv, sh_rkv, sh_lora, sc_state, ffn_state, w, *, rows, s, wkv, sc_rows=None):
    ga, gbo, xb, sc_new = _sc_gate(x, sc_state, w, rows=sc_rows or rows, s=s)
    r, lw, k, v, kk, b, g, bonus, shr, shl = _rwkv_prep(xb, sh_rkv, sh_lora, w, rows=rows, s=s)
    y, s_new = wkv(r, lw, k, v, kk, b, s_wkv)
    out, ffn_new = _post(x, y, bonus, g, ga, gbo, ffn_state, w, rows=rows, s=s)
    return out, s_new, shr, shl, sc_new, ffn_new


def kernel(x_prompt, x_sample, state_wkv, state_shift, state_sc_conv, state_ffn_conv, meta_tokens,
           norm1_g, w_in, b_gate, mu_shift, w0, w_decay_up, a0, w_aaa_up, w_gate_up, k_k, k_a, r_k,
           lnx_g, lnx_b, w_branch_rwkv, w_branch_sc, conv_sc, w_out, norm2_g, w_up, conv_ffn, w_down,
           final_norm_g):
    w = _prep_weights(norm1_g[0], w_in[0], b_gate[0], mu_shift[0], w0[0], w_decay_up[0], a0[0],
                      w_aaa_up[0], w_gate_up[0], k_k[0], k_a[0], r_k[0], lnx_g[0], lnx_b[0],
                      w_branch_rwkv[0], w_branch_sc[0], conv_sc[0], w_out[0], norm2_g[0], w_up[0],
                      conv_ffn[0], w_down[0], final_norm_g)
    d3 = 3 * D_MODEL

    bp, seq, _ = x_prompt.shape
    zeros = lambda *shape: jnp.zeros(shape, F32)
    _, m_wkv, m_shr, m_shl, m_sc, m_ffn = _layer(
        meta_tokens.astype(F32)[None], zeros(1, N_HEADS, HEAD_SIZE, HEAD_SIZE), zeros(1, 1, d3),
        zeros(1, 1, D_LORA_PAD), zeros(1, 2, D_MODEL), zeros(1, 2, 2 * D_FF), w,
        rows=N_META, s=1, wkv=functools.partial(_wkv, chunk=N_META, nq=1))

    rep = lambda t: jnp.broadcast_to(t, (bp,) + t.shape[1:])
    y_prompt, p_wkv, p_shr, p_shl, p_sc, p_ffn = _layer(
        x_prompt, rep(m_wkv), rep(m_shr), rep(m_shl), rep(m_sc), rep(m_ffn), w,
        rows=512, s=1, wkv=functools.partial(_wkv, chunk=64, nq=4), sc_rows=1024)

    bs, ts, _ = x_sample.shape
    n_grp = 2
    gsz = bs // n_grp

    def to_rows(t):
        n, c = t.shape[1:]
        return t.reshape(n_grp, gsz, n, c).transpose(0, 2, 1, 3).reshape(n_grp, n * gsz, c)

    sh = state_shift[0].reshape(n_grp, gsz, -1)
    y_s, s_wkv, s_shr, s_shl, s_sc, s_ffn = _layer(
        to_rows(x_sample), state_wkv[0].transpose(1, 2, 3, 0), sh[..., :d3], _pad_lora_cols(sh[..., d3:]),
        to_rows(state_sc_conv[0]), to_rows(state_ffn_conv[0]), w,
        rows=ts * gsz, s=gsz, wkv=functools.partial(_wkv_lanes, n_t=ts))
    s_wkv = s_wkv.transpose(3, 0, 1, 2)

    def rows_to_batch(t, n):
        c = t.shape[-1]
        return t.reshape(n_grp, n, gsz, c).transpose(0, 2, 1, 3).reshape(bs, n, c)

    y_sample = rows_to_batch(y_s, ts)
    shift_p = jnp.concatenate([p_shr[:, 0], _unpad_lora_cols(p_shl[:, 0])], axis=-1)
    shift_s = jnp.concatenate([rows_to_batch(s_shr, 1)[:, 0], _unpad_lora_cols(rows_to_batch(s_shl, 1)[:, 0])],
                              axis=-1)
    return (y_prompt, y_sample,
            p_wkv[None], s_wkv[None],
            shift_p[None], shift_s[None],
            p_sc[None], rows_to_batch(s_sc, 2)[None],
            p_ffn[None], rows_to_batch(s_ffn, 2)[None])
```

```python
import functools

import jax
import jax.numpy as jnp
from jax import lax
from jax.experimental import pallas as pl
from jax.experimental.pallas import tpu as pltpu

D_MODEL = 1024
N_META = 16
HEAD_SIZE = 64
N_HEADS = D_MODEL // HEAD_SIZE
LANES = 128
N_PAIRS = D_MODEL // LANES
D_DECAY_LORA = 64
D_AAA_LORA = 64
D_GATE_LORA = 160
D_LORA_PAD = 512
D_FF = 2816
FF_CHUNK = 256
COL_CHUNK = 256
CONV_W = 3
NEG_LOG2_E = -1.4426950408889634
EXP_NEG_HALF = 0.6065306597126334
RMS_EPS = 1e-6
GN_EPS = 64e-5
VMEM_LIMIT = 60 * 1024 * 1024

F32 = jnp.float32
BF16 = jnp.bfloat16


def _dot(a, b):
    return jnp.dot(a.astype(BF16), b.astype(BF16), preferred_element_type=F32)


def _dot_nt(a, b):
    return lax.dot_general(a.astype(BF16), b.astype(BF16), (((1,), (1,)), ((), ())),
                           preferred_element_type=F32)


def _dot_tn(a, b):
    return lax.dot_general(a.astype(BF16), b.astype(BF16), (((0,), (0,)), ((), ())),
                           preferred_element_type=F32)


def _head_ones(n):
    r = lax.broadcasted_iota(jnp.int32, (n, n), 0) // HEAD_SIZE
    c = lax.broadcasted_iota(jnp.int32, (n, n), 1) // HEAD_SIZE
    return jnp.where(r == c, 1.0, 0.0).astype(BF16)


def _head_sum(x, ones):
    return jnp.dot(x.astype(BF16), ones, preferred_element_type=F32)


def _rms_norm(x, g):
    return x * lax.rsqrt(jnp.mean(x * x, axis=-1, keepdims=True) + RMS_EPS) * g


def _sigmoid(x):
    return 1.0 / (1.0 + jnp.exp2(x * NEG_LOG2_E))


def _shift_rows(cur, carry, s):
    rows = cur.shape[0]
    if s % 8 == 0:
        return jnp.concatenate([carry, cur[:rows - s]], axis=0)
    assert s == 1
    rolled = pltpu.roll(cur, 1, axis=0)
    row = lax.broadcasted_iota(jnp.int32, cur.shape, 0)
    return jnp.where(row == 0, carry, rolled)


def _cumsum_rows(x):
    rows, cols = x.shape
    row = lax.broadcasted_iota(jnp.int32, x.shape, 0)
    sh = 1
    while sh < rows:
        if sh % 8 == 0:
            shifted = jnp.concatenate([jnp.zeros((sh, cols), x.dtype), x[:rows - sh]], axis=0)
        else:
            shifted = jnp.where(row >= sh, pltpu.roll(x, sh, axis=0), 0.0)
        x = x + shifted
        sh *= 2
    return x


def _rwkv_prep_kernel(xb_ref, sh_rkv_ref, sh_lora_ref, w_rkv_ref, w_lora_ref, mu_rkv_ref,
                      mu_lora_ref, w0_ref, wd_ref, a0_ref, wa_ref, wg_ref, kk_ref, ka_ref, rk_ref,
                      r_out, lw_out, k_out, v_out, kk_out, b_out, g_out, bonus_out, shr_out, shl_out,
                      c_rkv, c_lora, *, s):
    @pl.when(pl.program_id(1) == 0)
    def _():
        c_rkv[...] = sh_rkv_ref[...]
        c_lora[...] = sh_lora_ref[...]

    rows = xb_ref.shape[0]
    xb = xb_ref[...]
    p_lora = jnp.dot(xb, w_lora_ref[...], preferred_element_type=F32)
    prev_lora = _shift_rows(p_lora, c_lora[...], s)
    xl = p_lora + (prev_lora - p_lora) * mu_lora_ref[...]
    new_lora = p_lora[rows - s:]
    c_lora[...] = new_lora
    shl_out[...] = new_lora
    t_xw = jnp.tanh(xl[:, :128]).astype(BF16)
    xa = xl[:, 128:256].astype(BF16)
    s_xg = _sigmoid(xl[:, 256:]).astype(BF16)

    def proj(c):
        return [jnp.dot(xb, w_rkv_ref[:, base + COL_CHUNK * c:base + COL_CHUNK * (c + 1)],
                        preferred_element_type=F32) for base in (0, D_MODEL, 2 * D_MODEL)]

    ones = _head_ones(COL_CHUNK)
    nxt = proj(0)
    for c in range(D_MODEL // COL_CHUNK):
        cur = nxt
        if c + 1 < D_MODEL // COL_CHUNK:
            nxt = proj(c + 1)
        cols = slice(COL_CHUNK * c, COL_CHUNK * (c + 1))
        shifted = []
        for part, p in enumerate(cur):
            pcols = slice(part * D_MODEL + COL_CHUNK * c, part * D_MODEL + COL_CHUNK * (c + 1))
            prev = _shift_rows(p, c_rkv[:, pcols], s)
            shifted.append(p + (prev - p) * mu_rkv_ref[:, pcols])
            new = p[rows - s:]
            c_rkv[:, pcols] = new
            shr_out[:, pcols] = new
        r, k, v = shifted

        zw = w0_ref[:, cols] + jnp.dot(t_xw, wd_ref[:, cols], preferred_element_type=F32)
        lw_out[:, cols] = _sigmoid(zw) * (-EXP_NEG_HALF)
        a = _sigmoid(a0_ref[:, cols] + jnp.dot(xa, wa_ref[:, cols], preferred_element_type=F32))
        g_out[:, cols] = jnp.dot(s_xg, wg_ref[:, cols], preferred_element_type=F32).astype(g_out.dtype)

        kkr = k * kk_ref[:, cols]
        kk = kkr * lax.rsqrt(jnp.maximum(_head_sum(kkr * kkr, ones), 1e-24))
        k2 = k * (1.0 + (a - 1.0) * ka_ref[:, cols])
        r_out[:, cols] = r
        k_out[:, cols] = k2
        v_out[:, cols] = v.astype(v_out.dtype)
        kk_out[:, cols] = kk
        b_out[:, cols] = kk * a
        bonus_out[:, cols] = _head_sum(r * k2 * rk_ref[:, cols], ones) * v


def _sc_gate_kernel(x_ref, sc_state_ref, n1g_ref, w_sc_ref, w_gate_ref, b_gate_ref, conv_ref,
                    w_bsc_ref, ga_out, gbo_out, xb_out, sc_out, c_sc, *, s):
    @pl.when(pl.program_id(1) == 0)
    def _():
        c_sc[...] = sc_state_ref[...]

    rows = x_ref.shape[0]
    xb = _rms_norm(x_ref[...], n1g_ref[...]).astype(BF16)
    xb_out[...] = xb

    def proj(c):
        cs = lambda base: slice(base + COL_CHUNK * c, base + COL_CHUNK * (c + 1))
        sc = [jnp.dot(xb, w_sc_ref[:, cs(base)], preferred_element_type=F32)
              for base in (0, D_MODEL, 2 * D_MODEL)]
        gate = [jnp.dot(xb, w_gate_ref[:, cs(base)], preferred_element_type=F32) + b_gate_ref[:, cs(base)]
                for base in (0, D_MODEL)]
        return sc + gate

    mid = []
    gb = []
    nxt = proj(0)
    for c in range(D_MODEL // COL_CHUNK):
        cur = nxt
        if c + 1 < D_MODEL // COL_CHUNK:
            nxt = proj(c + 1)
        cols = slice(COL_CHUNK * c, COL_CHUNK * (c + 1))
        h, b_g, c_g, pg_a, pg_b = cur
        u = c_g * h
        carry = c_sc[:, cols]
        prev1 = _shift_rows(u, carry[s:], s)
        prev2 = _shift_rows(prev1, carry[:s], s)
        cw = conv_ref[:, cols]
        conv = prev2 * cw[0:1] + prev1 * cw[1:2] + u * cw[2:3]
        mid.append((b_g * conv).astype(BF16))
        new = u[rows - 2 * s:]
        c_sc[:, cols] = new
        sc_out[:, cols] = new
        ga_out[:, cols] = _sigmoid(pg_a)
        gb.append(_sigmoid(pg_b))
    o_b = jnp.dot(jnp.concatenate(mid, axis=1), w_bsc_ref[...], preferred_element_type=F32)
    for c in range(D_MODEL // COL_CHUNK):
        cols = slice(COL_CHUNK * c, COL_CHUNK * (c + 1))
        gbo_out[:, cols] = gb[c] * o_b[:, cols]


def _wkv_kernel(r_ref, lw_ref, k_ref, v_ref, kk_ref, b_ref, s0_ref, y_out, s_out, s_scr, *, C, nq):
    c_idx = pl.program_id(1)

    @pl.when(c_idx == 0)
    def _():
        zh = jnp.zeros((HEAD_SIZE, HEAD_SIZE), F32)
        for q in range(nq):
            for p in range(N_PAIRS):
                s_scr[q * N_PAIRS + p] = jnp.concatenate(
                    [jnp.concatenate([s0_ref[q, 2 * p], zh], axis=1),
                     jnp.concatenate([zh, s0_ref[q, 2 * p + 1]], axis=1)], axis=0)

    m0 =lax.broadcasted_iota(jnp.int32, (C, LANES), 1) < HEAD_SIZE
    gi = lax.broadcasted_iota(jnp.int32, (C, 2 * C), 0)
    gj = lax.broadcasted_iota(jnp.int32, (C, 2 * C), 1)
    gjm = jnp.where(gj >= C, gj - C, gj)
    strict = gjm < gi
    incl = gjm <= gi
    left = gj < C
    eye2 = jnp.where(gjm == gi, 1.0, 0.0)
    bi = lax.broadcasted_iota(jnp.int32, (LANES, LANES), 0) // HEAD_SIZE
    bj = lax.broadcasted_iota(jnp.int32, (LANES, LANES), 1) // HEAD_SIZE
    bd = bi == bj

    def swap_halves(x):
        if 2 * C == LANES:
            return pltpu.roll(x, C, axis=1)
        return jnp.concatenate([x[:, C:], x[:, :C]], axis=1)

    a_p, r_p, v_p, bk, w_last = [], [], [], [], []
    for q in range(nq):
        lw = lw_ref[q]
        cum = _cumsum_rows(lw)
        w_t = jnp.exp(cum)
        w_inv = jnp.exp(-cum)
        a_t = -kk_ref[q] * jnp.exp(cum - lw)
        b_t = b_ref[q] * w_inv
        k_t = k_ref[q] * w_inv
        r_t = r_ref[q] * w_t
        v_q = v_ref[q].astype(F32)
        for p in range(N_PAIRS):
            sl = slice(LANES * p, LANES * (p + 1))
            a_p.append(a_t[:, sl])
            r_p.append(r_t[:, sl])
            v_p.append(v_q[:, sl])
            bk.append(jnp.concatenate([b_t[:, sl], k_t[:, sl]], axis=0))
            w_last.append(w_t[C - 1:C, sl])
    ents = range(nq * N_PAIRS)

    g = [_dot_nt(jnp.concatenate([jnp.where(m0, a_p[e], 0.0), jnp.where(m0, r_p[e], 0.0),
                                  jnp.where(m0, 0.0, a_p[e]), jnp.where(m0, 0.0, r_p[e])], axis=0), bk[e])
         for e in ents]
    s_bd = [s_scr[e] for e in ents]
    s_t = [jnp.transpose(s_bd[e]) for e in ents]
    g1a = [swap_halves(g[e][2 * C:3 * C]) for e in ents]
    def block_diag(x):
        return jnp.concatenate([jnp.where(left, x, 0.0), jnp.where(left, 0.0, x)], axis=0)

    xpow = [jnp.where(strict, jnp.where(left, g[e][:C], g1a[e]), 0.0) for e in ents]
    tinv = [eye2 + xpow[e] for e in ents]
    n = 1
    if 2 * n < C:
        xpow = [_dot(xpow[e], block_diag(xpow[e])) for e in ents]
        n *= 2
    while 2 * n < C:
        z = [_dot(jnp.concatenate([xpow[e], tinv[e]], axis=0), block_diag(xpow[e])) for e in ents]
        xpow = [z[e][:C] for e in ents]
        tinv = [tinv[e] + z[e][C:] for e in ents]
        n *= 2
    if C > 2:
        tinv = [tinv[e] + _dot(tinv[e], block_diag(xpow[e])) for e in ents]
    vm0 = [jnp.where(m0, v_p[e], 0.0) for e in ents]
    vm1 = [jnp.where(m0, 0.0, v_p[e]) for e in ents]
    mak = [jnp.where(strict, jnp.where(left, g1a[e], g[e][:C]), 0.0) for e in ents]
    zmak = jnp.zeros((C, 2 * C), F32)
    sv = [_dot(jnp.concatenate([jnp.concatenate([a_p[e], mak[e]], axis=1),
                                jnp.concatenate([r_p[e], zmak], axis=1)], axis=0),
               jnp.concatenate([s_t[e], vm1[e], vm0[e]], axis=0)) for e in ents]
    rhs = [sv[e][:C] for e in ents]
    rs = [sv[e][C:] for e in ents]
    u = [_dot(tinv[e], jnp.concatenate([jnp.where(m0, rhs[e], 0.0), jnp.where(m0, 0.0, rhs[e])], axis=0))
         for e in ents]
    for e in ents:
        q, p = divmod(e, N_PAIRS)
        um0 = jnp.where(m0, u[e], 0.0)
        um1 = jnp.where(m0, 0.0, u[e])
        g0r = jnp.where(incl, g[e][C:2 * C], 0.0)
        g1r = jnp.where(incl, g[e][3 * C:], 0.0)
        if 2 * C == LANES:
            intra = _dot(jnp.concatenate([g0r, g1r], axis=1),
                         jnp.concatenate([um0, vm0[e], um1, vm1[e]], axis=0))
        else:
            intra = (_dot(g0r, jnp.concatenate([um0, vm0[e]], axis=0))
                     + _dot(g1r, jnp.concatenate([um1, vm1[e]], axis=0)))
        y_out[q, :, LANES * p:LANES * (p + 1)] = rs[e] + intra
    for e in ents:
        upd = _dot_tn(jnp.concatenate([u[e], v_p[e]], axis=0), bk[e])
        s_scr[e] = (s_bd[e] + jnp.where(bd, upd, 0.0)) * w_last[e]

    @pl.when(c_idx == pl.num_programs(1) - 1)
    def _():
        for q in range(nq):
            for p in range(N_PAIRS):
                s_fin = s_scr[q * N_PAIRS + p]
                s_out[q, 2 * p] = s_fin[:HEAD_SIZE, :HEAD_SIZE]
                s_out[q, 2 * p + 1] = s_fin[HEAD_SIZE:, HEAD_SIZE:]


def _wkv_lanes_kernel(r_ref, lw_ref, k_ref, v_ref, kk_ref, b_ref, s0_ref, y_out, s_out,
                      r_t, w_t, k_t, v_t, a_t, b_t, y_t, *, n_t, gsz):
    n_grp = r_ref.shape[0]

    def by_channel(ref, t):
        rows = [ref[g, t * gsz:(t + 1) * gsz, :] for g in range(n_grp)]
        return jnp.transpose(jnp.concatenate(rows, axis=0))

    for t in range(n_t):
        r_t[t] = by_channel(r_ref, t)
        w_t[t] = jnp.exp(by_channel(lw_ref, t))
        k_t[t] = by_channel(k_ref, t)
        v_t[t] = by_channel(v_ref, t)
        a_t[t] = -by_channel(kk_ref, t)
        b_t[t] = by_channel(b_ref, t)

    for hh in range(2):
        ch = slice(hh * HEAD_SIZE, (hh + 1) * HEAD_SIZE)

        def row_step(i, carry):
            s = s0_ref[hh, i]
            for t in range(n_t):
                sa = jnp.sum(s * a_t[t, ch, :], axis=0, keepdims=True)
                v_i = v_t[t, pl.ds(hh * HEAD_SIZE + i, 1), :]
                s = s * w_t[t, ch, :] + sa * b_t[t, ch, :] + v_i * k_t[t, ch, :]
                y_t[t, pl.ds(hh * HEAD_SIZE + i, 1), :] = jnp.sum(s * r_t[t, ch, :], axis=0, keepdims=True)
            s_out[hh, i] = s
            return carry

        lax.fori_loop(0, HEAD_SIZE, row_step, 0, unroll=8)

    for t in range(n_t):
        y = jnp.transpose(y_t[t])
        for g in range(n_grp):
            y_out[g, t * gsz:(t + 1) * gsz, :] = y[g * gsz:(g + 1) * gsz]


def _post_kernel(x_ref, y_ref, bonus_ref, g_ref, ga_ref, gbo_ref, ffn_state_ref, lnx_g_ref, lnx_b_ref,
                 w_br_ref, w_out_ref, n2g_ref, w_up_ref, conv_ref, w_down_ref, fng_ref,
                 out_ref, ffn_out, c_ffn, *, s):
    @pl.when(pl.program_id(1) == 0)
    def _():
        c_ffn[...] = ffn_state_ref[...]

    rows = x_ref.shape[0]
    ones = _head_ones(COL_CHUNK)
    z = []
    for c in range(D_MODEL // COL_CHUNK):
        cols = slice(COL_CHUNK * c, COL_CHUNK * (c + 1))
        y = y_ref[:, cols]
        dev = y - _head_sum(y, ones) * (1.0 / HEAD_SIZE)
        var = _head_sum(dev * dev, ones) * (1.0 / HEAD_SIZE)
        yn = dev * lax.rsqrt(var + GN_EPS) * lnx_g_ref[:, cols] + lnx_b_ref[:, cols]
        z.append(((yn + bonus_ref[:, cols]) * g_ref[:, cols].astype(F32)).astype(BF16))
    o_a = jnp.dot(jnp.concatenate(z, axis=1), w_br_ref[...], preferred_element_type=F32)
    x1 = x_ref[...] + _dot(ga_ref[...] * o_a + gbo_ref[...], w_out_ref[...])

    xb = _rms_norm(x1, n2g_ref[...]).astype(BF16)
    n_chunk = D_FF // FF_CHUNK

    def up_proj(j):
        return [jnp.dot(xb, w_up_ref[:, base + FF_CHUNK * j:base + FF_CHUNK * (j + 1)],
                        preferred_element_type=F32) for base in (0, D_FF)]

    hidden = []
    ups = up_proj(0)
    for j in range(n_chunk):
        cur = ups
        if j + 1 < n_chunk:
            ups = up_proj(j + 1)
        halves = []
        for base, up in zip((0, D_FF), cur):
            cols = slice(base + FF_CHUNK * j, base + FF_CHUNK * (j + 1))
            carry = c_ffn[:, cols]
            prev1 = _shift_rows(up, carry[s:], s)
            prev2 = _shift_rows(prev1, carry[:s], s)
            cw = conv_ref[:, cols]
            halves.append(prev2 * cw[0:1] + prev1 * cw[1:2] + up * cw[2:3])
            new = up[rows - 2 * s:]
            c_ffn[:, cols] = new
            ffn_out[:, cols] = new
        gate, val = halves
        hidden.append((gate * _sigmoid(gate) * val).astype(BF16))
    down = jnp.dot(jnp.concatenate(hidden, axis=1), w_down_ref[...], preferred_element_type=F32)
    out_ref[...] = _rms_norm(x1 + down, fng_ref[...])


def _const_spec(arr):
    nd = arr.ndim
    return pl.BlockSpec(arr.shape, lambda b, t: (0,) * nd, pipeline_mode=pl.Buffered(1))


def _row_spec(rows, cols):
    return pl.BlockSpec((None, rows, cols), lambda b, t: (b, t, 0))


def _state_spec(rows, cols):
    return pl.BlockSpec((None, rows, cols), lambda b, t: (b, 0, 0))


def _params():
    return pltpu.CompilerParams(dimension_semantics=("arbitrary", "arbitrary"),
                                vmem_limit_bytes=VMEM_LIMIT)


def _rwkv_prep(xb, sh_rkv, sh_lora, w, *, rows, s, narrow_vg):
    nb, total, _ = xb.shape
    grid = (nb, total // rows)
    consts = [w["w_rkv"], w["w_lora"], w["mu_rkv"], w["mu_lora"], w["w0"], w["wd"],
              w["a0"], w["wa"], w["wg"], w["k_k"], w["k_a"], w["r_k"]]
    tok = jax.ShapeDtypeStruct((nb, total, D_MODEL), F32)
    vg = jax.ShapeDtypeStruct((nb, total, D_MODEL), BF16 if narrow_vg else F32)
    return pl.pallas_call(
        functools.partial(_rwkv_prep_kernel, s=s),
        grid=grid,
        in_specs=[_row_spec(rows, D_MODEL), _state_spec(s, 3 * D_MODEL), _state_spec(s, D_LORA_PAD)]
                 + [_const_spec(c) for c in consts],
        out_specs=[_row_spec(rows, D_MODEL)] * 8
                  + [_state_spec(s, 3 * D_MODEL), _state_spec(s, D_LORA_PAD)],
        out_shape=[tok, tok, tok, vg, tok, tok, vg, tok] + [jax.ShapeDtypeStruct((nb, s, 3 * D_MODEL), F32),
                               jax.ShapeDtypeStruct((nb, s, D_LORA_PAD), F32)],
        scratch_shapes=[pltpu.VMEM((s, 3 * D_MODEL), F32), pltpu.VMEM((s, D_LORA_PAD), F32)],
        compiler_params=_params(),
        name="rwkv_prep",
    )(xb, sh_rkv, sh_lora, *consts)


def _sc_gate(x, sc_state, w, *, rows, s):
    nb, total, _ = x.shape
    grid = (nb, total // rows)
    consts = [w["norm1_g"], w["w_sc"], w["w_gate"], w["b_gate"], w["conv_sc"], w["w_branch_sc"]]
    tok = jax.ShapeDtypeStruct((nb, total, D_MODEL), F32)
    return pl.pallas_call(
        functools.partial(_sc_gate_kernel, s=s),
        grid=grid,
        in_specs=[_row_spec(rows, D_MODEL), _state_spec(2 * s, D_MODEL)] + [_const_spec(c) for c in consts],
        out_specs=[_row_spec(rows, D_MODEL)] * 3 + [_state_spec(2 * s, D_MODEL)],
        out_shape=[tok] * 2 + [jax.ShapeDtypeStruct((nb, total, D_MODEL), BF16),
                               jax.ShapeDtypeStruct((nb, 2 * s, D_MODEL), F32)],
        scratch_shapes=[pltpu.VMEM((2 * s, D_MODEL), F32)],
        compiler_params=_params(),
        name="sc_gate",
    )(x, sc_state, *consts)


def _wkv(r, lw, k, v, kk, b, s0, *, chunk, nq):
    nb, total, _ = r.shape
    grid = (nb // nq, total // chunk)
    tok_spec = pl.BlockSpec((nq, chunk, D_MODEL), lambda i, c: (i, c, 0))
    st_spec = pl.BlockSpec((nq, N_HEADS, HEAD_SIZE, HEAD_SIZE), lambda i, c: (i, 0, 0, 0))
    return pl.pallas_call(
        functools.partial(_wkv_kernel, C=chunk, nq=nq),
        grid=grid,
        in_specs=[tok_spec] * 6 + [st_spec],
        out_specs=[tok_spec, st_spec],
        out_shape=[jax.ShapeDtypeStruct((nb, total, D_MODEL), F32),
                   jax.ShapeDtypeStruct((nb, N_HEADS, HEAD_SIZE, HEAD_SIZE), F32)],
        scratch_shapes=[pltpu.VMEM((nq * N_PAIRS, LANES, LANES), F32)],
        compiler_params=_params(),
        name="wkv",
    )(r, lw, k, v, kk, b, s0)


def _wkv_lanes(r, lw, k, v, kk, b, s0, *, n_t):
    n_grp, total, _ = r.shape
    batch = s0.shape[-1]
    tok_spec = pl.BlockSpec((n_grp, total, LANES), lambda p: (0, 0, p))
    st_spec = pl.BlockSpec((2, HEAD_SIZE, HEAD_SIZE, batch), lambda p: (p, 0, 0, 0))
    return pl.pallas_call(
        functools.partial(_wkv_lanes_kernel, n_t=n_t, gsz=total // n_t),
        grid=(N_PAIRS,),
        in_specs=[tok_spec] * 6 + [st_spec],
        out_specs=[tok_spec, st_spec],
        out_shape=[jax.ShapeDtypeStruct(r.shape, F32), jax.ShapeDtypeStruct(s0.shape, F32)],
        scratch_shapes=[pltpu.VMEM((n_t, LANES, batch), F32)] * 7,
        compiler_params=pltpu.CompilerParams(dimension_semantics=("arbitrary",),
                                             vmem_limit_bytes=VMEM_LIMIT),
        name="wkv_lanes",
    )(r, lw, k, v, kk, b, s0)


def _post(x, y, bonus, g, ga, gbo, ffn_state, w, *, rows, s):
    nb, total, _ = x.shape
    grid = (nb, total // rows)
    consts = [w["lnx_g"], w["lnx_b"], w["w_branch_rwkv"], w["w_out"], w["norm2_g"], w["w_up"],
              w["conv_ffn"], w["w_down"], w["final_norm_g"]]
    return pl.pallas_call(
        functools.partial(_post_kernel, s=s),
        grid=grid,
        in_specs=[_row_spec(rows, D_MODEL)] * 6 + [_state_spec(2 * s, 2 * D_FF)]
                 + [_const_spec(c) for c in consts],
        out_specs=[_row_spec(rows, D_MODEL), _state_spec(2 * s, 2 * D_FF)],
        out_shape=[jax.ShapeDtypeStruct((nb, total, D_MODEL), F32),
                   jax.ShapeDtypeStruct((nb, 2 * s, 2 * D_FF), F32)],
        scratch_shapes=[pltpu.VMEM((2 * s, 2 * D_FF), F32)],
        compiler_params=_params(),
        name="post",
    )(x, y, bonus, g, ga, gbo, ffn_state, *consts)


def _pad_lora_cols(a):
    pad = lambda t, n: jnp.pad(t, [(0, 0)] * (t.ndim - 1) + [(0, n - t.shape[-1])])
    return jnp.concatenate([pad(a[..., :64], 128), pad(a[..., 64:128], 128), pad(a[..., 128:], 256)], axis=-1)


def _unpad_lora_cols(a):
    return jnp.concatenate([a[..., :64], a[..., 128:192], a[..., 256:256 + D_GATE_LORA]], axis=-1)


def _prep_weights(norm1_g, w_in, b_gate, mu_shift, w0, w_decay_up, a0, w_aaa_up, w_gate_up, k_k, k_a,
                  r_k, lnx_g, lnx_b, w_branch_rwkv, w_branch_sc, conv_sc, w_out, norm2_g, w_up, conv_ffn,
                  w_down, final_norm_g):
    row = lambda t: t.reshape(1, -1).astype(F32)
    d3 = 3 * D_MODEL
    n_lora = D_DECAY_LORA + D_AAA_LORA + D_GATE_LORA
    pad_rows = lambda t, n: jnp.pad(t, [(0, n - t.shape[0]), (0, 0)])
    return {
        "norm1_g": row(norm1_g),
        "w_rkv": w_in[:, :d3].astype(BF16),
        "w_lora": _pad_lora_cols(w_in[:, d3:d3 + n_lora]).astype(BF16),
        "w_sc": w_in[:, d3 + n_lora:2 * d3 + n_lora].astype(BF16),
        "w_gate": w_in[:, 2 * d3 + n_lora:].astype(BF16),
        "b_gate": row(b_gate),
        "mu_rkv": row(mu_shift[:d3]),
        "mu_lora": _pad_lora_cols(row(mu_shift[d3:])),
        "w0": row(w0),
        "wd": pad_rows(w_decay_up, 128).astype(BF16),
        "a0": row(a0),
        "wa": pad_rows(w_aaa_up, 128).astype(BF16),
        "wg": pad_rows(w_gate_up, 256).astype(BF16),
        "k_k": row(k_k), "k_a": row(k_a), "r_k": row(r_k),
        "lnx_g": row(lnx_g), "lnx_b": row(lnx_b),
        "w_branch_rwkv": w_branch_rwkv.astype(BF16),
        "w_branch_sc": w_branch_sc.astype(BF16),
        "conv_sc": conv_sc.astype(F32),
        "w_out": w_out.astype(BF16),
        "norm2_g": row(norm2_g),
        "w_up": w_up.astype(BF16),
        "conv_ffn": conv_ffn.astype(F32),
        "w_down": w_down.astype(BF16),
        "final_norm_g": row(final_norm_g),
    }


def _layer(x, s_wkv, sh_rkv, sh_lora, sc_state, ffn_state, w, *, rows, s, wkv, sc_rows=None, narrow_vg=False):
    ga, gbo, xb, sc_new = _sc_gate(x, sc_state, w, rows=sc_rows or rows, s=s)
    r, lw, k, v, kk, b, g, bonus, shr, shl = _rwkv_prep(xb, sh_rkv, sh_lora, w, rows=rows, s=s, narrow_vg=narrow_vg)
    y, s_new = wkv(r, lw, k, v, kk, b, s_wkv)
    out, ffn_new = _post(x, y, bonus, g, ga, gbo, ffn_state, w, rows=rows, s=s)
    return out, s_new, shr, shl, sc_new, ffn_new


def kernel(x_prompt, x_sample, state_wkv, state_shift, state_sc_conv, state_ffn_conv, meta_tokens,
           norm1_g, w_in, b_gate, mu_shift, w0, w_decay_up, a0, w_aaa_up, w_gate_up, k_k, k_a, r_k,
           lnx_g, lnx_b, w_branch_rwkv, w_branch_sc, conv_sc, w_out, norm2_g, w_up, conv_ffn, w_down,
           final_norm_g):
    w = _prep_weights(norm1_g[0], w_in[0], b_gate[0], mu_shift[0], w0[0], w_decay_up[0], a0[0],
                      w_aaa_up[0], w_gate_up[0], k_k[0], k_a[0], r_k[0], lnx_g[0], lnx_b[0],
                      w_branch_rwkv[0], w_branch_sc[0], conv_sc[0], w_out[0], norm2_g[0], w_up[0],
                      conv_ffn[0], w_down[0], final_norm_g)
    d3 = 3 * D_MODEL

    bp, seq, _ = x_prompt.shape
    zeros = lambda *shape: jnp.zeros(shape, F32)
    _, m_wkv, m_shr, m_shl, m_sc, m_ffn = _layer(
        meta_tokens.astype(F32)[None], zeros(1, N_HEADS, HEAD_SIZE, HEAD_SIZE), zeros(1, 1, d3),
        zeros(1, 1, D_LORA_PAD), zeros(1, 2, D_MODEL), zeros(1, 2, 2 * D_FF), w,
        rows=N_META, s=1, wkv=functools.partial(_wkv, chunk=N_META, nq=1))

    rep = lambda t: jnp.broadcast_to(t, (bp,) + t.shape[1:])
    y_prompt, p_wkv, p_shr, p_shl, p_sc, p_ffn = _layer(
        x_prompt, rep(m_wkv), rep(m_shr), rep(m_shl), rep(m_sc), rep(m_ffn), w,
        rows=512, s=1, wkv=functools.partial(_wkv, chunk=64, nq=4), sc_rows=1024, narrow_vg=True)

    bs, ts, _ = x_sample.shape
    n_grp = 2
    gsz = bs // n_grp

    def to_rows(t):
        n, c = t.shape[1:]
        return t.reshape(n_grp, gsz, n, c).transpose(0, 2, 1, 3).reshape(n_grp, n * gsz, c)

    sh = state_shift[0].reshape(n_grp, gsz, -1)
    y_s, s_wkv, s_shr, s_shl, s_sc, s_ffn = _layer(
        to_rows(x_sample), state_wkv[0].transpose(1, 2, 3, 0), sh[..., :d3], _pad_lora_cols(sh[..., d3:]),
        to_rows(state_sc_conv[0]), to_rows(state_ffn_conv[0]), w,
        rows=ts * gsz, s=gsz, wkv=functools.partial(_wkv_lanes, n_t=ts))
    s_wkv = s_wkv.transpose(3, 0, 1, 2)

    def rows_to_batch(t, n):
        c = t.shape[-1]
        return t.reshape(n_grp, n, gsz, c).transpose(0, 2, 1, 3).reshape(bs, n, c)

    y_sample = rows_to_batch(y_s, ts)
    shift_p = jnp.concatenate([p_shr[:, 0], _unpad_lora_cols(p_shl[:, 0])], axis=-1)
    shift_s = jnp.concatenate([rows_to_batch(s_shr, 1)[:, 0], _unpad_lora_cols(rows_to_batch(s_shl, 1)[:, 0])],
                              axis=-1)
    return (y_prompt, y_sample,
            p_wkv[None], s_wkv[None],
            shift_p[None], shift_s[None],
            p_sc[None], rows_to_batch(s_sc, 2)[None],
            p_ffn[None], rows_to_batch(s_ffn, 2)[None])
```

```python
import functools

import jax
import jax.numpy as jnp
from jax import lax
from jax.experimental import pallas as pl
from jax.experimental.pallas import tpu as pltpu

D_MODEL = 1024
N_META = 16
HEAD_SIZE = 64
N_HEADS = D_MODEL // HEAD_SIZE
LANES = 128
N_PAIRS = D_MODEL // LANES
D_DECAY_LORA = 64
D_AAA_LORA = 64
D_GATE_LORA = 160
D_LORA_PAD = 512
D_FF = 2816
FF_CHUNK = 256
COL_CHUNK = 256
CONV_W = 3
NEG_LOG2_E = -1.4426950408889634
EXP_NEG_HALF = 0.6065306597126334
RMS_EPS = 1e-6
GN_EPS = 64e-5
VMEM_LIMIT = 60 * 1024 * 1024

PROMPT_ROWS = 512
PROMPT_SC_ROWS = 1024
WKV_CHUNK = 64
WKV_SEQS = 4
SAMPLE_GROUPS = 2
LANE_ROW_UNROLL = 8

F32 = jnp.float32
BF16 = jnp.bfloat16


def _dot(a, b):
    return jnp.dot(a.astype(BF16), b.astype(BF16), preferred_element_type=F32)


def _dot_nt(a, b):
    return lax.dot_general(a.astype(BF16), b.astype(BF16), (((1,), (1,)), ((), ())),
                           preferred_element_type=F32)


def _dot_tn(a, b):
    return lax.dot_general(a.astype(BF16), b.astype(BF16), (((0,), (0,)), ((), ())),
                           preferred_element_type=F32)


def _head_ones(n):
    r = lax.broadcasted_iota(jnp.int32, (n, n), 0) // HEAD_SIZE
    c = lax.broadcasted_iota(jnp.int32, (n, n), 1) // HEAD_SIZE
    return jnp.where(r == c, 1.0, 0.0).astype(BF16)


def _head_sum(x, ones):
    return jnp.dot(x.astype(BF16), ones, preferred_element_type=F32)


def _rms_norm(x, g):
    return x * lax.rsqrt(jnp.mean(x * x, axis=-1, keepdims=True) + RMS_EPS) * g


def _sigmoid(x):
    return 1.0 / (1.0 + jnp.exp2(x * NEG_LOG2_E))


def _shift_rows(cur, carry, s):
    rows = cur.shape[0]
    if s % 8 == 0:
        return jnp.concatenate([carry, cur[:rows - s]], axis=0)
    assert s == 1
    rolled = pltpu.roll(cur, 1, axis=0)
    row = lax.broadcasted_iota(jnp.int32, cur.shape, 0)
    return jnp.where(row == 0, carry, rolled)


def _cumsum_rows(x):
    rows, cols = x.shape
    row = lax.broadcasted_iota(jnp.int32, x.shape, 0)
    sh = 1
    while sh < rows:
        if sh % 8 == 0:
            shifted = jnp.concatenate([jnp.zeros((sh, cols), x.dtype), x[:rows - sh]], axis=0)
        else:
            shifted = jnp.where(row >= sh, pltpu.roll(x, sh, axis=0), 0.0)
        x = x + shifted
        sh *= 2
    return x


def _rwkv_prep_kernel(xb_ref, sh_rkv_ref, sh_lora_ref, w_rkv_ref, w_lora_ref, mu_rkv_ref,
                      mu_lora_ref, w0_ref, wd_ref, a0_ref, wa_ref, wg_ref, kk_ref, ka_ref, rk_ref,
                      r_out, lw_out, k_out, v_out, kk_out, b_out, g_out, bonus_out, shr_out, shl_out,
                      c_rkv, c_lora, *, s):
    @pl.when(pl.program_id(1) == 0)
    def _():
        c_rkv[...] = sh_rkv_ref[...]
        c_lora[...] = sh_lora_ref[...]

    rows = xb_ref.shape[0]
    xb = xb_ref[...]
    p_lora = jnp.dot(xb, w_lora_ref[...], preferred_element_type=F32)
    prev_lora = _shift_rows(p_lora, c_lora[...], s)
    xl = p_lora + (prev_lora - p_lora) * mu_lora_ref[...]
    new_lora = p_lora[rows - s:]
    c_lora[...] = new_lora
    shl_out[...] = new_lora
    t_xw = jnp.tanh(xl[:, :128]).astype(BF16)
    xa = xl[:, 128:256].astype(BF16)
    s_xg = _sigmoid(xl[:, 256:]).astype(BF16)

    def proj(c):
        return [jnp.dot(xb, w_rkv_ref[:, base + COL_CHUNK * c:base + COL_CHUNK * (c + 1)],
                        preferred_element_type=F32) for base in (0, D_MODEL, 2 * D_MODEL)]

    ones = _head_ones(COL_CHUNK)
    nxt = proj(0)
    for c in range(D_MODEL // COL_CHUNK):
        cur = nxt
        if c + 1 < D_MODEL // COL_CHUNK:
            nxt = proj(c + 1)
        cols = slice(COL_CHUNK * c, COL_CHUNK * (c + 1))
        shifted = []
        for part, p in enumerate(cur):
            pcols = slice(part * D_MODEL + COL_CHUNK * c, part * D_MODEL + COL_CHUNK * (c + 1))
            prev = _shift_rows(p, c_rkv[:, pcols], s)
            shifted.append(p + (prev - p) * mu_rkv_ref[:, pcols])
            new = p[rows - s:]
            c_rkv[:, pcols] = new
            shr_out[:, pcols] = new
        r, k, v = shifted

        zw = w0_ref[:, cols] + jnp.dot(t_xw, wd_ref[:, cols], preferred_element_type=F32)
        lw_out[:, cols] = _sigmoid(zw) * (-EXP_NEG_HALF)
        a = _sigmoid(a0_ref[:, cols] + jnp.dot(xa, wa_ref[:, cols], preferred_element_type=F32))
        g_out[:, cols] = jnp.dot(s_xg, wg_ref[:, cols], preferred_element_type=F32).astype(g_out.dtype)

        kkr = k * kk_ref[:, cols]
        kk = kkr * lax.rsqrt(jnp.maximum(_head_sum(kkr * kkr, ones), 1e-24))
        k2 = k * (1.0 + (a - 1.0) * ka_ref[:, cols])
        r_out[:, cols] = r
        k_out[:, cols] = k2
        v_out[:, cols] = v.astype(v_out.dtype)
        kk_out[:, cols] = kk
        b_out[:, cols] = kk * a
        bonus_out[:, cols] = _head_sum(r * k2 * rk_ref[:, cols], ones) * v


def _sc_gate_kernel(x_ref, sc_state_ref, n1g_ref, w_sc_ref, w_gate_ref, b_gate_ref, conv_ref,
                    w_bsc_ref, ga_out, gbo_out, xb_out, sc_out, c_sc, *, s):
    @pl.when(pl.program_id(1) == 0)
    def _():
        c_sc[...] = sc_state_ref[...]

    rows = x_ref.shape[0]
    xb = _rms_norm(x_ref[...], n1g_ref[...]).astype(BF16)
    xb_out[...] = xb

    def proj(c):
        cs = lambda base: slice(base + COL_CHUNK * c, base + COL_CHUNK * (c + 1))
        sc = [jnp.dot(xb, w_sc_ref[:, cs(base)], preferred_element_type=F32)
              for base in (0, D_MODEL, 2 * D_MODEL)]
        gate = [jnp.dot(xb, w_gate_ref[:, cs(base)], preferred_element_type=F32) + b_gate_ref[:, cs(base)]
                for base in (0, D_MODEL)]
        return sc + gate

    mid = []
    gb = []
    nxt = proj(0)
    for c in range(D_MODEL // COL_CHUNK):
        cur = nxt
        if c + 1 < D_MODEL // COL_CHUNK:
            nxt = proj(c + 1)
        cols = slice(COL_CHUNK * c, COL_CHUNK * (c + 1))
        h, b_g, c_g, pg_a, pg_b = cur
        u = c_g * h
        carry = c_sc[:, cols]
        prev1 = _shift_rows(u, carry[s:], s)
        prev2 = _shift_rows(prev1, carry[:s], s)
        cw = conv_ref[:, cols]
        conv = prev2 * cw[0:1] + prev1 * cw[1:2] + u * cw[2:3]
        mid.append((b_g * conv).astype(BF16))
        new = u[rows - 2 * s:]
        c_sc[:, cols] = new
        sc_out[:, cols] = new
        ga_out[:, cols] = _sigmoid(pg_a)
        gb.append(_sigmoid(pg_b))
    o_b = jnp.dot(jnp.concatenate(mid, axis=1), w_bsc_ref[...], preferred_element_type=F32)
    for c in range(D_MODEL // COL_CHUNK):
        cols = slice(COL_CHUNK * c, COL_CHUNK * (c + 1))
        gbo_out[:, cols] = gb[c] * o_b[:, cols]


def _wkv_kernel(r_ref, lw_ref, k_ref, v_ref, kk_ref, b_ref, s0_ref, y_out, s_out, s_scr, *, C, nq):
    c_idx = pl.program_id(1)

    @pl.when(c_idx == 0)
    def _():
        zh = jnp.zeros((HEAD_SIZE, HEAD_SIZE), F32)
        for q in range(nq):
            for p in range(N_PAIRS):
                s_scr[q * N_PAIRS + p] = jnp.concatenate(
                    [jnp.concatenate([s0_ref[q, 2 * p], zh], axis=1),
                     jnp.concatenate([zh, s0_ref[q, 2 * p + 1]], axis=1)], axis=0)

    m0 =lax.broadcasted_iota(jnp.int32, (C, LANES), 1) < HEAD_SIZE
    gi = lax.broadcasted_iota(jnp.int32, (C, 2 * C), 0)
    gj = lax.broadcasted_iota(jnp.int32, (C, 2 * C), 1)
    gjm = jnp.where(gj >= C, gj - C, gj)
    strict = gjm < gi
    incl = gjm <= gi
    left = gj < C
    eye2 = jnp.where(gjm == gi, 1.0, 0.0)
    bi = lax.broadcasted_iota(jnp.int32, (LANES, LANES), 0) // HEAD_SIZE
    bj = lax.broadcasted_iota(jnp.int32, (LANES, LANES), 1) // HEAD_SIZE
    bd = bi == bj

    def swap_halves(x):
        if 2 * C == LANES:
            return pltpu.roll(x, C, axis=1)
        return jnp.concatenate([x[:, C:], x[:, :C]], axis=1)

    a_p, r_p, v_p, bk, w_last = [], [], [], [], []
    for q in range(nq):
        lw = lw_ref[q]
        cum = _cumsum_rows(lw)
        w_t = jnp.exp(cum)
        w_inv = jnp.exp(-cum)
        a_t = -kk_ref[q] * jnp.exp(cum - lw)
        b_t = b_ref[q] * w_inv
        k_t = k_ref[q] * w_inv
        r_t = r_ref[q] * w_t
        v_q = v_ref[q].astype(F32)
        for p in range(N_PAIRS):
            sl = slice(LANES * p, LANES * (p + 1))
            a_p.append(a_t[:, sl])
            r_p.append(r_t[:, sl])
            v_p.append(v_q[:, sl])
            bk.append(jnp.concatenate([b_t[:, sl], k_t[:, sl]], axis=0))
            w_last.append(w_t[C - 1:C, sl])
    ents = range(nq * N_PAIRS)

    g = [_dot_nt(jnp.concatenate([jnp.where(m0, a_p[e], 0.0), jnp.where(m0, r_p[e], 0.0),
                                  jnp.where(m0, 0.0, a_p[e]), jnp.where(m0, 0.0, r_p[e])], axis=0), bk[e])
         for e in ents]
    s_bd = [s_scr[e] for e in ents]
    s_t = [jnp.transpose(s_bd[e]) for e in ents]
    g1a = [swap_halves(g[e][2 * C:3 * C]) for e in ents]
    def block_diag(x):
        return jnp.concatenate([jnp.where(left, x, 0.0), jnp.where(left, 0.0, x)], axis=0)

    xpow = [jnp.where(strict, jnp.where(left, g[e][:C], g1a[e]), 0.0) for e in ents]
    tinv = [eye2 + xpow[e] for e in ents]
    n = 1
    if 2 * n < C:
        xpow = [_dot(xpow[e], block_diag(xpow[e])) for e in ents]
        n *= 2
    while 2 * n < C:
        z = [_dot(jnp.concatenate([xpow[e], tinv[e]], axis=0), block_diag(xpow[e])) for e in ents]
        xpow = [z[e][:C] for e in ents]
        tinv = [tinv[e] + z[e][C:] for e in ents]
        n *= 2
    if C > 2:
        tinv = [tinv[e] + _dot(tinv[e], block_diag(xpow[e])) for e in ents]
    vm0 = [jnp.where(m0, v_p[e], 0.0) for e in ents]
    vm1 = [jnp.where(m0, 0.0, v_p[e]) for e in ents]
    mak = [jnp.where(strict, jnp.where(left, g1a[e], g[e][:C]), 0.0) for e in ents]
    zmak = jnp.zeros((C, 2 * C), F32)
    sv = [_dot(jnp.concatenate([jnp.concatenate([a_p[e], mak[e]], axis=1),
                                jnp.concatenate([r_p[e], zmak], axis=1)], axis=0),
               jnp.concatenate([s_t[e], vm1[e], vm0[e]], axis=0)) for e in ents]
    rhs = [sv[e][:C] for e in ents]
    rs = [sv[e][C:] for e in ents]
    u = [_dot(tinv[e], jnp.concatenate([jnp.where(m0, rhs[e], 0.0), jnp.where(m0, 0.0, rhs[e])], axis=0))
         for e in ents]
    for e in ents:
        q, p = divmod(e, N_PAIRS)
        um0 = jnp.where(m0, u[e], 0.0)
        um1 = jnp.where(m0, 0.0, u[e])
        g0r = jnp.where(incl, g[e][C:2 * C], 0.0)
        g1r = jnp.where(incl, g[e][3 * C:], 0.0)
        if 2 * C == LANES:
            intra = _dot(jnp.concatenate([g0r, g1r], axis=1),
                         jnp.concatenate([um0, vm0[e], um1, vm1[e]], axis=0))
        else:
            intra = (_dot(g0r, jnp.concatenate([um0, vm0[e]], axis=0))
                     + _dot(g1r, jnp.concatenate([um1, vm1[e]], axis=0)))
        y_out[q, :, LANES * p:LANES * (p + 1)] = rs[e] + intra
    for e in ents:
        upd = _dot_tn(jnp.concatenate([u[e], v_p[e]], axis=0), bk[e])
        s_scr[e] = (s_bd[e] + jnp.where(bd, upd, 0.0)) * w_last[e]

    @pl.when(c_idx == pl.num_programs(1) - 1)
    def _():
        for q in range(nq):
            for p in range(N_PAIRS):
                s_fin = s_scr[q * N_PAIRS + p]
                s_out[q, 2 * p] = s_fin[:HEAD_SIZE, :HEAD_SIZE]
                s_out[q, 2 * p + 1] = s_fin[HEAD_SIZE:, HEAD_SIZE:]


def _wkv_lanes_kernel(r_ref, lw_ref, k_ref, v_ref, kk_ref, b_ref, s0_ref, y_out, s_out,
                      r_t, w_t, k_t, v_t, a_t, b_t, y_t, *, n_t, gsz):
    n_grp = r_ref.shape[0]

    def by_channel(ref, t):
        rows = [ref[g, t * gsz:(t + 1) * gsz, :] for g in range(n_grp)]
        return jnp.transpose(jnp.concatenate(rows, axis=0))

    for t in range(n_t):
        r_t[t] = by_channel(r_ref, t)
        w_t[t] = jnp.exp(by_channel(lw_ref, t))
        k_t[t] = by_channel(k_ref, t)
        v_t[t] = by_channel(v_ref, t)
        a_t[t] = -by_channel(kk_ref, t)
        b_t[t] = by_channel(b_ref, t)

    for hh in range(2):
        ch = slice(hh * HEAD_SIZE, (hh + 1) * HEAD_SIZE)

        def row_step(i, carry):
            s = s0_ref[hh, i]
            for t in range(n_t):
                sa = jnp.sum(s * a_t[t, ch, :], axis=0, keepdims=True)
                v_i = v_t[t, pl.ds(hh * HEAD_SIZE + i, 1), :]
                s = s * w_t[t, ch, :] + sa * b_t[t, ch, :] + v_i * k_t[t, ch, :]
                y_t[t, pl.ds(hh * HEAD_SIZE + i, 1), :] = jnp.sum(s * r_t[t, ch, :], axis=0, keepdims=True)
            s_out[hh, i] = s
            return carry

        lax.fori_loop(0, HEAD_SIZE, row_step, 0, unroll=LANE_ROW_UNROLL)

    for t in range(n_t):
        y = jnp.transpose(y_t[t])
        for g in range(n_grp):
            y_out[g, t * gsz:(t + 1) * gsz, :] = y[g * gsz:(g + 1) * gsz]


def _post_kernel(x_ref, y_ref, bonus_ref, g_ref, ga_ref, gbo_ref, ffn_state_ref, lnx_g_ref, lnx_b_ref,
                 w_br_ref, w_out_ref, n2g_ref, w_up_ref, conv_ref, w_down_ref, fng_ref,
                 out_ref, ffn_out, c_ffn, *, s):
    @pl.when(pl.program_id(1) == 0)
    def _():
        c_ffn[...] = ffn_state_ref[...]

    rows = x_ref.shape[0]
    ones = _head_ones(COL_CHUNK)
    z = []
    for c in range(D_MODEL // COL_CHUNK):
        cols = slice(COL_CHUNK * c, COL_CHUNK * (c + 1))
        y = y_ref[:, cols]
        dev = y - _head_sum(y, ones) * (1.0 / HEAD_SIZE)
        var = _head_sum(dev * dev, ones) * (1.0 / HEAD_SIZE)
        yn = dev * lax.rsqrt(var + GN_EPS) * lnx_g_ref[:, cols] + lnx_b_ref[:, cols]
        z.append(((yn + bonus_ref[:, cols]) * g_ref[:, cols].astype(F32)).astype(BF16))
    o_a = jnp.dot(jnp.concatenate(z, axis=1), w_br_ref[...], preferred_element_type=F32)
    x1 = x_ref[...] + _dot(ga_ref[...] * o_a + gbo_ref[...], w_out_ref[...])

    xb = _rms_norm(x1, n2g_ref[...]).astype(BF16)
    n_chunk = D_FF // FF_CHUNK

    def up_proj(j):
        return [jnp.dot(xb, w_up_ref[:, base + FF_CHUNK * j:base + FF_CHUNK * (j + 1)],
                        preferred_element_type=F32) for base in (0, D_FF)]

    hidden = []
    ups = up_proj(0)
    for j in range(n_chunk):
        cur = ups
        if j + 1 < n_chunk:
            ups = up_proj(j + 1)
        halves = []
        for base, up in zip((0, D_FF), cur):
            cols = slice(base + FF_CHUNK * j, base + FF_CHUNK * (j + 1))
            carry = c_ffn[:, cols]
            prev1 = _shift_rows(up, carry[s:], s)
            prev2 = _shift_rows(prev1, carry[:s], s)
            cw = conv_ref[:, cols]
            halves.append(prev2 * cw[0:1] + prev1 * cw[1:2] + up * cw[2:3])
            new = up[rows - 2 * s:]
            c_ffn[:, cols] = new
            ffn_out[:, cols] = new
        gate, val = halves
        hidden.append((gate * _sigmoid(gate) * val).astype(BF16))
    down = jnp.dot(jnp.concatenate(hidden, axis=1), w_down_ref[...], preferred_element_type=F32)
    out_ref[...] = _rms_norm(x1 + down, fng_ref[...])


def _const_spec(arr):
    nd = arr.ndim
    return pl.BlockSpec(arr.shape, lambda b, t: (0,) * nd, pipeline_mode=pl.Buffered(1))


def _row_spec(rows, cols):
    return pl.BlockSpec((None, rows, cols), lambda b, t: (b, t, 0))


def _state_spec(rows, cols):
    return pl.BlockSpec((None, rows, cols), lambda b, t: (b, 0, 0))


def _params():
    return pltpu.CompilerParams(dimension_semantics=("arbitrary", "arbitrary"),
                                vmem_limit_bytes=VMEM_LIMIT)


def _rwkv_prep(xb, sh_rkv, sh_lora, w, *, rows, s, narrow_vg):
    nb, total, _ = xb.shape
    grid = (nb, total // rows)
    consts = [w["w_rkv"], w["w_lora"], w["mu_rkv"], w["mu_lora"], w["w0"], w["wd"],
              w["a0"], w["wa"], w["wg"], w["k_k"], w["k_a"], w["r_k"]]
    tok = jax.ShapeDtypeStruct((nb, total, D_MODEL), F32)
    vg = jax.ShapeDtypeStruct((nb, total, D_MODEL), BF16 if narrow_vg else F32)
    return pl.pallas_call(
        functools.partial(_rwkv_prep_kernel, s=s),
        grid=grid,
        in_specs=[_row_spec(rows, D_MODEL), _state_spec(s, 3 * D_MODEL), _state_spec(s, D_LORA_PAD)]
                 + [_const_spec(c) for c in consts],
        out_specs=[_row_spec(rows, D_MODEL)] * 8
                  + [_state_spec(s, 3 * D_MODEL), _state_spec(s, D_LORA_PAD)],
        out_shape=[tok, tok, tok, vg, tok, tok, vg, tok] + [jax.ShapeDtypeStruct((nb, s, 3 * D_MODEL), F32),
                               jax.ShapeDtypeStruct((nb, s, D_LORA_PAD), F32)],
        scratch_shapes=[pltpu.VMEM((s, 3 * D_MODEL), F32), pltpu.VMEM((s, D_LORA_PAD), F32)],
        compiler_params=_params(),
        name="rwkv_prep",
    )(xb, sh_rkv, sh_lora, *consts)


def _sc_gate(x, sc_state, w, *, rows, s):
    nb, total, _ = x.shape
    grid = (nb, total // rows)
    consts = [w["norm1_g"], w["w_sc"], w["w_gate"], w["b_gate"], w["conv_sc"], w["w_branch_sc"]]
    tok = jax.ShapeDtypeStruct((nb, total, D_MODEL), F32)
    return pl.pallas_call(
        functools.partial(_sc_gate_kernel, s=s),
        grid=grid,
        in_specs=[_row_spec(rows, D_MODEL), _state_spec(2 * s, D_MODEL)] + [_const_spec(c) for c in consts],
        out_specs=[_row_spec(rows, D_MODEL)] * 3 + [_state_spec(2 * s, D_MODEL)],
        out_shape=[tok] * 2 + [jax.ShapeDtypeStruct((nb, total, D_MODEL), BF16),
                               jax.ShapeDtypeStruct((nb, 2 * s, D_MODEL), F32)],
        scratch_shapes=[pltpu.VMEM((2 * s, D_MODEL), F32)],
        compiler_params=_params(),
        name="sc_gate",
    )(x, sc_state, *consts)


def _wkv(r, lw, k, v, kk, b, s0, *, chunk, nq):
    nb, total, _ = r.shape
    grid = (nb // nq, total // chunk)
    tok_spec = pl.BlockSpec((nq, chunk, D_MODEL), lambda i, c: (i, c, 0))
    st_spec = pl.BlockSpec((nq, N_HEADS, HEAD_SIZE, HEAD_SIZE), lambda i, c: (i, 0, 0, 0))
    return pl.pallas_call(
        functools.partial(_wkv_kernel, C=chunk, nq=nq),
        grid=grid,
        in_specs=[tok_spec] * 6 + [st_spec],
        out_specs=[tok_spec, st_spec],
        out_shape=[jax.ShapeDtypeStruct((nb, total, D_MODEL), F32),
                   jax.ShapeDtypeStruct((nb, N_HEADS, HEAD_SIZE, HEAD_SIZE), F32)],
        scratch_shapes=[pltpu.VMEM((nq * N_PAIRS, LANES, LANES), F32)],
        compiler_params=_params(),
        name="wkv",
    )(r, lw, k, v, kk, b, s0)


def _wkv_lanes(r, lw, k, v, kk, b, s0, *, n_t):
    n_grp, total, _ = r.shape
    batch = s0.shape[-1]
    tok_spec = pl.BlockSpec((n_grp, total, LANES), lambda p: (0, 0, p))
    st_spec = pl.BlockSpec((2, HEAD_SIZE, HEAD_SIZE, batch), lambda p: (p, 0, 0, 0))
    return pl.pallas_call(
        functools.partial(_wkv_lanes_kernel, n_t=n_t, gsz=total // n_t),
        grid=(N_PAIRS,),
        in_specs=[tok_spec] * 6 + [st_spec],
        out_specs=[tok_spec, st_spec],
        out_shape=[jax.ShapeDtypeStruct(r.shape, F32), jax.ShapeDtypeStruct(s0.shape, F32)],
        scratch_shapes=[pltpu.VMEM((n_t, LANES, batch), F32)] * 7,
        compiler_params=pltpu.CompilerParams(dimension_semantics=("arbitrary",),
                                             vmem_limit_bytes=VMEM_LIMIT),
        name="wkv_lanes",
    )(r, lw, k, v, kk, b, s0)


def _post(x, y, bonus, g, ga, gbo, ffn_state, w, *, rows, s):
    nb, total, _ = x.shape
    grid = (nb, total // rows)
    consts = [w["lnx_g"], w["lnx_b"], w["w_branch_rwkv"], w["w_out"], w["norm2_g"], w["w_up"],
              w["conv_ffn"], w["w_down"], w["final_norm_g"]]
    return pl.pallas_call(
        functools.partial(_post_kernel, s=s),
        grid=grid,
        in_specs=[_row_spec(rows, D_MODEL)] * 6 + [_state_spec(2 * s, 2 * D_FF)]
                 + [_const_spec(c) for c in consts],
        out_specs=[_row_spec(rows, D_MODEL), _state_spec(2 * s, 2 * D_FF)],
        out_shape=[jax.ShapeDtypeStruct((nb, total, D_MODEL), F32),
                   jax.ShapeDtypeStruct((nb, 2 * s, 2 * D_FF), F32)],
        scratch_shapes=[pltpu.VMEM((2 * s, 2 * D_FF), F32)],
        compiler_params=_params(),
        name="post",
    )(x, y, bonus, g, ga, gbo, ffn_state, *consts)


def _pad_lora_cols(a):
    pad = lambda t, n: jnp.pad(t, [(0, 0)] * (t.ndim - 1) + [(0, n - t.shape[-1])])
    return jnp.concatenate([pad(a[..., :64], 128), pad(a[..., 64:128], 128), pad(a[..., 128:], 256)], axis=-1)


def _unpad_lora_cols(a):
    return jnp.concatenate([a[..., :64], a[..., 128:192], a[..., 256:256 + D_GATE_LORA]], axis=-1)


def _prep_weights(norm1_g, w_in, b_gate, mu_shift, w0, w_decay_up, a0, w_aaa_up, w_gate_up, k_k, k_a,
                  r_k, lnx_g, lnx_b, w_branch_rwkv, w_branch_sc, conv_sc, w_out, norm2_g, w_up, conv_ffn,
                  w_down, final_norm_g):
    row = lambda t: t.reshape(1, -1).astype(F32)
    d3 = 3 * D_MODEL
    n_lora = D_DECAY_LORA + D_AAA_LORA + D_GATE_LORA
    pad_rows = lambda t, n: jnp.pad(t, [(0, n - t.shape[0]), (0, 0)])
    return {
        "norm1_g": row(norm1_g),
        "w_rkv": w_in[:, :d3].astype(BF16),
        "w_lora": _pad_lora_cols(w_in[:, d3:d3 + n_lora]).astype(BF16),
        "w_sc": w_in[:, d3 + n_lora:2 * d3 + n_lora].astype(BF16),
        "w_gate": w_in[:, 2 * d3 + n_lora:].astype(BF16),
        "b_gate": row(b_gate),
        "mu_rkv": row(mu_shift[:d3]),
        "mu_lora": _pad_lora_cols(row(mu_shift[d3:])),
        "w0": row(w0),
        "wd": pad_rows(w_decay_up, 128).astype(BF16),
        "a0": row(a0),
        "wa": pad_rows(w_aaa_up, 128).astype(BF16),
        "wg": pad_rows(w_gate_up, 256).astype(BF16),
        "k_k": row(k_k), "k_a": row(k_a), "r_k": row(r_k),
        "lnx_g": row(lnx_g), "lnx_b": row(lnx_b),
        "w_branch_rwkv": w_branch_rwkv.astype(BF16),
        "w_branch_sc": w_branch_sc.astype(BF16),
        "conv_sc": conv_sc.astype(F32),
        "w_out": w_out.astype(BF16),
        "norm2_g": row(norm2_g),
        "w_up": w_up.astype(BF16),
        "conv_ffn": conv_ffn.astype(F32),
        "w_down": w_down.astype(BF16),
        "final_norm_g": row(final_norm_g),
    }


def _layer(x, s_wkv, sh_rkv, sh_lora, sc_state, ffn_state, w, *, rows, s, wkv, sc_rows=None, narrow_vg=False):
    ga, gbo, xb, sc_new = _sc_gate(x, sc_state, w, rows=sc_rows or rows, s=s)
    r, lw, k, v, kk, b, g, bonus, shr, shl = _rwkv_prep(xb, sh_rkv, sh_lora, w, rows=rows, s=s, narrow_vg=narrow_vg)
    y, s_new = wkv(r, lw, k, v, kk, b, s_wkv)
    out, ffn_new = _post(x, y, bonus, g, ga, gbo, ffn_state, w, rows=rows, s=s)
    return out, s_new, shr, shl, sc_new, ffn_new


def kernel(x_prompt, x_sample, state_wkv, state_shift, state_sc_conv, state_ffn_conv, meta_tokens,
           norm1_g, w_in, b_gate, mu_shift, w0, w_decay_up, a0, w_aaa_up, w_gate_up, k_k, k_a, r_k,
           lnx_g, lnx_b, w_branch_rwkv, w_branch_sc, conv_sc, w_out, norm2_g, w_up, conv_ffn, w_down,
           final_norm_g):
    w = _prep_weights(norm1_g[0], w_in[0], b_gate[0], mu_shift[0], w0[0], w_decay_up[0], a0[0],
                      w_aaa_up[0], w_gate_up[0], k_k[0], k_a[0], r_k[0], lnx_g[0], lnx_b[0],
                      w_branch_rwkv[0], w_branch_sc[0], conv_sc[0], w_out[0], norm2_g[0], w_up[0],
                      conv_ffn[0], w_down[0], final_norm_g)
    d3 = 3 * D_MODEL

    bp, seq, _ = x_prompt.shape
    zeros = lambda *shape: jnp.zeros(shape, F32)
    _, m_wkv, m_shr, m_shl, m_sc, m_ffn = _layer(
        meta_tokens.astype(F32)[None], zeros(1, N_HEADS, HEAD_SIZE, HEAD_SIZE), zeros(1, 1, d3),
        zeros(1, 1, D_LORA_PAD), zeros(1, 2, D_MODEL), zeros(1, 2, 2 * D_FF), w,
        rows=N_META, s=1, wkv=functools.partial(_wkv, chunk=N_META, nq=1))

    rep = lambda t: jnp.broadcast_to(t, (bp,) + t.shape[1:])
    y_prompt, p_wkv, p_shr, p_shl, p_sc, p_ffn = _layer(
        x_prompt, rep(m_wkv), rep(m_shr), rep(m_shl), rep(m_sc), rep(m_ffn), w,
        rows=PROMPT_ROWS, s=1, wkv=functools.partial(_wkv, chunk=WKV_CHUNK, nq=WKV_SEQS),
        sc_rows=PROMPT_SC_ROWS, narrow_vg=True)

    bs, ts, _ = x_sample.shape
    n_grp = SAMPLE_GROUPS
    gsz = bs // n_grp

    def to_rows(t):
        n, c = t.shape[1:]
        return t.reshape(n_grp, gsz, n, c).transpose(0, 2, 1, 3).reshape(n_grp, n * gsz, c)

    sh = state_shift[0].reshape(n_grp, gsz, -1)
    y_s, s_wkv, s_shr, s_shl, s_sc, s_ffn = _layer(
        to_rows(x_sample), state_wkv[0].transpose(1, 2, 3, 0), sh[..., :d3], _pad_lora_cols(sh[..., d3:]),
        to_rows(state_sc_conv[0]), to_rows(state_ffn_conv[0]), w,
        rows=ts * gsz, s=gsz, wkv=functools.partial(_wkv_lanes, n_t=ts))
    s_wkv = s_wkv.transpose(3, 0, 1, 2)

    def rows_to_batch(t, n):
        c = t.shape[-1]
        return t.reshape(n_grp, n, gsz, c).transpose(0, 2, 1, 3).reshape(bs, n, c)

    y_sample = rows_to_batch(y_s, ts)
    shift_p = jnp.concatenate([p_shr[:, 0], _unpad_lora_cols(p_shl[:, 0])], axis=-1)
    shift_s = jnp.concatenate([rows_to_batch(s_shr, 1)[:, 0], _unpad_lora_cols(rows_to_batch(s_shl, 1)[:, 0])],
                              axis=-1)
    return (y_prompt, y_sample,
            p_wkv[None], s_wkv[None],
            shift_p[None], shift_s[None],
            p_sc[None], rows_to_batch(s_sc, 2)[None],
            p_ffn[None], rows_to_batch(s_ffn, 2)[None])
```

```python
import functools

import jax
import jax.numpy as jnp
from jax import lax
from jax.experimental import pallas as pl
from jax.experimental.pallas import tpu as pltpu

D_MODEL = 1024
N_META = 16
HEAD_SIZE = 64
N_HEADS = D_MODEL // HEAD_SIZE
LANES = 128
N_PAIRS = D_MODEL // LANES
D_DECAY_LORA = 64
D_AAA_LORA = 64
D_GATE_LORA = 160
D_LORA_PAD = 512
D_FF = 2816
FF_CHUNK = 256
COL_CHUNK = 256
CONV_W = 3
NEG_LOG2_E = -1.4426950408889634
EXP_NEG_HALF = 0.6065306597126334
RMS_EPS = 1e-6
GN_EPS = 64e-5
VMEM_LIMIT = 60 * 1024 * 1024

WIDE_R, WIDE_LW, WIDE_K, WIDE_KK, WIDE_B, WIDE_BONUS = range(6)
VG_V, VG_G = range(2)

PROMPT_ROWS = 512
PROMPT_SC_ROWS = 1024
WKV_CHUNK = 64
WKV_SEQS = 4
SAMPLE_GROUPS = 2
LANE_ROW_UNROLL = 8

F32 = jnp.float32
BF16 = jnp.bfloat16


def _dot(a, b):
    return jnp.dot(a.astype(BF16), b.astype(BF16), preferred_element_type=F32)


def _dot_nt(a, b):
    return lax.dot_general(a.astype(BF16), b.astype(BF16), (((1,), (1,)), ((), ())),
                           preferred_element_type=F32)


def _dot_tn(a, b):
    return lax.dot_general(a.astype(BF16), b.astype(BF16), (((0,), (0,)), ((), ())),
                           preferred_element_type=F32)


def _head_ones(n):
    r = lax.broadcasted_iota(jnp.int32, (n, n), 0) // HEAD_SIZE
    c = lax.broadcasted_iota(jnp.int32, (n, n), 1) // HEAD_SIZE
    return jnp.where(r == c, 1.0, 0.0).astype(BF16)


def _head_sum(x, ones):
    return jnp.dot(x.astype(BF16), ones, preferred_element_type=F32)


def _rms_norm(x, g):
    return x * lax.rsqrt(jnp.mean(x * x, axis=-1, keepdims=True) + RMS_EPS) * g


def _sigmoid(x):
    return 1.0 / (1.0 + jnp.exp2(x * NEG_LOG2_E))


def _shift_rows(cur, carry, s):
    rows = cur.shape[0]
    if s % 8 == 0:
        return jnp.concatenate([carry, cur[:rows - s]], axis=0)
    assert s == 1
    rolled = pltpu.roll(cur, 1, axis=0)
    row = lax.broadcasted_iota(jnp.int32, cur.shape, 0)
    return jnp.where(row == 0, carry, rolled)


def _cumsum_rows(x):
    rows, cols = x.shape
    row = lax.broadcasted_iota(jnp.int32, x.shape, 0)
    sh = 1
    while sh < rows:
        if sh % 8 == 0:
            shifted = jnp.concatenate([jnp.zeros((sh, cols), x.dtype), x[:rows - sh]], axis=0)
        else:
            shifted = jnp.where(row >= sh, pltpu.roll(x, sh, axis=0), 0.0)
        x = x + shifted
        sh *= 2
    return x


def _rwkv_prep_kernel(xb_ref, sh_rkv_ref, sh_lora_ref, w_rkv_ref, w_lora_ref, mu_rkv_ref,
                      mu_lora_ref, w0_ref, wd_ref, a0_ref, wa_ref, wg_ref, kk_ref, ka_ref, rk_ref,
                      wide_out, vg_out, shr_out, shl_out,
                      c_rkv, c_lora, *, s):
    def put(ref, slot, cols, val):
        ref[:, slot * D_MODEL + cols.start:slot * D_MODEL + cols.stop] = val.astype(ref.dtype)

    @pl.when(pl.program_id(1) == 0)
    def _():
        c_rkv[...] = sh_rkv_ref[...]
        c_lora[...] = sh_lora_ref[...]

    rows = xb_ref.shape[0]
    xb = xb_ref[...]
    p_lora = jnp.dot(xb, w_lora_ref[...], preferred_element_type=F32)
    prev_lora = _shift_rows(p_lora, c_lora[...], s)
    xl = p_lora + (prev_lora - p_lora) * mu_lora_ref[...]
    new_lora = p_lora[rows - s:]
    c_lora[...] = new_lora
    shl_out[...] = new_lora
    t_xw = jnp.tanh(xl[:, :128]).astype(BF16)
    xa = xl[:, 128:256].astype(BF16)
    s_xg = _sigmoid(xl[:, 256:]).astype(BF16)

    def proj(c):
        return [jnp.dot(xb, w_rkv_ref[:, base + COL_CHUNK * c:base + COL_CHUNK * (c + 1)],
                        preferred_element_type=F32) for base in (0, D_MODEL, 2 * D_MODEL)]

    ones = _head_ones(COL_CHUNK)
    nxt = proj(0)
    for c in range(D_MODEL // COL_CHUNK):
        cur = nxt
        if c + 1 < D_MODEL // COL_CHUNK:
            nxt = proj(c + 1)
        cols = slice(COL_CHUNK * c, COL_CHUNK * (c + 1))
        shifted = []
        for part, p in enumerate(cur):
            pcols = slice(part * D_MODEL + COL_CHUNK * c, part * D_MODEL + COL_CHUNK * (c + 1))
            prev = _shift_rows(p, c_rkv[:, pcols], s)
            shifted.append(p + (prev - p) * mu_rkv_ref[:, pcols])
            new = p[rows - s:]
            c_rkv[:, pcols] = new
            shr_out[:, pcols] = new
        r, k, v = shifted

        zw = w0_ref[:, cols] + jnp.dot(t_xw, wd_ref[:, cols], preferred_element_type=F32)
        put(wide_out, WIDE_LW, cols, _sigmoid(zw) * (-EXP_NEG_HALF))
        a = _sigmoid(a0_ref[:, cols] + jnp.dot(xa, wa_ref[:, cols], preferred_element_type=F32))
        put(vg_out, VG_G, cols, jnp.dot(s_xg, wg_ref[:, cols], preferred_element_type=F32))

        kkr = k * kk_ref[:, cols]
        kk = kkr * lax.rsqrt(jnp.maximum(_head_sum(kkr * kkr, ones), 1e-24))
        k2 = k * (1.0 + (a - 1.0) * ka_ref[:, cols])
        put(wide_out, WIDE_R, cols, r)
        put(wide_out, WIDE_K, cols, k2)
        put(vg_out, VG_V, cols, v)
        put(wide_out, WIDE_KK, cols, kk)
        put(wide_out, WIDE_B, cols, kk * a)
        put(wide_out, WIDE_BONUS, cols, _head_sum(r * k2 * rk_ref[:, cols], ones) * v)


def _sc_gate_kernel(x_ref, sc_state_ref, n1g_ref, w_sc_ref, w_gate_ref, b_gate_ref, conv_ref,
                    w_bsc_ref, ga_out, gbo_out, xb_out, sc_out, c_sc, *, s):
    @pl.when(pl.program_id(1) == 0)
    def _():
        c_sc[...] = sc_state_ref[...]

    rows = x_ref.shape[0]
    xb = _rms_norm(x_ref[...], n1g_ref[...]).astype(BF16)
    xb_out[...] = xb

    def proj(c):
        cs = lambda base: slice(base + COL_CHUNK * c, base + COL_CHUNK * (c + 1))
        sc = [jnp.dot(xb, w_sc_ref[:, cs(base)], preferred_element_type=F32)
              for base in (0, D_MODEL, 2 * D_MODEL)]
        gate = [jnp.dot(xb, w_gate_ref[:, cs(base)], preferred_element_type=F32) + b_gate_ref[:, cs(base)]
                for base in (0, D_MODEL)]
        return sc + gate

    mid = []
    gb = []
    nxt = proj(0)
    for c in range(D_MODEL // COL_CHUNK):
        cur = nxt
        if c + 1 < D_MODEL // COL_CHUNK:
            nxt = proj(c + 1)
        cols = slice(COL_CHUNK * c, COL_CHUNK * (c + 1))
        h, b_g, c_g, pg_a, pg_b = cur
        u = c_g * h
        carry = c_sc[:, cols]
        prev1 = _shift_rows(u, carry[s:], s)
        prev2 = _shift_rows(prev1, carry[:s], s)
        cw = conv_ref[:, cols]
        conv = prev2 * cw[0:1] + prev1 * cw[1:2] + u * cw[2:3]
        mid.append((b_g * conv).astype(BF16))
        new = u[rows - 2 * s:]
        c_sc[:, cols] = new
        sc_out[:, cols] = new
        ga_out[:, cols] = _sigmoid(pg_a)
        gb.append(_sigmoid(pg_b))
    o_b = jnp.dot(jnp.concatenate(mid, axis=1), w_bsc_ref[...], preferred_element_type=F32)
    for c in range(D_MODEL // COL_CHUNK):
        cols = slice(COL_CHUNK * c, COL_CHUNK * (c + 1))
        gbo_out[:, cols] = gb[c] * o_b[:, cols]


def _wkv_kernel(r_ref, lw_ref, k_ref, v_ref, kk_ref, b_ref, s0_ref, y_out, s_out, s_scr, *, C, nq):
    c_idx = pl.program_id(1)

    @pl.when(c_idx == 0)
    def _():
        zh = jnp.zeros((HEAD_SIZE, HEAD_SIZE), F32)
        for q in range(nq):
            for p in range(N_PAIRS):
                s_scr[q * N_PAIRS + p] = jnp.concatenate(
                    [jnp.concatenate([s0_ref[q, 2 * p], zh], axis=1),
                     jnp.concatenate([zh, s0_ref[q, 2 * p + 1]], axis=1)], axis=0)

    m0 =lax.broadcasted_iota(jnp.int32, (C, LANES), 1) < HEAD_SIZE
    gi = lax.broadcasted_iota(jnp.int32, (C, 2 * C), 0)
    gj = lax.broadcasted_iota(jnp.int32, (C, 2 * C), 1)
    gjm = jnp.where(gj >= C, gj - C, gj)
    strict = gjm < gi
    incl = gjm <= gi
    left = gj < C
    eye2 = jnp.where(gjm == gi, 1.0, 0.0)
    bi = lax.broadcasted_iota(jnp.int32, (LANES, LANES), 0) // HEAD_SIZE
    bj = lax.broadcasted_iota(jnp.int32, (LANES, LANES), 1) // HEAD_SIZE
    bd = bi == bj

    def swap_halves(x):
        if 2 * C == LANES:
            return pltpu.roll(x, C, axis=1)
        return jnp.concatenate([x[:, C:], x[:, :C]], axis=1)

    a_p, r_p, v_p, bk, w_last = [], [], [], [], []
    for q in range(nq):
        lw = lw_ref[q]
        cum = _cumsum_rows(lw)
        w_t = jnp.exp(cum)
        w_inv = jnp.exp(-cum)
        a_t = -kk_ref[q] * jnp.exp(cum - lw)
        b_t = b_ref[q] * w_inv
        k_t = k_ref[q] * w_inv
        r_t = r_ref[q] * w_t
        v_q = v_ref[q].astype(F32)
        for p in range(N_PAIRS):
            sl = slice(LANES * p, LANES * (p + 1))
            a_p.append(a_t[:, sl])
            r_p.append(r_t[:, sl])
            v_p.append(v_q[:, sl])
            bk.append(jnp.concatenate([b_t[:, sl], k_t[:, sl]], axis=0))
            w_last.append(w_t[C - 1:C, sl])
    ents = range(nq * N_PAIRS)

    g = [_dot_nt(jnp.concatenate([jnp.where(m0, a_p[e], 0.0), jnp.where(m0, r_p[e], 0.0),
                                  jnp.where(m0, 0.0, a_p[e]), jnp.where(m0, 0.0, r_p[e])], axis=0), bk[e])
         for e in ents]
    s_bd = [s_scr[e] for e in ents]
    s_t = [jnp.transpose(s_bd[e]) for e in ents]
    g1a = [swap_halves(g[e][2 * C:3 * C]) for e in ents]
    def block_diag(x):
        return jnp.concatenate([jnp.where(left, x, 0.0), jnp.where(left, 0.0, x)], axis=0)

    xpow = [jnp.where(strict, jnp.where(left, g[e][:C], g1a[e]), 0.0) for e in ents]
    tinv = [eye2 + xpow[e] for e in ents]
    n = 1
    if 2 * n < C:
        xpow = [_dot(xpow[e], block_diag(xpow[e])) for e in ents]
        n *= 2
    while 2 * n < C:
        z = [_dot(jnp.concatenate([xpow[e], tinv[e]], axis=0), block_diag(xpow[e])) for e in ents]
        xpow = [z[e][:C] for e in ents]
        tinv = [tinv[e] + z[e][C:] for e in ents]
        n *= 2
    if C > 2:
        tinv = [tinv[e] + _dot(tinv[e], block_diag(xpow[e])) for e in ents]
    vm0 = [jnp.where(m0, v_p[e], 0.0) for e in ents]
    vm1 = [jnp.where(m0, 0.0, v_p[e]) for e in ents]
    mak = [jnp.where(strict, jnp.where(left, g1a[e], g[e][:C]), 0.0) for e in ents]
    zmak = jnp.zeros((C, 2 * C), F32)
    sv = [_dot(jnp.concatenate([jnp.concatenate([a_p[e], mak[e]], axis=1),
                                jnp.concatenate([r_p[e], zmak], axis=1)], axis=0),
               jnp.concatenate([s_t[e], vm1[e], vm0[e]], axis=0)) for e in ents]
    rhs = [sv[e][:C] for e in ents]
    rs = [sv[e][C:] for e in ents]
    u = [_dot(tinv[e], jnp.concatenate([jnp.where(m0, rhs[e], 0.0), jnp.where(m0, 0.0, rhs[e])], axis=0))
         for e in ents]
    for e in ents:
        q, p = divmod(e, N_PAIRS)
        um0 = jnp.where(m0, u[e], 0.0)
        um1 = jnp.where(m0, 0.0, u[e])
        g0r = jnp.where(incl, g[e][C:2 * C], 0.0)
        g1r = jnp.where(incl, g[e][3 * C:], 0.0)
        if 2 * C == LANES:
            intra = _dot(jnp.concatenate([g0r, g1r], axis=1),
                         jnp.concatenate([um0, vm0[e], um1, vm1[e]], axis=0))
        else:
            intra = (_dot(g0r, jnp.concatenate([um0, vm0[e]], axis=0))
                     + _dot(g1r, jnp.concatenate([um1, vm1[e]], axis=0)))
        y_out[q, :, LANES * p:LANES * (p + 1)] = rs[e] + intra
    for e in ents:
        upd = _dot_tn(jnp.concatenate([u[e], v_p[e]], axis=0), bk[e])
        s_scr[e] = (s_bd[e] + jnp.where(bd, upd, 0.0)) * w_last[e]

    @pl.when(c_idx == pl.num_programs(1) - 1)
    def _():
        for q in range(nq):
            for p in range(N_PAIRS):
                s_fin = s_scr[q * N_PAIRS + p]
                s_out[q, 2 * p] = s_fin[:HEAD_SIZE, :HEAD_SIZE]
                s_out[q, 2 * p + 1] = s_fin[HEAD_SIZE:, HEAD_SIZE:]


def _wkv_lanes_kernel(r_ref, lw_ref, k_ref, v_ref, kk_ref, b_ref, s0_ref, y_out, s_out,
                      r_t, w_t, k_t, v_t, a_t, b_t, y_t, *, n_t, gsz):
    n_grp = r_ref.shape[0]

    def by_channel(ref, t):
        rows = [ref[g, t * gsz:(t + 1) * gsz, :] for g in range(n_grp)]
        return jnp.transpose(jnp.concatenate(rows, axis=0))

    for t in range(n_t):
        r_t[t] = by_channel(r_ref, t)
        w_t[t] = jnp.exp(by_channel(lw_ref, t))
        k_t[t] = by_channel(k_ref, t)
        v_t[t] = by_channel(v_ref, t)
        a_t[t] = -by_channel(kk_ref, t)
        b_t[t] = by_channel(b_ref, t)

    for hh in range(2):
        ch = slice(hh * HEAD_SIZE, (hh + 1) * HEAD_SIZE)

        def row_step(i, carry):
            s = s0_ref[hh, i]
            for t in range(n_t):
                sa = jnp.sum(s * a_t[t, ch, :], axis=0, keepdims=True)
                v_i = v_t[t, pl.ds(hh * HEAD_SIZE + i, 1), :]
                s = s * w_t[t, ch, :] + sa * b_t[t, ch, :] + v_i * k_t[t, ch, :]
                y_t[t, pl.ds(hh * HEAD_SIZE + i, 1), :] = jnp.sum(s * r_t[t, ch, :], axis=0, keepdims=True)
            s_out[hh, i] = s
            return carry

        lax.fori_loop(0, HEAD_SIZE, row_step, 0, unroll=LANE_ROW_UNROLL)

    for t in range(n_t):
        y = jnp.transpose(y_t[t])
        for g in range(n_grp):
            y_out[g, t * gsz:(t + 1) * gsz, :] = y[g * gsz:(g + 1) * gsz]


def _post_kernel(x_ref, y_ref, bonus_ref, g_ref, ga_ref, gbo_ref, ffn_state_ref, lnx_g_ref, lnx_b_ref,
                 w_br_ref, w_out_ref, n2g_ref, w_up_ref, conv_ref, w_down_ref, fng_ref,
                 out_ref, ffn_out, c_ffn, *, s):
    @pl.when(pl.program_id(1) == 0)
    def _():
        c_ffn[...] = ffn_state_ref[...]

    rows = x_ref.shape[0]
    ones = _head_ones(COL_CHUNK)
    z = []
    for c in range(D_MODEL // COL_CHUNK):
        cols = slice(COL_CHUNK * c, COL_CHUNK * (c + 1))
        y = y_ref[:, cols]
        dev = y - _head_sum(y, ones) * (1.0 / HEAD_SIZE)
        var = _head_sum(dev * dev, ones) * (1.0 / HEAD_SIZE)
        yn = dev * lax.rsqrt(var + GN_EPS) * lnx_g_ref[:, cols] + lnx_b_ref[:, cols]
        z.append(((yn + bonus_ref[:, cols]) * g_ref[:, cols].astype(F32)).astype(BF16))
    o_a = jnp.dot(jnp.concatenate(z, axis=1), w_br_ref[...], preferred_element_type=F32)
    x1 = x_ref[...] + _dot(ga_ref[...] * o_a + gbo_ref[...], w_out_ref[...])

    xb = _rms_norm(x1, n2g_ref[...]).astype(BF16)
    n_chunk = D_FF // FF_CHUNK

    def up_proj(j):
        return [jnp.dot(xb, w_up_ref[:, base + FF_CHUNK * j:base + FF_CHUNK * (j + 1)],
                        preferred_element_type=F32) for base in (0, D_FF)]

    hidden = []
    ups = up_proj(0)
    for j in range(n_chunk):
        cur = ups
        if j + 1 < n_chunk:
            ups = up_proj(j + 1)
        halves = []
        for base, up in zip((0, D_FF), cur):
            cols = slice(base + FF_CHUNK * j, base + FF_CHUNK * (j + 1))
            carry = c_ffn[:, cols]
            prev1 = _shift_rows(up, carry[s:], s)
            prev2 = _shift_rows(prev1, carry[:s], s)
            cw = conv_ref[:, cols]
            halves.append(prev2 * cw[0:1] + prev1 * cw[1:2] + up * cw[2:3])
            new = up[rows - 2 * s:]
            c_ffn[:, cols] = new
            ffn_out[:, cols] = new
        gate, val = halves
        hidden.append((gate * _sigmoid(gate) * val).astype(BF16))
    down = jnp.dot(jnp.concatenate(hidden, axis=1), w_down_ref[...], preferred_element_type=F32)
    out_ref[...] = _rms_norm(x1 + down, fng_ref[...])


def _const_spec(arr):
    nd = arr.ndim
    return pl.BlockSpec(arr.shape, lambda b, t: (0,) * nd, pipeline_mode=pl.Buffered(1))


def _row_spec(rows, cols, col_block=0):
    return pl.BlockSpec((None, rows, cols), lambda b, t: (b, t, col_block))


def _tok(arr, col_block=0):
    return arr, col_block


def _state_spec(rows, cols):
    return pl.BlockSpec((None, rows, cols), lambda b, t: (b, 0, 0))


def _params():
    return pltpu.CompilerParams(dimension_semantics=("arbitrary", "arbitrary"),
                                vmem_limit_bytes=VMEM_LIMIT)


def _rwkv_prep(xb, sh_rkv, sh_lora, w, *, rows, s, narrow_vg):
    nb, total, _ = xb.shape
    grid = (nb, total // rows)
    consts = [w["w_rkv"], w["w_lora"], w["mu_rkv"], w["mu_lora"], w["w0"], w["wd"],
              w["a0"], w["wa"], w["wg"], w["k_k"], w["k_a"], w["r_k"]]
    wide = jax.ShapeDtypeStruct((nb, total, 6 * D_MODEL), F32)
    vg = jax.ShapeDtypeStruct((nb, total, 2 * D_MODEL), BF16 if narrow_vg else F32)
    return pl.pallas_call(
        functools.partial(_rwkv_prep_kernel, s=s),
        grid=grid,
        in_specs=[_row_spec(rows, D_MODEL), _state_spec(s, 3 * D_MODEL), _state_spec(s, D_LORA_PAD)]
                 + [_const_spec(c) for c in consts],
        out_specs=[_row_spec(rows, 6 * D_MODEL), _row_spec(rows, 2 * D_MODEL),
                   _state_spec(s, 3 * D_MODEL), _state_spec(s, D_LORA_PAD)],
        out_shape=[wide, vg] + [jax.ShapeDtypeStruct((nb, s, 3 * D_MODEL), F32),
                               jax.ShapeDtypeStruct((nb, s, D_LORA_PAD), F32)],
        scratch_shapes=[pltpu.VMEM((s, 3 * D_MODEL), F32), pltpu.VMEM((s, D_LORA_PAD), F32)],
        compiler_params=_params(),
        name="rwkv_prep",
    )(xb, sh_rkv, sh_lora, *consts)


def _sc_gate(x, sc_state, w, *, rows, s):
    nb, total, _ = x.shape
    grid = (nb, total // rows)
    consts = [w["norm1_g"], w["w_sc"], w["w_gate"], w["b_gate"], w["conv_sc"], w["w_branch_sc"]]
    tok = jax.ShapeDtypeStruct((nb, total, D_MODEL), F32)
    return pl.pallas_call(
        functools.partial(_sc_gate_kernel, s=s),
        grid=grid,
        in_specs=[_row_spec(rows, D_MODEL), _state_spec(2 * s, D_MODEL)] + [_const_spec(c) for c in consts],
        out_specs=[_row_spec(rows, D_MODEL)] * 3 + [_state_spec(2 * s, D_MODEL)],
        out_shape=[tok] * 2 + [jax.ShapeDtypeStruct((nb, total, D_MODEL), BF16),
                               jax.ShapeDtypeStruct((nb, 2 * s, D_MODEL), F32)],
        scratch_shapes=[pltpu.VMEM((2 * s, D_MODEL), F32)],
        compiler_params=_params(),
        name="sc_gate",
    )(x, sc_state, *consts)


def _wkv(r, lw, k, v, kk, b, s0, *, chunk, nq):
    toks = (r, lw, k, v, kk, b)
    nb, total, _ = r[0].shape
    grid = (nb // nq, total // chunk)
    tok_spec = pl.BlockSpec((nq, chunk, D_MODEL), lambda i, c: (i, c, 0))
    in_tok_specs = [pl.BlockSpec((nq, chunk, D_MODEL), functools.partial(lambda i, c, blk: (i, c, blk), blk=blk))
                    for _, blk in toks]
    st_spec = pl.BlockSpec((nq, N_HEADS, HEAD_SIZE, HEAD_SIZE), lambda i, c: (i, 0, 0, 0))
    return pl.pallas_call(
        functools.partial(_wkv_kernel, C=chunk, nq=nq),
        grid=grid,
        in_specs=in_tok_specs + [st_spec],
        out_specs=[tok_spec, st_spec],
        out_shape=[jax.ShapeDtypeStruct((nb, total, D_MODEL), F32),
                   jax.ShapeDtypeStruct((nb, N_HEADS, HEAD_SIZE, HEAD_SIZE), F32)],
        scratch_shapes=[pltpu.VMEM((nq * N_PAIRS, LANES, LANES), F32)],
        compiler_params=_params(),
        name="wkv",
    )(*[arr for arr, _ in toks], s0)


def _wkv_lanes(r, lw, k, v, kk, b, s0, *, n_t):
    toks = (r, lw, k, v, kk, b)
    n_grp, total, _ = r[0].shape
    batch = s0.shape[-1]
    tok_spec = pl.BlockSpec((n_grp, total, LANES), lambda p: (0, 0, p))
    in_tok_specs = [pl.BlockSpec((n_grp, total, LANES),
                                 functools.partial(lambda p, blk: (0, 0, blk * N_PAIRS + p), blk=blk))
                    for _, blk in toks]
    st_spec = pl.BlockSpec((2, HEAD_SIZE, HEAD_SIZE, batch), lambda p: (p, 0, 0, 0))
    return pl.pallas_call(
        functools.partial(_wkv_lanes_kernel, n_t=n_t, gsz=total // n_t),
        grid=(N_PAIRS,),
        in_specs=in_tok_specs + [st_spec],
        out_specs=[tok_spec, st_spec],
        out_shape=[jax.ShapeDtypeStruct((n_grp, total, D_MODEL), F32), jax.ShapeDtypeStruct(s0.shape, F32)],
        scratch_shapes=[pltpu.VMEM((n_t, LANES, batch), F32)] * 7,
        compiler_params=pltpu.CompilerParams(dimension_semantics=("arbitrary",),
                                             vmem_limit_bytes=VMEM_LIMIT),
        name="wkv_lanes",
    )(*[arr for arr, _ in toks], s0)


def _post(x, y, bonus, g, ga, gbo, ffn_state, w, *, rows, s):
    nb, total, _ = x.shape
    grid = (nb, total // rows)
    consts = [w["lnx_g"], w["lnx_b"], w["w_branch_rwkv"], w["w_out"], w["norm2_g"], w["w_up"],
              w["conv_ffn"], w["w_down"], w["final_norm_g"]]
    return pl.pallas_call(
        functools.partial(_post_kernel, s=s),
        grid=grid,
        in_specs=[_row_spec(rows, D_MODEL), _row_spec(rows, D_MODEL), _row_spec(rows, D_MODEL, bonus[1]),
                  _row_spec(rows, D_MODEL, g[1]), _row_spec(rows, D_MODEL), _row_spec(rows, D_MODEL),
                  _state_spec(2 * s, 2 * D_FF)] + [_const_spec(c) for c in consts],
        out_specs=[_row_spec(rows, D_MODEL), _state_spec(2 * s, 2 * D_FF)],
        out_shape=[jax.ShapeDtypeStruct((nb, total, D_MODEL), F32),
                   jax.ShapeDtypeStruct((nb, 2 * s, 2 * D_FF), F32)],
        scratch_shapes=[pltpu.VMEM((2 * s, 2 * D_FF), F32)],
        compiler_params=_params(),
        name="post",
    )(x, y, bonus[0], g[0], ga, gbo, ffn_state, *consts)


def _pad_lora_cols(a):
    pad = lambda t, n: jnp.pad(t, [(0, 0)] * (t.ndim - 1) + [(0, n - t.shape[-1])])
    return jnp.concatenate([pad(a[..., :64], 128), pad(a[..., 64:128], 128), pad(a[..., 128:], 256)], axis=-1)


def _unpad_lora_cols(a):
    return jnp.concatenate([a[..., :64], a[..., 128:192], a[..., 256:256 + D_GATE_LORA]], axis=-1)


def _prep_weights(norm1_g, w_in, b_gate, mu_shift, w0, w_decay_up, a0, w_aaa_up, w_gate_up, k_k, k_a,
                  r_k, lnx_g, lnx_b, w_branch_rwkv, w_branch_sc, conv_sc, w_out, norm2_g, w_up, conv_ffn,
                  w_down, final_norm_g):
    row = lambda t: t.reshape(1, -1).astype(F32)
    d3 = 3 * D_MODEL
    n_lora = D_DECAY_LORA + D_AAA_LORA + D_GATE_LORA
    pad_rows = lambda t, n: jnp.pad(t, [(0, n - t.shape[0]), (0, 0)])
    return {
        "norm1_g": row(norm1_g),
        "w_rkv": w_in[:, :d3].astype(BF16),
        "w_lora": _pad_lora_cols(w_in[:, d3:d3 + n_lora]).astype(BF16),
        "w_sc": w_in[:, d3 + n_lora:2 * d3 + n_lora].astype(BF16),
        "w_gate": w_in[:, 2 * d3 + n_lora:].astype(BF16),
        "b_gate": row(b_gate),
        "mu_rkv": row(mu_shift[:d3]),
        "mu_lora": _pad_lora_cols(row(mu_shift[d3:])),
        "w0": row(w0),
        "wd": pad_rows(w_decay_up, 128).astype(BF16),
        "a0": row(a0),
        "wa": pad_rows(w_aaa_up, 128).astype(BF16),
        "wg": pad_rows(w_gate_up, 256).astype(BF16),
        "k_k": row(k_k), "k_a": row(k_a), "r_k": row(r_k),
        "lnx_g": row(lnx_g), "lnx_b": row(lnx_b),
        "w_branch_rwkv": w_branch_rwkv.astype(BF16),
        "w_branch_sc": w_branch_sc.astype(BF16),
        "conv_sc": conv_sc.astype(F32),
        "w_out": w_out.astype(BF16),
        "norm2_g": row(norm2_g),
        "w_up": w_up.astype(BF16),
        "conv_ffn": conv_ffn.astype(F32),
        "w_down": w_down.astype(BF16),
        "final_norm_g": row(final_norm_g),
    }


def _layer(x, s_wkv, sh_rkv, sh_lora, sc_state, ffn_state, w, *, rows, s, wkv, sc_rows=None, narrow_vg=False):
    ga, gbo, xb, sc_new = _sc_gate(x, sc_state, w, rows=sc_rows or rows, s=s)
    wide, vg, shr, shl = _rwkv_prep(xb, sh_rkv, sh_lora, w, rows=rows, s=s, narrow_vg=narrow_vg)
    y, s_new = wkv(_tok(wide, WIDE_R), _tok(wide, WIDE_LW), _tok(wide, WIDE_K), _tok(vg, VG_V),
                   _tok(wide, WIDE_KK), _tok(wide, WIDE_B), s_wkv)
    out, ffn_new = _post(x, y, _tok(wide, WIDE_BONUS), _tok(vg, VG_G), ga, gbo, ffn_state, w, rows=rows, s=s)
    return out, s_new, shr, shl, sc_new, ffn_new


def kernel(x_prompt, x_sample, state_wkv, state_shift, state_sc_conv, state_ffn_conv, meta_tokens,
           norm1_g, w_in, b_gate, mu_shift, w0, w_decay_up, a0, w_aaa_up, w_gate_up, k_k, k_a, r_k,
           lnx_g, lnx_b, w_branch_rwkv, w_branch_sc, conv_sc, w_out, norm2_g, w_up, conv_ffn, w_down,
           final_norm_g):
    w = _prep_weights(norm1_g[0], w_in[0], b_gate[0], mu_shift[0], w0[0], w_decay_up[0], a0[0],
                      w_aaa_up[0], w_gate_up[0], k_k[0], k_a[0], r_k[0], lnx_g[0], lnx_b[0],
                      w_branch_rwkv[0], w_branch_sc[0], conv_sc[0], w_out[0], norm2_g[0], w_up[0],
                      conv_ffn[0], w_down[0], final_norm_g)
    d3 = 3 * D_MODEL

    bp, seq, _ = x_prompt.shape
    zeros = lambda *shape: jnp.zeros(shape, F32)
    _, m_wkv, m_shr, m_shl, m_sc, m_ffn = _layer(
        meta_tokens.astype(F32)[None], zeros(1, N_HEADS, HEAD_SIZE, HEAD_SIZE), zeros(1, 1, d3),
        zeros(1, 1, D_LORA_PAD), zeros(1, 2, D_MODEL), zeros(1, 2, 2 * D_FF), w,
        rows=N_META, s=1, wkv=functools.partial(_wkv, chunk=N_META, nq=1))

    rep = lambda t: jnp.broadcast_to(t, (bp,) + t.shape[1:])
    y_prompt, p_wkv, p_shr, p_shl, p_sc, p_ffn = _layer(
        x_prompt, rep(m_wkv), rep(m_shr), rep(m_shl), rep(m_sc), rep(m_ffn), w,
        rows=PROMPT_ROWS, s=1, wkv=functools.partial(_wkv, chunk=WKV_CHUNK, nq=WKV_SEQS),
        sc_rows=PROMPT_SC_ROWS, narrow_vg=True)

    bs, ts, _ = x_sample.shape
    n_grp = SAMPLE_GROUPS
    gsz = bs // n_grp

    def to_rows(t):
        n, c = t.shape[1:]
        return t.reshape(n_grp, gsz, n, c).transpose(0, 2, 1, 3).reshape(n_grp, n * gsz, c)

    sh = state_shift[0].reshape(n_grp, gsz, -1)
    y_s, s_wkv, s_shr, s_shl, s_sc, s_ffn = _layer(
        to_rows(x_sample), state_wkv[0].transpose(1, 2, 3, 0), sh[..., :d3], _pad_lora_cols(sh[..., d3:]),
        to_rows(state_sc_conv[0]), to_rows(state_ffn_conv[0]), w,
        rows=ts * gsz, s=gsz, wkv=functools.partial(_wkv_lanes, n_t=ts))
    s_wkv = s_wkv.transpose(3, 0, 1, 2)

    def rows_to_batch(t, n):
        c = t.shape[-1]
        return t.reshape(n_grp, n, gsz, c).transpose(0, 2, 1, 3).reshape(bs, n, c)

    y_sample = rows_to_batch(y_s, ts)
    shift_p = jnp.concatenate([p_shr[:, 0], _unpad_lora_cols(p_shl[:, 0])], axis=-1)
    shift_s = jnp.concatenate([rows_to_batch(s_shr, 1)[:, 0], _unpad_lora_cols(rows_to_batch(s_shl, 1)[:, 0])],
                              axis=-1)
    return (y_prompt, y_sample,
            p_wkv[None], s_wkv[None],
            shift_p[None], shift_s[None],
            p_sc[None], rows_to_batch(s_sc, 2)[None],
            p_ffn[None], rows_to_batch(s_ffn, 2)[None])
```

```python
import functools

import jax
import jax.numpy as jnp
from jax import lax
from jax.experimental import pallas as pl
from jax.experimental.pallas import tpu as pltpu

D_MODEL = 1024
N_META = 16
HEAD_SIZE = 64
N_HEADS = D_MODEL // HEAD_SIZE
LANES = 128
N_PAIRS = D_MODEL // LANES
D_DECAY_LORA = 64
D_AAA_LORA = 64
D_GATE_LORA = 160
D_LORA_PAD = 512
D_FF = 2816
FF_CHUNK = 256
COL_CHUNK = 256
CONV_W = 3
NEG_LOG2_E = -1.4426950408889634
EXP_NEG_HALF = 0.6065306597126334
RMS_EPS = 1e-6
GN_EPS = 64e-5
VMEM_LIMIT = 60 * 1024 * 1024

PROMPT_ROWS = 512
PROMPT_SC_ROWS = 1024
WKV_CHUNK = 64
WKV_SEQS = 4
SAMPLE_GROUPS = 2
LANE_ROW_UNROLL = 16

F32 = jnp.float32
BF16 = jnp.bfloat16


def _dot(a, b):
    return jnp.dot(a.astype(BF16), b.astype(BF16), preferred_element_type=F32)


def _dot_nt(a, b):
    return lax.dot_general(a.astype(BF16), b.astype(BF16), (((1,), (1,)), ((), ())),
                           preferred_element_type=F32)


def _dot_tn(a, b):
    return lax.dot_general(a.astype(BF16), b.astype(BF16), (((0,), (0,)), ((), ())),
                           preferred_element_type=F32)


def _head_ones(n):
    r = lax.broadcasted_iota(jnp.int32, (n, n), 0) // HEAD_SIZE
    c = lax.broadcasted_iota(jnp.int32, (n, n), 1) // HEAD_SIZE
    return jnp.where(r == c, 1.0, 0.0).astype(BF16)


def _head_sum(x, ones):
    return jnp.dot(x.astype(BF16), ones, preferred_element_type=F32)


def _rms_norm(x, g):
    return x * lax.rsqrt(jnp.mean(x * x, axis=-1, keepdims=True) + RMS_EPS) * g


def _sigmoid(x):
    return 1.0 / (1.0 + jnp.exp2(x * NEG_LOG2_E))


def _shift_rows(cur, carry, s):
    rows = cur.shape[0]
    if s % 8 == 0:
        return jnp.concatenate([carry, cur[:rows - s]], axis=0)
    assert s == 1
    rolled = pltpu.roll(cur, 1, axis=0)
    row = lax.broadcasted_iota(jnp.int32, cur.shape, 0)
    return jnp.where(row == 0, carry, rolled)


def _cumsum_rows(x):
    rows, cols = x.shape
    row = lax.broadcasted_iota(jnp.int32, x.shape, 0)
    sh = 1
    while sh < rows:
        if sh % 8 == 0:
            shifted = jnp.concatenate([jnp.zeros((sh, cols), x.dtype), x[:rows - sh]], axis=0)
        else:
            shifted = jnp.where(row >= sh, pltpu.roll(x, sh, axis=0), 0.0)
        x = x + shifted
        sh *= 2
    return x


def _rwkv_prep_kernel(xb_ref, sh_rkv_ref, sh_lora_ref, w_rkv_ref, w_lora_ref, mu_rkv_ref,
                      mu_lora_ref, w0_ref, wd_ref, a0_ref, wa_ref, wg_ref, kk_ref, ka_ref, rk_ref,
                      r_out, lw_out, k_out, v_out, kk_out, b_out, g_out, bonus_out, shr_out, shl_out,
                      c_rkv, c_lora, *, s):
    @pl.when(pl.program_id(1) == 0)
    def _():
        c_rkv[...] = sh_rkv_ref[...]
        c_lora[...] = sh_lora_ref[...]

    rows = xb_ref.shape[0]
    xb = xb_ref[...]
    p_lora = jnp.dot(xb, w_lora_ref[...], preferred_element_type=F32)
    prev_lora = _shift_rows(p_lora, c_lora[...], s)
    xl = p_lora + (prev_lora - p_lora) * mu_lora_ref[...]
    new_lora = p_lora[rows - s:]
    c_lora[...] = new_lora
    shl_out[...] = new_lora
    t_xw = jnp.tanh(xl[:, :128]).astype(BF16)
    xa = xl[:, 128:256].astype(BF16)
    s_xg = _sigmoid(xl[:, 256:]).astype(BF16)

    def proj(c):
        return [jnp.dot(xb, w_rkv_ref[:, base + COL_CHUNK * c:base + COL_CHUNK * (c + 1)],
                        preferred_element_type=F32) for base in (0, D_MODEL, 2 * D_MODEL)]

    ones = _head_ones(COL_CHUNK)
    nxt = proj(0)
    for c in range(D_MODEL // COL_CHUNK):
        cur = nxt
        if c + 1 < D_MODEL // COL_CHUNK:
            nxt = proj(c + 1)
        cols = slice(COL_CHUNK * c, COL_CHUNK * (c + 1))
        shifted = []
        for part, p in enumerate(cur):
            pcols = slice(part * D_MODEL + COL_CHUNK * c, part * D_MODEL + COL_CHUNK * (c + 1))
            prev = _shift_rows(p, c_rkv[:, pcols], s)
            shifted.append(p + (prev - p) * mu_rkv_ref[:, pcols])
            new = p[rows - s:]
            c_rkv[:, pcols] = new
            shr_out[:, pcols] = new
        r, k, v = shifted

        zw = w0_ref[:, cols] + jnp.dot(t_xw, wd_ref[:, cols], preferred_element_type=F32)
        lw_out[:, cols] = _sigmoid(zw) * (-EXP_NEG_HALF)
        a = _sigmoid(a0_ref[:, cols] + jnp.dot(xa, wa_ref[:, cols], preferred_element_type=F32))
        g_out[:, cols] = jnp.dot(s_xg, wg_ref[:, cols], preferred_element_type=F32).astype(g_out.dtype)

        kkr = k * kk_ref[:, cols]
        kk = kkr * lax.rsqrt(jnp.maximum(_head_sum(kkr * kkr, ones), 1e-24))
        k2 = k * (1.0 + (a - 1.0) * ka_ref[:, cols])
        r_out[:, cols] = r
        k_out[:, cols] = k2
        v_out[:, cols] = v.astype(v_out.dtype)
        kk_out[:, cols] = kk
        b_out[:, cols] = kk * a
        bonus_out[:, cols] = _head_sum(r * k2 * rk_ref[:, cols], ones) * v


def _sc_gate_kernel(x_ref, sc_state_ref, n1g_ref, w_sc_ref, w_gate_ref, b_gate_ref, conv_ref,
                    w_bsc_ref, ga_out, gbo_out, xb_out, sc_out, c_sc, *, s):
    @pl.when(pl.program_id(1) == 0)
    def _():
        c_sc[...] = sc_state_ref[...]

    rows = x_ref.shape[0]
    xb = _rms_norm(x_ref[...], n1g_ref[...]).astype(BF16)
    xb_out[...] = xb

    def proj(c):
        cs = lambda base: slice(base + COL_CHUNK * c, base + COL_CHUNK * (c + 1))
        sc = [jnp.dot(xb, w_sc_ref[:, cs(base)], preferred_element_type=F32)
              for base in (0, D_MODEL, 2 * D_MODEL)]
        gate = [jnp.dot(xb, w_gate_ref[:, cs(base)], preferred_element_type=F32) + b_gate_ref[:, cs(base)]
                for base in (0, D_MODEL)]
        return sc + gate

    mid = []
    gb = []
    nxt = proj(0)
    for c in range(D_MODEL // COL_CHUNK):
        cur = nxt
        if c + 1 < D_MODEL // COL_CHUNK:
            nxt = proj(c + 1)
        cols = slice(COL_CHUNK * c, COL_CHUNK * (c + 1))
        h, b_g, c_g, pg_a, pg_b = cur
        u = c_g * h
        carry = c_sc[:, cols]
        prev1 = _shift_rows(u, carry[s:], s)
        prev2 = _shift_rows(prev1, carry[:s], s)
        cw = conv_ref[:, cols]
        conv = prev2 * cw[0:1] + prev1 * cw[1:2] + u * cw[2:3]
        mid.append((b_g * conv).astype(BF16))
        new = u[rows - 2 * s:]
        c_sc[:, cols] = new
        sc_out[:, cols] = new
        ga_out[:, cols] = _sigmoid(pg_a)
        gb.append(_sigmoid(pg_b))
    o_b = jnp.dot(jnp.concatenate(mid, axis=1), w_bsc_ref[...], preferred_element_type=F32)
    for c in range(D_MODEL // COL_CHUNK):
        cols = slice(COL_CHUNK * c, COL_CHUNK * (c + 1))
        gbo_out[:, cols] = gb[c] * o_b[:, cols]


def _wkv_kernel(r_ref, lw_ref, k_ref, v_ref, kk_ref, b_ref, s0_ref, y_out, s_out, s_scr, *, C, nq):
    c_idx = pl.program_id(1)

    @pl.when(c_idx == 0)
    def _():
        zh = jnp.zeros((HEAD_SIZE, HEAD_SIZE), F32)
        for q in range(nq):
            for p in range(N_PAIRS):
                s_scr[q * N_PAIRS + p] = jnp.concatenate(
                    [jnp.concatenate([s0_ref[q, 2 * p], zh], axis=1),
                     jnp.concatenate([zh, s0_ref[q, 2 * p + 1]], axis=1)], axis=0)

    m0 =lax.broadcasted_iota(jnp.int32, (C, LANES), 1) < HEAD_SIZE
    gi = lax.broadcasted_iota(jnp.int32, (C, 2 * C), 0)
    gj = lax.broadcasted_iota(jnp.int32, (C, 2 * C), 1)
    gjm = jnp.where(gj >= C, gj - C, gj)
    strict = gjm < gi
    incl = gjm <= gi
    left = gj < C
    eye2 = jnp.where(gjm == gi, 1.0, 0.0)
    bi = lax.broadcasted_iota(jnp.int32, (LANES, LANES), 0) // HEAD_SIZE
    bj = lax.broadcasted_iota(jnp.int32, (LANES, LANES), 1) // HEAD_SIZE
    bd = bi == bj

    def swap_halves(x):
        if 2 * C == LANES:
            return pltpu.roll(x, C, axis=1)
        return jnp.concatenate([x[:, C:], x[:, :C]], axis=1)

    a_p, r_p, v_p, bk, w_last = [], [], [], [], []
    for q in range(nq):
        lw = lw_ref[q]
        cum = _cumsum_rows(lw)
        w_t = jnp.exp(cum)
        w_inv = jnp.exp(-cum)
        a_t = -kk_ref[q] * jnp.exp(cum - lw)
        b_t = b_ref[q] * w_inv
        k_t = k_ref[q] * w_inv
        r_t = r_ref[q] * w_t
        v_q = v_ref[q].astype(F32)
        for p in range(N_PAIRS):
            sl = slice(LANES * p, LANES * (p + 1))
            a_p.append(a_t[:, sl])
            r_p.append(r_t[:, sl])
            v_p.append(v_q[:, sl])
            bk.append(jnp.concatenate([b_t[:, sl], k_t[:, sl]], axis=0))
            w_last.append(w_t[C - 1:C, sl])
    ents = range(nq * N_PAIRS)

    g = [_dot_nt(jnp.concatenate([jnp.where(m0, a_p[e], 0.0), jnp.where(m0, r_p[e], 0.0),
                                  jnp.where(m0, 0.0, a_p[e]), jnp.where(m0, 0.0, r_p[e])], axis=0), bk[e])
         for e in ents]
    s_bd = [s_scr[e] for e in ents]
    s_t = [jnp.transpose(s_bd[e]) for e in ents]
    g1a = [swap_halves(g[e][2 * C:3 * C]) for e in ents]
    def block_diag(x):
        return jnp.concatenate([jnp.where(left, x, 0.0), jnp.where(left, 0.0, x)], axis=0)

    xpow = [jnp.where(strict, jnp.where(left, g[e][:C], g1a[e]), 0.0) for e in ents]
    tinv = [eye2 + xpow[e] for e in ents]
    n = 1
    if 2 * n < C:
        xpow = [_dot(xpow[e], block_diag(xpow[e])) for e in ents]
        n *= 2
    while 2 * n < C:
        z = [_dot(jnp.concatenate([xpow[e], tinv[e]], axis=0), block_diag(xpow[e])) for e in ents]
        xpow = [z[e][:C] for e in ents]
        tinv = [tinv[e] + z[e][C:] for e in ents]
        n *= 2
    if C > 2:
        tinv = [tinv[e] + _dot(tinv[e], block_diag(xpow[e])) for e in ents]
    vm0 = [jnp.where(m0, v_p[e], 0.0) for e in ents]
    vm1 = [jnp.where(m0, 0.0, v_p[e]) for e in ents]
    mak = [jnp.where(strict, jnp.where(left, g1a[e], g[e][:C]), 0.0) for e in ents]
    zmak = jnp.zeros((C, 2 * C), F32)
    sv = [_dot(jnp.concatenate([jnp.concatenate([a_p[e], mak[e]], axis=1),
                                jnp.concatenate([r_p[e], zmak], axis=1)], axis=0),
               jnp.concatenate([s_t[e], vm1[e], vm0[e]], axis=0)) for e in ents]
    rhs = [sv[e][:C] for e in ents]
    rs = [sv[e][C:] for e in ents]
    u = [_dot(tinv[e], jnp.concatenate([jnp.where(m0, rhs[e], 0.0), jnp.where(m0, 0.0, rhs[e])], axis=0))
         for e in ents]
    for e in ents:
        q, p = divmod(e, N_PAIRS)
        um0 = jnp.where(m0, u[e], 0.0)
        um1 = jnp.where(m0, 0.0, u[e])
        g0r = jnp.where(incl, g[e][C:2 * C], 0.0)
        g1r = jnp.where(incl, g[e][3 * C:], 0.0)
        if 2 * C == LANES:
            intra = _dot(jnp.concatenate([g0r, g1r], axis=1),
                         jnp.concatenate([um0, vm0[e], um1, vm1[e]], axis=0))
        else:
            intra = (_dot(g0r, jnp.concatenate([um0, vm0[e]], axis=0))
                     + _dot(g1r, jnp.concatenate([um1, vm1[e]], axis=0)))
        y_out[q, :, LANES * p:LANES * (p + 1)] = rs[e] + intra
    for e in ents:
        upd = _dot_tn(jnp.concatenate([u[e], v_p[e]], axis=0), bk[e])
        s_scr[e] = (s_bd[e] + jnp.where(bd, upd, 0.0)) * w_last[e]

    @pl.when(c_idx == pl.num_programs(1) - 1)
    def _():
        for q in range(nq):
            for p in range(N_PAIRS):
                s_fin = s_scr[q * N_PAIRS + p]
                s_out[q, 2 * p] = s_fin[:HEAD_SIZE, :HEAD_SIZE]
                s_out[q, 2 * p + 1] = s_fin[HEAD_SIZE:, HEAD_SIZE:]


def _wkv_lanes_kernel(r_ref, lw_ref, k_ref, v_ref, kk_ref, b_ref, s0_ref, y_out, s_out,
                      r_t, w_t, k_t, v_t, a_t, b_t, y_t, *, n_t, gsz):
    n_grp = r_ref.shape[0]

    def by_channel(ref, t):
        rows = [ref[g, t * gsz:(t + 1) * gsz, :] for g in range(n_grp)]
        return jnp.transpose(jnp.concatenate(rows, axis=0))

    for t in range(n_t):
        r_t[t] = by_channel(r_ref, t)
        w_t[t] = jnp.exp(by_channel(lw_ref, t))
        k_t[t] = by_channel(k_ref, t)
        v_t[t] = by_channel(v_ref, t)
        a_t[t] = -by_channel(kk_ref, t)
        b_t[t] = by_channel(b_ref, t)

    for hh in range(2):
        ch = slice(hh * HEAD_SIZE, (hh + 1) * HEAD_SIZE)

        def row_step(i, carry):
            s = s0_ref[hh, i]
            for t in range(n_t):
                sa = jnp.sum(s * a_t[t, ch, :], axis=0, keepdims=True)
                v_i = v_t[t, pl.ds(hh * HEAD_SIZE + i, 1), :]
                s = s * w_t[t, ch, :] + sa * b_t[t, ch, :] + v_i * k_t[t, ch, :]
                y_t[t, pl.ds(hh * HEAD_SIZE + i, 1), :] = jnp.sum(s * r_t[t, ch, :], axis=0, keepdims=True)
            s_out[hh, i] = s
            return carry

        lax.fori_loop(0, HEAD_SIZE, row_step, 0, unroll=LANE_ROW_UNROLL)

    for t in range(n_t):
        y = jnp.transpose(y_t[t])
        for g in range(n_grp):
            y_out[g, t * gsz:(t + 1) * gsz, :] = y[g * gsz:(g + 1) * gsz]


def _post_kernel(x_ref, y_ref, bonus_ref, g_ref, ga_ref, gbo_ref, ffn_state_ref, lnx_g_ref, lnx_b_ref,
                 w_br_ref, w_out_ref, n2g_ref, w_up_ref, conv_ref, w_down_ref, fng_ref,
                 out_ref, ffn_out, c_ffn, *, s):
    @pl.when(pl.program_id(1) == 0)
    def _():
        c_ffn[...] = ffn_state_ref[...]

    rows = x_ref.shape[0]
    ones = _head_ones(COL_CHUNK)
    z = []
    for c in range(D_MODEL // COL_CHUNK):
        cols = slice(COL_CHUNK * c, COL_CHUNK * (c + 1))
        y = y_ref[:, cols]
        dev = y - _head_sum(y, ones) * (1.0 / HEAD_SIZE)
        var = _head_sum(dev * dev, ones) * (1.0 / HEAD_SIZE)
        yn = dev * lax.rsqrt(var + GN_EPS) * lnx_g_ref[:, cols] + lnx_b_ref[:, cols]
        z.append(((yn + bonus_ref[:, cols]) * g_ref[:, cols].astype(F32)).astype(BF16))
    o_a = jnp.dot(jnp.concatenate(z, axis=1), w_br_ref[...], preferred_element_type=F32)
    x1 = x_ref[...] + _dot(ga_ref[...] * o_a + gbo_ref[...], w_out_ref[...])

    xb = _rms_norm(x1, n2g_ref[...]).astype(BF16)
    n_chunk = D_FF // FF_CHUNK

    def up_proj(j):
        return [jnp.dot(xb, w_up_ref[:, base + FF_CHUNK * j:base + FF_CHUNK * (j + 1)],
                        preferred_element_type=F32) for base in (0, D_FF)]

    hidden = []
    ups = up_proj(0)
    for j in range(n_chunk):
        cur = ups
        if j + 1 < n_chunk:
            ups = up_proj(j + 1)
        halves = []
        for base, up in zip((0, D_FF), cur):
            cols = slice(base + FF_CHUNK * j, base + FF_CHUNK * (j + 1))
            carry = c_ffn[:, cols]
            prev1 = _shift_rows(up, carry[s:], s)
            prev2 = _shift_rows(prev1, carry[:s], s)
            cw = conv_ref[:, cols]
            halves.append(prev2 * cw[0:1] + prev1 * cw[1:2] + up * cw[2:3])
            new = up[rows - 2 * s:]
            c_ffn[:, cols] = new
            ffn_out[:, cols] = new
        gate, val = halves
        hidden.append((gate * _sigmoid(gate) * val).astype(BF16))
    down = jnp.dot(jnp.concatenate(hidden, axis=1), w_down_ref[...], preferred_element_type=F32)
    out_ref[...] = _rms_norm(x1 + down, fng_ref[...])


def _const_spec(arr):
    nd = arr.ndim
    return pl.BlockSpec(arr.shape, lambda b, t: (0,) * nd, pipeline_mode=pl.Buffered(1))


def _row_spec(rows, cols):
    return pl.BlockSpec((None, rows, cols), lambda b, t: (b, t, 0))


def _state_spec(rows, cols):
    return pl.BlockSpec((None, rows, cols), lambda b, t: (b, 0, 0))


def _params():
    return pltpu.CompilerParams(dimension_semantics=("arbitrary", "arbitrary"),
                                vmem_limit_bytes=VMEM_LIMIT)


def _rwkv_prep(xb, sh_rkv, sh_lora, w, *, rows, s, narrow_vg):
    nb, total, _ = xb.shape
    grid = (nb, total // rows)
    consts = [w["w_rkv"], w["w_lora"], w["mu_rkv"], w["mu_lora"], w["w0"], w["wd"],
              w["a0"], w["wa"], w["wg"], w["k_k"], w["k_a"], w["r_k"]]
    tok = jax.ShapeDtypeStruct((nb, total, D_MODEL), F32)
    vg = jax.ShapeDtypeStruct((nb, total, D_MODEL), BF16 if narrow_vg else F32)
    return pl.pallas_call(
        functools.partial(_rwkv_prep_kernel, s=s),
        grid=grid,
        in_specs=[_row_spec(rows, D_MODEL), _state_spec(s, 3 * D_MODEL), _state_spec(s, D_LORA_PAD)]
                 + [_const_spec(c) for c in consts],
        out_specs=[_row_spec(rows, D_MODEL)] * 8
                  + [_state_spec(s, 3 * D_MODEL), _state_spec(s, D_LORA_PAD)],
        out_shape=[tok, tok, tok, vg, tok, tok, vg, tok] + [jax.ShapeDtypeStruct((nb, s, 3 * D_MODEL), F32),
                               jax.ShapeDtypeStruct((nb, s, D_LORA_PAD), F32)],
        scratch_shapes=[pltpu.VMEM((s, 3 * D_MODEL), F32), pltpu.VMEM((s, D_LORA_PAD), F32)],
        compiler_params=_params(),
        name="rwkv_prep",
    )(xb, sh_rkv, sh_lora, *consts)


def _sc_gate(x, sc_state, w, *, rows, s):
    nb, total, _ = x.shape
    grid = (nb, total // rows)
    consts = [w["norm1_g"], w["w_sc"], w["w_gate"], w["b_gate"], w["conv_sc"], w["w_branch_sc"]]
    tok = jax.ShapeDtypeStruct((nb, total, D_MODEL), F32)
    return pl.pallas_call(
        functools.partial(_sc_gate_kernel, s=s),
        grid=grid,
        in_specs=[_row_spec(rows, D_MODEL), _state_spec(2 * s, D_MODEL)] + [_const_spec(c) for c in consts],
        out_specs=[_row_spec(rows, D_MODEL)] * 3 + [_state_spec(2 * s, D_MODEL)],
        out_shape=[tok] * 2 + [jax.ShapeDtypeStruct((nb, total, D_MODEL), BF16),
                               jax.ShapeDtypeStruct((nb, 2 * s, D_MODEL), F32)],
        scratch_shapes=[pltpu.VMEM((2 * s, D_MODEL), F32)],
        compiler_params=_params(),
        name="sc_gate",
    )(x, sc_state, *consts)


def _wkv(r, lw, k, v, kk, b, s0, *, chunk, nq):
    nb, total, _ = r.shape
    grid = (nb // nq, total // chunk)
    tok_spec = pl.BlockSpec((nq, chunk, D_MODEL), lambda i, c: (i, c, 0))
    st_spec = pl.BlockSpec((nq, N_HEADS, HEAD_SIZE, HEAD_SIZE), lambda i, c: (i, 0, 0, 0))
    return pl.pallas_call(
        functools.partial(_wkv_kernel, C=chunk, nq=nq),
        grid=grid,
        in_specs=[tok_spec] * 6 + [st_spec],
        out_specs=[tok_spec, st_spec],
        out_shape=[jax.ShapeDtypeStruct((nb, total, D_MODEL), F32),
                   jax.ShapeDtypeStruct((nb, N_HEADS, HEAD_SIZE, HEAD_SIZE), F32)],
        scratch_shapes=[pltpu.VMEM((nq * N_PAIRS, LANES, LANES), F32)],
        compiler_params=_params(),
        name="wkv",
    )(r, lw, k, v, kk, b, s0)


def _wkv_lanes(r, lw, k, v, kk, b, s0, *, n_t):
    n_grp, total, _ = r.shape
    batch = s0.shape[-1]
    tok_spec = pl.BlockSpec((n_grp, total, LANES), lambda p: (0, 0, p))
    st_spec = pl.BlockSpec((2, HEAD_SIZE, HEAD_SIZE, batch), lambda p: (p, 0, 0, 0))
    return pl.pallas_call(
        functools.partial(_wkv_lanes_kernel, n_t=n_t, gsz=total // n_t),
        grid=(N_PAIRS,),
        in_specs=[tok_spec] * 6 + [st_spec],
        out_specs=[tok_spec, st_spec],
        out_shape=[jax.ShapeDtypeStruct(r.shape, F32), jax.ShapeDtypeStruct(s0.shape, F32)],
        scratch_shapes=[pltpu.VMEM((n_t, LANES, batch), F32)] * 7,
        compiler_params=pltpu.CompilerParams(dimension_semantics=("arbitrary",),
                                             vmem_limit_bytes=VMEM_LIMIT),
        name="wkv_lanes",
    )(r, lw, k, v, kk, b, s0)


def _post(x, y, bonus, g, ga, gbo, ffn_state, w, *, rows, s):
    nb, total, _ = x.shape
    grid = (nb, total // rows)
    consts = [w["lnx_g"], w["lnx_b"], w["w_branch_rwkv"], w["w_out"], w["norm2_g"], w["w_up"],
              w["conv_ffn"], w["w_down"], w["final_norm_g"]]
    return pl.pallas_call(
        functools.partial(_post_kernel, s=s),
        grid=grid,
        in_specs=[_row_spec(rows, D_MODEL)] * 6 + [_state_spec(2 * s, 2 * D_FF)]
                 + [_const_spec(c) for c in consts],
        out_specs=[_row_spec(rows, D_MODEL), _state_spec(2 * s, 2 * D_FF)],
        out_shape=[jax.ShapeDtypeStruct((nb, total, D_MODEL), F32),
                   jax.ShapeDtypeStruct((nb, 2 * s, 2 * D_FF), F32)],
        scratch_shapes=[pltpu.VMEM((2 * s, 2 * D_FF), F32)],
        compiler_params=_params(),
        name="post",
    )(x, y, bonus, g, ga, gbo, ffn_state, *consts)


def _pad_lora_cols(a):
    pad = lambda t, n: jnp.pad(t, [(0, 0)] * (t.ndim - 1) + [(0, n - t.shape[-1])])
    return jnp.concatenate([pad(a[..., :64], 128), pad(a[..., 64:128], 128), pad(a[..., 128:], 256)], axis=-1)


def _unpad_lora_cols(a):
    return jnp.concatenate([a[..., :64], a[..., 128:192], a[..., 256:256 + D_GATE_LORA]], axis=-1)


def _prep_weights(norm1_g, w_in, b_gate, mu_shift, w0, w_decay_up, a0, w_aaa_up, w_gate_up, k_k, k_a,
                  r_k, lnx_g, lnx_b, w_branch_rwkv, w_branch_sc, conv_sc, w_out, norm2_g, w_up, conv_ffn,
                  w_down, final_norm_g):
    row = lambda t: t.reshape(1, -1).astype(F32)
    d3 = 3 * D_MODEL
    n_lora = D_DECAY_LORA + D_AAA_LORA + D_GATE_LORA
    pad_rows = lambda t, n: jnp.pad(t, [(0, n - t.shape[0]), (0, 0)])
    return {
        "norm1_g": row(norm1_g),
        "w_rkv": w_in[:, :d3].astype(BF16),
        "w_lora": _pad_lora_cols(w_in[:, d3:d3 + n_lora]).astype(BF16),
        "w_sc": w_in[:, d3 + n_lora:2 * d3 + n_lora].astype(BF16),
        "w_gate": w_in[:, 2 * d3 + n_lora:].astype(BF16),
        "b_gate": row(b_gate),
        "mu_rkv": row(mu_shift[:d3]),
        "mu_lora": _pad_lora_cols(row(mu_shift[d3:])),
        "w0": row(w0),
        "wd": pad_rows(w_decay_up, 128).astype(BF16),
        "a0": row(a0),
        "wa": pad_rows(w_aaa_up, 128).astype(BF16),
        "wg": pad_rows(w_gate_up, 256).astype(BF16),
        "k_k": row(k_k), "k_a": row(k_a), "r_k": row(r_k),
        "lnx_g": row(lnx_g), "lnx_b": row(lnx_b),
        "w_branch_rwkv": w_branch_rwkv.astype(BF16),
        "w_branch_sc": w_branch_sc.astype(BF16),
        "conv_sc": conv_sc.astype(F32),
        "w_out": w_out.astype(BF16),
        "norm2_g": row(norm2_g),
        "w_up": w_up.astype(BF16),
        "conv_ffn": conv_ffn.astype(F32),
        "w_down": w_down.astype(BF16),
        "final_norm_g": row(final_norm_g),
    }


def _layer(x, s_wkv, sh_rkv, sh_lora, sc_state, ffn_state, w, *, rows, s, wkv, sc_rows=None, narrow_vg=False):
    ga, gbo, xb, sc_new = _sc_gate(x, sc_state, w, rows=sc_rows or rows, s=s)
    r, lw, k, v, kk, b, g, bonus, shr, shl = _rwkv_prep(xb, sh_rkv, sh_lora, w, rows=rows, s=s, narrow_vg=narrow_vg)
    y, s_new = wkv(r, lw, k, v, kk, b, s_wkv)
    out, ffn_new = _post(x, y, bonus, g, ga, gbo, ffn_state, w, rows=rows, s=s)
    return out, s_new, shr, shl, sc_new, ffn_new


def kernel(x_prompt, x_sample, state_wkv, state_shift, state_sc_conv, state_ffn_conv, meta_tokens,
           norm1_g, w_in, b_gate, mu_shift, w0, w_decay_up, a0, w_aaa_up, w_gate_up, k_k, k_a, r_k,
           lnx_g, lnx_b, w_branch_rwkv, w_branch_sc, conv_sc, w_out, norm2_g, w_up, conv_ffn, w_down,
           final_norm_g):
    w = _prep_weights(norm1_g[0], w_in[0], b_gate[0], mu_shift[0], w0[0], w_decay_up[0], a0[0],
                      w_aaa_up[0], w_gate_up[0], k_k[0], k_a[0], r_k[0], lnx_g[0], lnx_b[0],
                      w_branch_rwkv[0], w_branch_sc[0], conv_sc[0], w_out[0], norm2_g[0], w_up[0],
                      conv_ffn[0], w_down[0], final_norm_g)
    d3 = 3 * D_MODEL

    bp, seq, _ = x_prompt.shape
    zeros = lambda *shape: jnp.zeros(shape, F32)
    _, m_wkv, m_shr, m_shl, m_sc, m_ffn = _layer(
        meta_tokens.astype(F32)[None], zeros(1, N_HEADS, HEAD_SIZE, HEAD_SIZE), zeros(1, 1, d3),
        zeros(1, 1, D_LORA_PAD), zeros(1, 2, D_MODEL), zeros(1, 2, 2 * D_FF), w,
        rows=N_META, s=1, wkv=functools.partial(_wkv, chunk=N_META, nq=1))

    rep = lambda t: jnp.broadcast_to(t, (bp,) + t.shape[1:])
    y_prompt, p_wkv, p_shr, p_shl, p_sc, p_ffn = _layer(
        x_prompt, rep(m_wkv), rep(m_shr), rep(m_shl), rep(m_sc), rep(m_ffn), w,
        rows=PROMPT_ROWS, s=1, wkv=functools.partial(_wkv, chunk=WKV_CHUNK, nq=WKV_SEQS),
        sc_rows=PROMPT_SC_ROWS, narrow_vg=True)

    bs, ts, _ = x_sample.shape
    n_grp = SAMPLE_GROUPS
    gsz = bs // n_grp

    def to_rows(t):
        n, c = t.shape[1:]
        return t.reshape(n_grp, gsz, n, c).transpose(0, 2, 1, 3).reshape(n_grp, n * gsz, c)

    sh = state_shift[0].reshape(n_grp, gsz, -1)
    y_s, s_wkv, s_shr, s_shl, s_sc, s_ffn = _layer(
        to_rows(x_sample), state_wkv[0].transpose(1, 2, 3, 0), sh[..., :d3], _pad_lora_cols(sh[..., d3:]),
        to_rows(state_sc_conv[0]), to_rows(state_ffn_conv[0]), w,
        rows=ts * gsz, s=gsz, wkv=functools.partial(_wkv_lanes, n_t=ts))
    s_wkv = s_wkv.transpose(3, 0, 1, 2)

    def rows_to_batch(t, n):
        c = t.shape[-1]
        return t.reshape(n_grp, n, gsz, c).transpose(0, 2, 1, 3).reshape(bs, n, c)

    y_sample = rows_to_batch(y_s, ts)
    shift_p = jnp.concatenate([p_shr[:, 0], _unpad_lora_cols(p_shl[:, 0])], axis=-1)
    shift_s = jnp.concatenate([rows_to_batch(s_shr, 1)[:, 0], _unpad_lora_cols(rows_to_batch(s_shl, 1)[:, 0])],
                              axis=-1)
    return (y_prompt, y_sample,
            p_wkv[None], s_wkv[None],
            shift_p[None], shift_s[None],
            p_sc[None], rows_to_batch(s_sc, 2)[None],
            p_ffn[None], rows_to_batch(s_ffn, 2)[None])
```

```python
import functools

import jax
import jax.numpy as jnp
from jax import lax
from jax.experimental import pallas as pl
from jax.experimental.pallas import tpu as pltpu

D_MODEL = 1024
N_META = 16
HEAD_SIZE = 64
N_HEADS = D_MODEL // HEAD_SIZE
LANES = 128
N_PAIRS = D_MODEL // LANES
D_DECAY_LORA = 64
D_AAA_LORA = 64
D_GATE_LORA = 160
D_LORA_PAD = 512
D_FF = 2816
FF_CHUNK = 256
COL_CHUNK = 256
CONV_W = 3
NEG_LOG2_E = -1.4426950408889634
EXP_NEG_HALF = 0.6065306597126334
RMS_EPS = 1e-6
GN_EPS = 64e-5
VMEM_LIMIT = 60 * 1024 * 1024

PROMPT_ROWS = 512
PROMPT_SC_ROWS = 1024
WKV_CHUNK = 64
WKV_SEQS = 4
SAMPLE_GROUPS = 2
LANE_ROW_UNROLL = 16

F32 = jnp.float32
BF16 = jnp.bfloat16


def _dot(a, b):
    return jnp.dot(a.astype(BF16), b.astype(BF16), preferred_element_type=F32)


def _dot_nt(a, b):
    return lax.dot_general(a.astype(BF16), b.astype(BF16), (((1,), (1,)), ((), ())),
                           preferred_element_type=F32)


def _dot_tn(a, b):
    return lax.dot_general(a.astype(BF16), b.astype(BF16), (((0,), (0,)), ((), ())),
                           preferred_element_type=F32)


def _head_ones(n):
    r = lax.broadcasted_iota(jnp.int32, (n, n), 0) // HEAD_SIZE
    c = lax.broadcasted_iota(jnp.int32, (n, n), 1) // HEAD_SIZE
    return jnp.where(r == c, 1.0, 0.0).astype(BF16)


def _head_sum(x, ones):
    return jnp.dot(x.astype(BF16), ones, preferred_element_type=F32)


def _rms_norm(x, g):
    return x * lax.rsqrt(jnp.mean(x * x, axis=-1, keepdims=True) + RMS_EPS) * g


def _sigmoid(x):
    return 1.0 / (1.0 + jnp.exp2(x * NEG_LOG2_E))


def _shift_rows(cur, carry, s):
    rows = cur.shape[0]
    if s % 8 == 0:
        return jnp.concatenate([carry, cur[:rows - s]], axis=0)
    assert s == 1
    rolled = pltpu.roll(cur, 1, axis=0)
    row = lax.broadcasted_iota(jnp.int32, cur.shape, 0)
    return jnp.where(row == 0, carry, rolled)


def _cumsum_rows(x):
    rows, cols = x.shape
    row = lax.broadcasted_iota(jnp.int32, x.shape, 0)
    sh = 1
    while sh < rows:
        if sh % 8 == 0:
            shifted = jnp.concatenate([jnp.zeros((sh, cols), x.dtype), x[:rows - sh]], axis=0)
        else:
            shifted = jnp.where(row >= sh, pltpu.roll(x, sh, axis=0), 0.0)
        x = x + shifted
        sh *= 2
    return x


def _rwkv_prep_kernel(xb_ref, sh_rkv_ref, sh_lora_ref, w_rkv_ref, w_lora_ref, mu_rkv_ref,
                      mu_lora_ref, w0_ref, wd_ref, a0_ref, wa_ref, wg_ref, kk_ref, ka_ref, rk_ref,
                      r_out, lw_out, k_out, v_out, kk_out, b_out, g_out, bonus_out, shr_out, shl_out,
                      c_rkv, c_lora, *, s):
    @pl.when(pl.program_id(1) == 0)
    def _():
        c_rkv[...] = sh_rkv_ref[...]
        c_lora[...] = sh_lora_ref[...]

    rows = xb_ref.shape[0]
    xb = xb_ref[...]
    p_lora = jnp.dot(xb, w_lora_ref[...], preferred_element_type=F32)
    prev_lora = _shift_rows(p_lora, c_lora[...], s)
    xl = p_lora + (prev_lora - p_lora) * mu_lora_ref[...]
    new_lora = p_lora[rows - s:]
    c_lora[...] = new_lora
    shl_out[...] = new_lora
    t_xw = jnp.tanh(xl[:, :128]).astype(BF16)
    xa = xl[:, 128:256].astype(BF16)
    s_xg = _sigmoid(xl[:, 256:]).astype(BF16)

    def proj(c):
        return [jnp.dot(xb, w_rkv_ref[:, base + COL_CHUNK * c:base + COL_CHUNK * (c + 1)],
                        preferred_element_type=F32) for base in (0, D_MODEL, 2 * D_MODEL)]

    ones = _head_ones(COL_CHUNK)
    nxt = proj(0)
    for c in range(D_MODEL // COL_CHUNK):
        cur = nxt
        if c + 1 < D_MODEL // COL_CHUNK:
            nxt = proj(c + 1)
        cols = slice(COL_CHUNK * c, COL_CHUNK * (c + 1))
        shifted = []
        for part, p in enumerate(cur):
            pcols = slice(part * D_MODEL + COL_CHUNK * c, part * D_MODEL + COL_CHUNK * (c + 1))
            prev = _shift_rows(p, c_rkv[:, pcols], s)
            shifted.append(p + (prev - p) * mu_rkv_ref[:, pcols])
            new = p[rows - s:]
            c_rkv[:, pcols] = new
            shr_out[:, pcols] = new
        r, k, v = shifted

        zw = w0_ref[:, cols] + jnp.dot(t_xw, wd_ref[:, cols], preferred_element_type=F32)
        lw_out[:, cols] = _sigmoid(zw) * (-EXP_NEG_HALF)
        a = _sigmoid(a0_ref[:, cols] + jnp.dot(xa, wa_ref[:, cols], preferred_element_type=F32))
        g_out[:, cols] = jnp.dot(s_xg, wg_ref[:, cols], preferred_element_type=F32).astype(g_out.dtype)

        kkr = k * kk_ref[:, cols]
        kk = kkr * lax.rsqrt(jnp.maximum(_head_sum(kkr * kkr, ones), 1e-24))
        k2 = k * (1.0 + (a - 1.0) * ka_ref[:, cols])
        r_out[:, cols] = r
        k_out[:, cols] = k2
        v_out[:, cols] = v.astype(v_out.dtype)
        kk_out[:, cols] = kk
        b_out[:, cols] = kk * a
        bonus_out[:, cols] = _head_sum(r * k2 * rk_ref[:, cols], ones) * v


def _sc_gate_kernel(x_ref, sc_state_ref, n1g_ref, w_sc_ref, w_gate_ref, b_gate_ref, conv_ref,
                    w_bsc_ref, ga_out, gbo_out, xb_out, sc_out, c_sc, *, s):
    @pl.when(pl.program_id(1) == 0)
    def _():
        c_sc[...] = sc_state_ref[...]

    rows = x_ref.shape[0]
    xb = _rms_norm(x_ref[...], n1g_ref[...]).astype(BF16)
    xb_out[...] = xb

    def proj(c):
        cs = lambda base: slice(base + COL_CHUNK * c, base + COL_CHUNK * (c + 1))
        sc = [jnp.dot(xb, w_sc_ref[:, cs(base)], preferred_element_type=F32)
              for base in (0, D_MODEL, 2 * D_MODEL)]
        gate = [jnp.dot(xb, w_gate_ref[:, cs(base)], preferred_element_type=F32) + b_gate_ref[:, cs(base)]
                for base in (0, D_MODEL)]
        return sc + gate

    mid = []
    gb = []
    nxt = proj(0)
    for c in range(D_MODEL // COL_CHUNK):
        cur = nxt
        if c + 1 < D_MODEL // COL_CHUNK:
            nxt = proj(c + 1)
        cols = slice(COL_CHUNK * c, COL_CHUNK * (c + 1))
        h, b_g, c_g, pg_a, pg_b = cur
        u = c_g * h
        carry = c_sc[:, cols]
        prev1 = _shift_rows(u, carry[s:], s)
        prev2 = _shift_rows(prev1, carry[:s], s)
        cw = conv_ref[:, cols]
        conv = prev2 * cw[0:1] + prev1 * cw[1:2] + u * cw[2:3]
        mid.append((b_g * conv).astype(BF16))
        new = u[rows - 2 * s:]
        c_sc[:, cols] = new
        sc_out[:, cols] = new
        ga_out[:, cols] = _sigmoid(pg_a)
        gb.append(_sigmoid(pg_b))
    o_b = jnp.dot(jnp.concatenate(mid, axis=1), w_bsc_ref[...], preferred_element_type=F32)
    for c in range(D_MODEL // COL_CHUNK):
        cols = slice(COL_CHUNK * c, COL_CHUNK * (c + 1))
        gbo_out[:, cols] = gb[c] * o_b[:, cols]


def _wkv_kernel(r_ref, lw_ref, k_ref, v_ref, kk_ref, b_ref, s0_ref, y_out, s_out, s_scr, *, C, nq, shared_s0):
    c_idx = pl.program_id(1)

    @pl.when(c_idx == 0)
    def _():
        zh = jnp.zeros((HEAD_SIZE, HEAD_SIZE), F32)
        for q in range(nq):
            qs = 0 if shared_s0 else q
            for p in range(N_PAIRS):
                s_scr[q * N_PAIRS + p] = jnp.concatenate(
                    [jnp.concatenate([s0_ref[qs, 2 * p], zh], axis=1),
                     jnp.concatenate([zh, s0_ref[qs, 2 * p + 1]], axis=1)], axis=0)

    m0 =lax.broadcasted_iota(jnp.int32, (C, LANES), 1) < HEAD_SIZE
    gi = lax.broadcasted_iota(jnp.int32, (C, 2 * C), 0)
    gj = lax.broadcasted_iota(jnp.int32, (C, 2 * C), 1)
    gjm = jnp.where(gj >= C, gj - C, gj)
    strict = gjm < gi
    incl = gjm <= gi
    left = gj < C
    eye2 = jnp.where(gjm == gi, 1.0, 0.0)
    bi = lax.broadcasted_iota(jnp.int32, (LANES, LANES), 0) // HEAD_SIZE
    bj = lax.broadcasted_iota(jnp.int32, (LANES, LANES), 1) // HEAD_SIZE
    bd = bi == bj

    def swap_halves(x):
        if 2 * C == LANES:
            return pltpu.roll(x, C, axis=1)
        return jnp.concatenate([x[:, C:], x[:, :C]], axis=1)

    a_p, r_p, v_p, bk, w_last = [], [], [], [], []
    for q in range(nq):
        lw = lw_ref[q]
        cum = _cumsum_rows(lw)
        w_t = jnp.exp(cum)
        w_inv = jnp.exp(-cum)
        a_t = -kk_ref[q] * jnp.exp(cum - lw)
        b_t = b_ref[q] * w_inv
        k_t = k_ref[q] * w_inv
        r_t = r_ref[q] * w_t
        v_q = v_ref[q].astype(F32)
        for p in range(N_PAIRS):
            sl = slice(LANES * p, LANES * (p + 1))
            a_p.append(a_t[:, sl])
            r_p.append(r_t[:, sl])
            v_p.append(v_q[:, sl])
            bk.append(jnp.concatenate([b_t[:, sl], k_t[:, sl]], axis=0))
            w_last.append(w_t[C - 1:C, sl])
    ents = range(nq * N_PAIRS)

    g = [_dot_nt(jnp.concatenate([jnp.where(m0, a_p[e], 0.0), jnp.where(m0, r_p[e], 0.0),
                                  jnp.where(m0, 0.0, a_p[e]), jnp.where(m0, 0.0, r_p[e])], axis=0), bk[e])
         for e in ents]
    s_bd = [s_scr[e] for e in ents]
    s_t = [jnp.transpose(s_bd[e]) for e in ents]
    g1a = [swap_halves(g[e][2 * C:3 * C]) for e in ents]
    def block_diag(x):
        return jnp.concatenate([jnp.where(left, x, 0.0), jnp.where(left, 0.0, x)], axis=0)

    xpow = [jnp.where(strict, jnp.where(left, g[e][:C], g1a[e]), 0.0) for e in ents]
    tinv = [eye2 + xpow[e] for e in ents]
    n = 1
    if 2 * n < C:
        xpow = [_dot(xpow[e], block_diag(xpow[e])) for e in ents]
        n *= 2
    while 2 * n < C:
        z = [_dot(jnp.concatenate([xpow[e], tinv[e]], axis=0), block_diag(xpow[e])) for e in ents]
        xpow = [z[e][:C] for e in ents]
        tinv = [tinv[e] + z[e][C:] for e in ents]
        n *= 2
    if C > 2:
        tinv = [tinv[e] + _dot(tinv[e], block_diag(xpow[e])) for e in ents]
    vm0 = [jnp.where(m0, v_p[e], 0.0) for e in ents]
    vm1 = [jnp.where(m0, 0.0, v_p[e]) for e in ents]
    mak = [jnp.where(strict, jnp.where(left, g1a[e], g[e][:C]), 0.0) for e in ents]
    zmak = jnp.zeros((C, 2 * C), F32)
    sv = [_dot(jnp.concatenate([jnp.concatenate([a_p[e], mak[e]], axis=1),
                                jnp.concatenate([r_p[e], zmak], axis=1)], axis=0),
               jnp.concatenate([s_t[e], vm1[e], vm0[e]], axis=0)) for e in ents]
    rhs = [sv[e][:C] for e in ents]
    rs = [sv[e][C:] for e in ents]
    u = [_dot(tinv[e], jnp.concatenate([jnp.where(m0, rhs[e], 0.0), jnp.where(m0, 0.0, rhs[e])], axis=0))
         for e in ents]
    for e in ents:
        q, p = divmod(e, N_PAIRS)
        um0 = jnp.where(m0, u[e], 0.0)
        um1 = jnp.where(m0, 0.0, u[e])
        g0r = jnp.where(incl, g[e][C:2 * C], 0.0)
        g1r = jnp.where(incl, g[e][3 * C:], 0.0)
        if 2 * C == LANES:
            intra = _dot(jnp.concatenate([g0r, g1r], axis=1),
                         jnp.concatenate([um0, vm0[e], um1, vm1[e]], axis=0))
        else:
            intra = (_dot(g0r, jnp.concatenate([um0, vm0[e]], axis=0))
                     + _dot(g1r, jnp.concatenate([um1, vm1[e]], axis=0)))
        y_out[q, :, LANES * p:LANES * (p + 1)] = rs[e] + intra
    for e in ents:
        upd = _dot_tn(jnp.concatenate([u[e], v_p[e]], axis=0), bk[e])
        s_scr[e] = (s_bd[e] + jnp.where(bd, upd, 0.0)) * w_last[e]

    @pl.when(c_idx == pl.num_programs(1) - 1)
    def _():
        for q in range(nq):
            for p in range(N_PAIRS):
                s_fin = s_scr[q * N_PAIRS + p]
                s_out[q, 2 * p] = s_fin[:HEAD_SIZE, :HEAD_SIZE]
                s_out[q, 2 * p + 1] = s_fin[HEAD_SIZE:, HEAD_SIZE:]


def _wkv_lanes_kernel(r_ref, lw_ref, k_ref, v_ref, kk_ref, b_ref, s0_ref, y_out, s_out,
                      r_t, w_t, k_t, v_t, a_t, b_t, y_t, *, n_t, gsz):
    n_grp = r_ref.shape[0]

    def by_channel(ref, t):
        rows = [ref[g, t * gsz:(t + 1) * gsz, :] for g in range(n_grp)]
        return jnp.transpose(jnp.concatenate(rows, axis=0))

    for t in range(n_t):
        r_t[t] = by_channel(r_ref, t)
        w_t[t] = jnp.exp(by_channel(lw_ref, t))
        k_t[t] = by_channel(k_ref, t)
        v_t[t] = by_channel(v_ref, t)
        a_t[t] = -by_channel(kk_ref, t)
        b_t[t] = by_channel(b_ref, t)

    for hh in range(2):
        ch = slice(hh * HEAD_SIZE, (hh + 1) * HEAD_SIZE)

        def row_step(i, carry):
            s = s0_ref[hh, i]
            for t in range(n_t):
                sa = jnp.sum(s * a_t[t, ch, :], axis=0, keepdims=True)
                v_i = v_t[t, pl.ds(hh * HEAD_SIZE + i, 1), :]
                s = s * w_t[t, ch, :] + sa * b_t[t, ch, :] + v_i * k_t[t, ch, :]
                y_t[t, pl.ds(hh * HEAD_SIZE + i, 1), :] = jnp.sum(s * r_t[t, ch, :], axis=0, keepdims=True)
            s_out[hh, i] = s
            return carry

        lax.fori_loop(0, HEAD_SIZE, row_step, 0, unroll=LANE_ROW_UNROLL)

    for t in range(n_t):
        y = jnp.transpose(y_t[t])
        for g in range(n_grp):
            y_out[g, t * gsz:(t + 1) * gsz, :] = y[g * gsz:(g + 1) * gsz]


def _post_kernel(x_ref, y_ref, bonus_ref, g_ref, ga_ref, gbo_ref, ffn_state_ref, lnx_g_ref, lnx_b_ref,
                 w_br_ref, w_out_ref, n2g_ref, w_up_ref, conv_ref, w_down_ref, fng_ref,
                 out_ref, ffn_out, c_ffn, *, s):
    @pl.when(pl.program_id(1) == 0)
    def _():
        c_ffn[...] = ffn_state_ref[...]

    rows = x_ref.shape[0]
    ones = _head_ones(COL_CHUNK)
    z = []
    for c in range(D_MODEL // COL_CHUNK):
        cols = slice(COL_CHUNK * c, COL_CHUNK * (c + 1))
        y = y_ref[:, cols]
        dev = y - _head_sum(y, ones) * (1.0 / HEAD_SIZE)
        var = _head_sum(dev * dev, ones) * (1.0 / HEAD_SIZE)
        yn = dev * lax.rsqrt(var + GN_EPS) * lnx_g_ref[:, cols] + lnx_b_ref[:, cols]
        z.append(((yn + bonus_ref[:, cols]) * g_ref[:, cols].astype(F32)).astype(BF16))
    o_a = jnp.dot(jnp.concatenate(z, axis=1), w_br_ref[...], preferred_element_type=F32)
    x1 = x_ref[...] + _dot(ga_ref[...] * o_a + gbo_ref[...], w_out_ref[...])

    xb = _rms_norm(x1, n2g_ref[...]).astype(BF16)
    n_chunk = D_FF // FF_CHUNK

    def up_proj(j):
        return [jnp.dot(xb, w_up_ref[:, base + FF_CHUNK * j:base + FF_CHUNK * (j + 1)],
                        preferred_element_type=F32) for base in (0, D_FF)]

    hidden = []
    ups = up_proj(0)
    for j in range(n_chunk):
        cur = ups
        if j + 1 < n_chunk:
            ups = up_proj(j + 1)
        halves = []
        for base, up in zip((0, D_FF), cur):
            cols = slice(base + FF_CHUNK * j, base + FF_CHUNK * (j + 1))
            carry = c_ffn[:, cols]
            prev1 = _shift_rows(up, carry[s:], s)
            prev2 = _shift_rows(prev1, carry[:s], s)
            cw = conv_ref[:, cols]
            halves.append(prev2 * cw[0:1] + prev1 * cw[1:2] + up * cw[2:3])
            new = up[rows - 2 * s:]
            c_ffn[:, cols] = new
            ffn_out[:, cols] = new
        gate, val = halves
        hidden.append((gate * _sigmoid(gate) * val).astype(BF16))
    down = jnp.dot(jnp.concatenate(hidden, axis=1), w_down_ref[...], preferred_element_type=F32)
    out_ref[...] = _rms_norm(x1 + down, fng_ref[...])


def _const_spec(arr):
    nd = arr.ndim
    return pl.BlockSpec(arr.shape, lambda b, t: (0,) * nd, pipeline_mode=pl.Buffered(1))


def _row_spec(rows, cols):
    return pl.BlockSpec((None, rows, cols), lambda b, t: (b, t, 0))


def _state_spec(rows, cols, state=None):
    if state is not None and state.shape[0] == 1:
        return pl.BlockSpec((None, rows, cols), lambda b, t: (0, 0, 0))
    return pl.BlockSpec((None, rows, cols), lambda b, t: (b, 0, 0))


def _params():
    return pltpu.CompilerParams(dimension_semantics=("arbitrary", "arbitrary"),
                                vmem_limit_bytes=VMEM_LIMIT)


def _rwkv_prep(xb, sh_rkv, sh_lora, w, *, rows, s, narrow_vg):
    nb, total, _ = xb.shape
    grid = (nb, total // rows)
    consts = [w["w_rkv"], w["w_lora"], w["mu_rkv"], w["mu_lora"], w["w0"], w["wd"],
              w["a0"], w["wa"], w["wg"], w["k_k"], w["k_a"], w["r_k"]]
    tok = jax.ShapeDtypeStruct((nb, total, D_MODEL), F32)
    vg = jax.ShapeDtypeStruct((nb, total, D_MODEL), BF16 if narrow_vg else F32)
    return pl.pallas_call(
        functools.partial(_rwkv_prep_kernel, s=s),
        grid=grid,
        in_specs=[_row_spec(rows, D_MODEL), _state_spec(s, 3 * D_MODEL, sh_rkv), _state_spec(s, D_LORA_PAD, sh_lora)]
                 + [_const_spec(c) for c in consts],
        out_specs=[_row_spec(rows, D_MODEL)] * 8
                  + [_state_spec(s, 3 * D_MODEL), _state_spec(s, D_LORA_PAD)],
        out_shape=[tok, tok, tok, vg, tok, tok, vg, tok] + [jax.ShapeDtypeStruct((nb, s, 3 * D_MODEL), F32),
                               jax.ShapeDtypeStruct((nb, s, D_LORA_PAD), F32)],
        scratch_shapes=[pltpu.VMEM((s, 3 * D_MODEL), F32), pltpu.VMEM((s, D_LORA_PAD), F32)],
        compiler_params=_params(),
        name="rwkv_prep",
    )(xb, sh_rkv, sh_lora, *consts)


def _sc_gate(x, sc_state, w, *, rows, s):
    nb, total, _ = x.shape
    grid = (nb, total // rows)
    consts = [w["norm1_g"], w["w_sc"], w["w_gate"], w["b_gate"], w["conv_sc"], w["w_branch_sc"]]
    tok = jax.ShapeDtypeStruct((nb, total, D_MODEL), F32)
    return pl.pallas_call(
        functools.partial(_sc_gate_kernel, s=s),
        grid=grid,
        in_specs=[_row_spec(rows, D_MODEL), _state_spec(2 * s, D_MODEL, sc_state)] + [_const_spec(c) for c in consts],
        out_specs=[_row_spec(rows, D_MODEL)] * 3 + [_state_spec(2 * s, D_MODEL)],
        out_shape=[tok] * 2 + [jax.ShapeDtypeStruct((nb, total, D_MODEL), BF16),
                               jax.ShapeDtypeStruct((nb, 2 * s, D_MODEL), F32)],
        scratch_shapes=[pltpu.VMEM((2 * s, D_MODEL), F32)],
        compiler_params=_params(),
        name="sc_gate",
    )(x, sc_state, *consts)


def _wkv(r, lw, k, v, kk, b, s0, *, chunk, nq):
    nb, total, _ = r.shape
    grid = (nb // nq, total // chunk)
    tok_spec = pl.BlockSpec((nq, chunk, D_MODEL), lambda i, c: (i, c, 0))
    st_spec = pl.BlockSpec((nq, N_HEADS, HEAD_SIZE, HEAD_SIZE), lambda i, c: (i, 0, 0, 0))
    shared_s0 = s0.shape[0] == 1 and nb > 1
    s0_spec = pl.BlockSpec((1, N_HEADS, HEAD_SIZE, HEAD_SIZE), lambda i, c: (0, 0, 0, 0)) if shared_s0 else st_spec
    return pl.pallas_call(
        functools.partial(_wkv_kernel, C=chunk, nq=nq, shared_s0=shared_s0),
        grid=grid,
        in_specs=[tok_spec] * 6 + [s0_spec],
        out_specs=[tok_spec, st_spec],
        out_shape=[jax.ShapeDtypeStruct((nb, total, D_MODEL), F32),
                   jax.ShapeDtypeStruct((nb, N_HEADS, HEAD_SIZE, HEAD_SIZE), F32)],
        scratch_shapes=[pltpu.VMEM((nq * N_PAIRS, LANES, LANES), F32)],
        compiler_params=_params(),
        name="wkv",
    )(r, lw, k, v, kk, b, s0)


def _wkv_lanes(r, lw, k, v, kk, b, s0, *, n_t):
    n_grp, total, _ = r.shape
    batch = s0.shape[-1]
    tok_spec = pl.BlockSpec((n_grp, total, LANES), lambda p: (0, 0, p))
    st_spec = pl.BlockSpec((2, HEAD_SIZE, HEAD_SIZE, batch), lambda p: (p, 0, 0, 0))
    return pl.pallas_call(
        functools.partial(_wkv_lanes_kernel, n_t=n_t, gsz=total // n_t),
        grid=(N_PAIRS,),
        in_specs=[tok_spec] * 6 + [st_spec],
        out_specs=[tok_spec, st_spec],
        out_shape=[jax.ShapeDtypeStruct(r.shape, F32), jax.ShapeDtypeStruct(s0.shape, F32)],
        scratch_shapes=[pltpu.VMEM((n_t, LANES, batch), F32)] * 7,
        compiler_params=pltpu.CompilerParams(dimension_semantics=("arbitrary",),
                                             vmem_limit_bytes=VMEM_LIMIT),
        name="wkv_lanes",
    )(r, lw, k, v, kk, b, s0)


def _post(x, y, bonus, g, ga, gbo, ffn_state, w, *, rows, s):
    nb, total, _ = x.shape
    grid = (nb, total // rows)
    consts = [w["lnx_g"], w["lnx_b"], w["w_branch_rwkv"], w["w_out"], w["norm2_g"], w["w_up"],
              w["conv_ffn"], w["w_down"], w["final_norm_g"]]
    return pl.pallas_call(
        functools.partial(_post_kernel, s=s),
        grid=grid,
        in_specs=[_row_spec(rows, D_MODEL)] * 6 + [_state_spec(2 * s, 2 * D_FF, ffn_state)]
                 + [_const_spec(c) for c in consts],
        out_specs=[_row_spec(rows, D_MODEL), _state_spec(2 * s, 2 * D_FF)],
        out_shape=[jax.ShapeDtypeStruct((nb, total, D_MODEL), F32),
                   jax.ShapeDtypeStruct((nb, 2 * s, 2 * D_FF), F32)],
        scratch_shapes=[pltpu.VMEM((2 * s, 2 * D_FF), F32)],
        compiler_params=_params(),
        name="post",
    )(x, y, bonus, g, ga, gbo, ffn_state, *consts)


def _pad_lora_cols(a):
    pad = lambda t, n: jnp.pad(t, [(0, 0)] * (t.ndim - 1) + [(0, n - t.shape[-1])])
    return jnp.concatenate([pad(a[..., :64], 128), pad(a[..., 64:128], 128), pad(a[..., 128:], 256)], axis=-1)


def _unpad_lora_cols(a):
    return jnp.concatenate([a[..., :64], a[..., 128:192], a[..., 256:256 + D_GATE_LORA]], axis=-1)


def _prep_weights(norm1_g, w_in, b_gate, mu_shift, w0, w_decay_up, a0, w_aaa_up, w_gate_up, k_k, k_a,
                  r_k, lnx_g, lnx_b, w_branch_rwkv, w_branch_sc, conv_sc, w_out, norm2_g, w_up, conv_ffn,
                  w_down, final_norm_g):
    row = lambda t: t.reshape(1, -1).astype(F32)
    d3 = 3 * D_MODEL
    n_lora = D_DECAY_LORA + D_AAA_LORA + D_GATE_LORA
    pad_rows = lambda t, n: jnp.pad(t, [(0, n - t.shape[0]), (0, 0)])
    return {
        "norm1_g": row(norm1_g),
        "w_rkv": w_in[:, :d3].astype(BF16),
        "w_lora": _pad_lora_cols(w_in[:, d3:d3 + n_lora]).astype(BF16),
        "w_sc": w_in[:, d3 + n_lora:2 * d3 + n_lora].astype(BF16),
        "w_gate": w_in[:, 2 * d3 + n_lora:].astype(BF16),
        "b_gate": row(b_gate),
        "mu_rkv": row(mu_shift[:d3]),
        "mu_lora": _pad_lora_cols(row(mu_shift[d3:])),
        "w0": row(w0),
        "wd": pad_rows(w_decay_up, 128).astype(BF16),
        "a0": row(a0),
        "wa": pad_rows(w_aaa_up, 128).astype(BF16),
        "wg": pad_rows(w_gate_up, 256).astype(BF16),
        "k_k": row(k_k), "k_a": row(k_a), "r_k": row(r_k),
        "lnx_g": row(lnx_g), "lnx_b": row(lnx_b),
        "w_branch_rwkv": w_branch_rwkv.astype(BF16),
        "w_branch_sc": w_branch_sc.astype(BF16),
        "conv_sc": conv_sc.astype(F32),
        "w_out": w_out.astype(BF16),
        "norm2_g": row(norm2_g),
        "w_up": w_up.astype(BF16),
        "conv_ffn": conv_ffn.astype(F32),
        "w_down": w_down.astype(BF16),
        "final_norm_g": row(final_norm_g),
    }


def _layer(x, s_wkv, sh_rkv, sh_lora, sc_state, ffn_state, w, *, rows, s, wkv, sc_rows=None, narrow_vg=False):
    ga, gbo, xb, sc_new = _sc_gate(x, sc_state, w, rows=sc_rows or rows, s=s)
    r, lw, k, v, kk, b, g, bonus, shr, shl = _rwkv_prep(xb, sh_rkv, sh_lora, w, rows=rows, s=s, narrow_vg=narrow_vg)
    y, s_new = wkv(r, lw, k, v, kk, b, s_wkv)
    out, ffn_new = _post(x, y, bonus, g, ga, gbo, ffn_state, w, rows=rows, s=s)
    return out, s_new, shr, shl, sc_new, ffn_new


def kernel(x_prompt, x_sample, state_wkv, state_shift, state_sc_conv, state_ffn_conv, meta_tokens,
           norm1_g, w_in, b_gate, mu_shift, w0, w_decay_up, a0, w_aaa_up, w_gate_up, k_k, k_a, r_k,
           lnx_g, lnx_b, w_branch_rwkv, w_branch_sc, conv_sc, w_out, norm2_g, w_up, conv_ffn, w_down,
           final_norm_g):
    w = _prep_weights(norm1_g[0], w_in[0], b_gate[0], mu_shift[0], w0[0], w_decay_up[0], a0[0],
                      w_aaa_up[0], w_gate_up[0], k_k[0], k_a[0], r_k[0], lnx_g[0], lnx_b[0],
                      w_branch_rwkv[0], w_branch_sc[0], conv_sc[0], w_out[0], norm2_g[0], w_up[0],
                      conv_ffn[0], w_down[0], final_norm_g)
    d3 = 3 * D_MODEL

    bp, seq, _ = x_prompt.shape
    zeros = lambda *shape: jnp.zeros(shape, F32)
    _, m_wkv, m_shr, m_shl, m_sc, m_ffn = _layer(
        meta_tokens.astype(F32)[None], zeros(1, N_HEADS, HEAD_SIZE, HEAD_SIZE), zeros(1, 1, d3),
        zeros(1, 1, D_LORA_PAD), zeros(1, 2, D_MODEL), zeros(1, 2, 2 * D_FF), w,
        rows=N_META, s=1, wkv=functools.partial(_wkv, chunk=N_META, nq=1))

    y_prompt, p_wkv, p_shr, p_shl, p_sc, p_ffn = _layer(
        x_prompt, m_wkv, m_shr, m_shl, m_sc, m_ffn, w,
        rows=PROMPT_ROWS, s=1, wkv=functools.partial(_wkv, chunk=WKV_CHUNK, nq=WKV_SEQS),
        sc_rows=PROMPT_SC_ROWS, narrow_vg=True)

    bs, ts, _ = x_sample.shape
    n_grp = SAMPLE_GROUPS
    gsz = bs // n_grp

    def to_rows(t):
        n, c = t.shape[1:]
        return t.reshape(n_grp, gsz, n, c).transpose(0, 2, 1, 3).reshape(n_grp, n * gsz, c)

    sh = state_shift[0].reshape(n_grp, gsz, -1)
    y_s, s_wkv, s_shr, s_shl, s_sc, s_ffn = _layer(
        to_rows(x_sample), state_wkv[0].transpose(1, 2, 3, 0), sh[..., :d3], _pad_lora_cols(sh[..., d3:]),
        to_rows(state_sc_conv[0]), to_rows(state_ffn_conv[0]), w,
        rows=ts * gsz, s=gsz, wkv=functools.partial(_wkv_lanes, n_t=ts))
    s_wkv = s_wkv.transpose(3, 0, 1, 2)

    def rows_to_batch(t, n):
        c = t.shape[-1]
        return t.reshape(n_grp, n, gsz, c).transpose(0, 2, 1, 3).reshape(bs, n, c)

    y_sample = rows_to_batch(y_s, ts)
    shift_p = jnp.concatenate([p_shr[:, 0], _unpad_lora_cols(p_shl[:, 0])], axis=-1)
    shift_s = jnp.concatenate([rows_to_batch(s_shr, 1)[:, 0], _unpad_lora_cols(rows_to_batch(s_shl, 1)[:, 0])],
                              axis=-1)
    return (y_prompt, y_sample,
            p_wkv[None], s_wkv[None],
            shift_p[None], shift_s[None],
            p_sc[None], rows_to_batch(s_sc, 2)[None],
            p_ffn[None], rows_to_batch(s_ffn, 2)[None])
```
